```python
import math
import jax, jax.numpy as jnp
from jax import lax
import numpy as np

D_MODEL = 2048
BATCH = 2
SEQ = 4096
DEPTH = 1

HEAD_DIM = 128
N_MOBA_HEADS = 8
N_FOX_HEADS = 8
MOBA_WIDTH = N_MOBA_HEADS * HEAD_DIM
FOX_WIDTH = N_FOX_HEADS * HEAD_DIM
MOBA_BLOCK = 256
MOBA_TOP_K = 3
MOBA_Q_CHUNK = 32
FOX_Q_BLOCK = 128
ROPE_THETA = 500000.0
ROPE_DIM = HEAD_DIM // 4
D_FF = 4 * D_MODEL
N_BRANCHES = 2
RMS_EPS = 1e-6
NEG_BIG = -1e30
IN_SPLITS = [MOBA_WIDTH, MOBA_WIDTH, MOBA_WIDTH, FOX_WIDTH, FOX_WIDTH, FOX_WIDTH, N_FOX_HEADS, D_MODEL, D_MODEL]
IN_COLS = sum(IN_SPLITS)

kernel_name = "hybrid_moba_fox_gated_block"


def rmsnorm(x, g):
    xf = x.astype(jnp.float32)
    y = xf * lax.rsqrt(jnp.mean(xf * xf, axis=-1, keepdims=True) + RMS_EPS)
    return (y * g.astype(jnp.float32)).astype(x.dtype)


def partial_rotary(t):
    S = t.shape[2]
    half = ROPE_DIM // 2
    inv_freq = ROPE_THETA ** (-jnp.arange(0, ROPE_DIM, 2, dtype=jnp.float32) / ROPE_DIM)
    ang = jnp.arange(S, dtype=jnp.float32)[:, None] * inv_freq[None, :]
    cos, sin = jnp.cos(ang), jnp.sin(ang)
    tf = t.astype(jnp.float32)
    x1, x2, rest = tf[..., :half], tf[..., half:ROPE_DIM], tf[..., ROPE_DIM:]
    out = jnp.concatenate([x1 * cos - x2 * sin, x2 * cos + x1 * sin, rest], axis=-1)
    return out.astype(t.dtype)


def split_heads(t, n_heads):
    B, S, _ = t.shape
    return t.reshape(B, S, n_heads, HEAD_DIM).transpose(0, 2, 1, 3)


def merge_heads(t):
    B, H, S, D = t.shape
    return t.transpose(0, 2, 1, 3).reshape(B, S, H * D)


def moba_attention(q, k, v):
    B, H, S, dh = q.shape
    nb = -(-S // MOBA_BLOCK)
    pad = nb * MOBA_BLOCK - S
    kp = jnp.pad(k, ((0, 0), (0, 0), (0, pad), (0, 0)))
    vp = jnp.pad(v, ((0, 0), (0, 0), (0, pad), (0, 0)))
    kb = kp.reshape(B, H, nb, MOBA_BLOCK, dh)
    vb = vp.reshape(B, H, nb, MOBA_BLOCK, dh)
    kmean = jnp.mean(kb.astype(jnp.float32), axis=3)
    n_sel = min(MOBA_TOP_K, nb)
    scale = dh ** -0.5
    n_chunks = S // MOBA_Q_CHUNK
    qc = q.reshape(B, H, n_chunks, MOBA_Q_CHUNK, dh).transpose(2, 0, 1, 3, 4)
    bi = jnp.arange(B)[:, None, None, None]
    hi = jnp.arange(H)[None, :, None, None]
    block_ids = jnp.arange(nb)
    own_offsets = jnp.arange(MOBA_BLOCK)

    def chunk_fn(args):
        c, qi = args
        q_pos = c * MOBA_Q_CHUNK + jnp.arange(MOBA_Q_CHUNK)
        blk = (c * MOBA_Q_CHUNK) // MOBA_BLOCK
        gate = jnp.einsum('bhqd,bhnd->bhqn', qi.astype(jnp.float32), kmean)
        gate = jnp.where(block_ids < blk, gate, -jnp.inf)
        _, idx = lax.top_k(gate, n_sel)
        sel_valid = idx < blk
        k_sel = kb[bi, hi, idx]
        v_sel = vb[bi, hi, idx]
        s_sel = jnp.einsum('bhqd,bhqnkd->bhqnk', qi, k_sel, preferred_element_type=jnp.float32) * scale
        s_sel = jnp.where(sel_valid[..., None], s_sel, NEG_BIG)
        k_own = lax.dynamic_index_in_dim(kb, blk, axis=2, keepdims=False)
        v_own = lax.dynamic_index_in_dim(vb, blk, axis=2, keepdims=False)
        s_own = jnp.einsum('bhqd,bhkd->bhqk', qi, k_own, preferred_element_type=jnp.float32) * scale
        key_pos = blk * MOBA_BLOCK + own_offsets
        s_own = jnp.where(key_pos[None, :] <= q_pos[:, None], s_own, NEG_BIG)
        logits = jnp.concatenate([s_sel.reshape(B, H, MOBA_Q_CHUNK, n_sel * MOBA_BLOCK), s_own], axis=-1)
        p = jax.nn.softmax(logits, axis=-1).astype(v.dtype)
        p_sel = p[..., :n_sel * MOBA_BLOCK].reshape(B, H, MOBA_Q_CHUNK, n_sel, MOBA_BLOCK)
        p_own = p[..., n_sel * MOBA_BLOCK:]
        return (jnp.einsum('bhqnk,bhqnkd->bhqd', p_sel, v_sel)
                + jnp.einsum('bhqk,bhkd->bhqd', p_own, v_own))

    out = lax.map(chunk_fn, (jnp.arange(n_chunks), qc))
    return out.transpose(1, 2, 0, 3, 4).reshape(B, H, S, dh)


def fox_attention(q, k, v, log_f):
    B, H, S, dh = q.shape
    F = jnp.cumsum(log_f, axis=-1)
    scale = dh ** -0.5
    n_blocks = S // FOX_Q_BLOCK
    qb = q.reshape(B, H, n_blocks, FOX_Q_BLOCK, dh).transpose(2, 0, 1, 3, 4)
    Fq = F.reshape(B, H, n_blocks, FOX_Q_BLOCK).transpose(2, 0, 1, 3)
    key_pos = jnp.arange(S)

    def block_fn(args):
        c, qi, fi = args
        q_pos = c * FOX_Q_BLOCK + jnp.arange(FOX_Q_BLOCK)
        s = jnp.einsum('bhqd,bhkd->bhqk', qi, k, preferred_element_type=jnp.float32) * scale
        s = s + fi[..., :, None] - F[:, :, None, :]
        s = jnp.where(key_pos[None, :] <= q_pos[:, None], s, NEG_BIG)
        p = jax.nn.softmax(s, axis=-1).astype(v.dtype)
        return jnp.einsum('bhqk,bhkd->bhqd', p, v)

    out = lax.map(block_fn, (jnp.arange(n_blocks), qb, Fq))
    return out.transpose(1, 2, 0, 3, 4).reshape(B, H, S, dh)


def setup_inputs(seed: int = 0) -> dict:
    key = jax.random.key(seed)
    ks = jax.random.split(key, 13)
    f32 = jnp.float32
    x = jax.random.normal(ks[0], (BATCH, SEQ, D_MODEL), f32)
    w_in = jax.random.normal(ks[1], (DEPTH, D_MODEL, IN_COLS), f32) * D_MODEL ** -0.5
    b_forget = 3.0 + 0.1 * jax.random.normal(ks[2], (DEPTH, N_FOX_HEADS), f32)
    w_branch_moba = jax.random.normal(ks[3], (DEPTH, MOBA_WIDTH, D_MODEL), f32) * MOBA_WIDTH ** -0.5
    w_branch_fox = jax.random.normal(ks[4], (DEPTH, FOX_WIDTH, D_MODEL), f32) * FOX_WIDTH ** -0.5
    w_out = jax.random.normal(ks[5], (DEPTH, D_MODEL, D_MODEL), f32) * D_MODEL ** -0.5
    g_mix_pre = 1.0 + 0.05 * jax.random.normal(ks[6], (DEPTH, D_MODEL), f32)
    g_mix_post = 1.0 + 0.05 * jax.random.normal(ks[7], (DEPTH, D_MODEL), f32)
    w_up = jax.random.normal(ks[8], (DEPTH, D_MODEL, D_FF), f32) * D_MODEL ** -0.5
    w_down = jax.random.normal(ks[9], (DEPTH, D_FF, D_MODEL), f32) * D_FF ** -0.5
    g_mlp_pre = 1.0 + 0.05 * jax.random.normal(ks[10], (DEPTH, D_MODEL), f32)
    g_mlp_post = 1.0 + 0.05 * jax.random.normal(ks[11], (DEPTH, D_MODEL), f32)
    return {"x": x, "w_in": w_in, "b_forget": b_forget, "w_branch_moba": w_branch_moba,
            "w_branch_fox": w_branch_fox, "w_out": w_out, "g_mix_pre": g_mix_pre,
            "g_mix_post": g_mix_post, "w_up": w_up, "w_down": w_down,
            "g_mlp_pre": g_mlp_pre, "g_mlp_post": g_mlp_post}


def reference(x, w_in, b_forget, w_branch_moba, w_branch_fox, w_out, g_mix_pre, g_mix_post,
              w_up, w_down, g_mlp_pre, g_mlp_post):
    split_points = list(np.cumsum(IN_SPLITS)[:-1])
    for l in range(DEPTH):
        h = rmsnorm(x, g_mix_pre[l])
        proj = jnp.einsum('bsd,dc->bsc', h, w_in[l])
        mq, mk, mv, fq, fk, fv, ff, ga, gb = jnp.split(proj, split_points, axis=-1)
        mq = partial_rotary(split_heads(mq, N_MOBA_HEADS))
        mk = partial_rotary(split_heads(mk, N_MOBA_HEADS))
        o_moba = merge_heads(moba_attention(mq, mk, split_heads(mv, N_MOBA_HEADS)))
        log_f = jax.nn.log_sigmoid((ff + b_forget[l]).astype(jnp.float32)).transpose(0, 2, 1)
        o_fox = merge_heads(fox_attention(split_heads(fq, N_FOX_HEADS), split_heads(fk, N_FOX_HEADS),
                                          split_heads(fv, N_FOX_HEADS), log_f))
        y_moba = jnp.einsum('bsc,cd->bsd', o_moba, w_branch_moba[l])
        y_fox = jnp.einsum('bsc,cd->bsd', o_fox, w_branch_fox[l])
        merged = jax.nn.sigmoid(ga) * y_moba + jax.nn.sigmoid(gb) * y_fox
        mixed = jnp.einsum('bsd,de->bse', merged, w_out[l])
        x = x + rmsnorm(mixed, g_mix_post[l])
        h = rmsnorm(x, g_mlp_pre[l])
        u = jnp.einsum('bsd,df->bsf', h, w_up[l])
        m = jnp.einsum('bsf,fd->bsd', jnp.square(jax.nn.relu(u)), w_down[l])
        x = x + rmsnorm(m, g_mlp_post[l])
    return x
```

```python
import functools

import jax
import jax.numpy as jnp
from jax import lax
from jax.experimental import pallas as pl
from jax.experimental.pallas import tpu as pltpu

D_MODEL = 2048
HEAD_DIM = 128
N_HEADS = 8
WIDTH = N_HEADS * HEAD_DIM
MOBA_BLOCK = 256
MOBA_TOP_K = 3
ROPE_THETA = 500000.0
ROPE_DIM = HEAD_DIM // 4
ROPE_HALF = ROPE_DIM // 2
D_FF = 4 * D_MODEL
RMS_EPS = 1e-6
NEG_BIG = -1e30
LANES = 128

QKV_COLS = 6 * WIDTH
PROJ_COLS = QKV_COLS + 2 * D_MODEL

VMEM_LIMIT = 56 * 1024 * 1024

BF16 = jnp.bfloat16
F32 = jnp.float32


def _rms_scale(x):
    return lax.rsqrt(jnp.mean(x * x, axis=-1, keepdims=True) + RMS_EPS)


def _split3(x):
    hi = x.astype(BF16)
    r1 = x - hi.astype(F32)
    mid = r1.astype(BF16)
    lo = (r1 - mid.astype(F32)).astype(BF16)
    return hi, mid, lo


def _dot(a, b):
    return jnp.dot(a, b, preferred_element_type=F32)


def _dot_nt(a, b):
    return lax.dot_general(a, b, (((1,), (1,)), ((), ())), preferred_element_type=F32)


IN_TM = 1024
IN_TN = 1024


def _in_proj_kernel(x_ref, g_ref, w_ref, wff_ref, cos_ref, sina_ref, sinb_ref,
                    proj_ref, ff_ref, kmean_ref, h_ref):
    j = pl.program_id(1)

    @pl.when(j == 0)
    def _():
        x = x_ref[...]
        h = (x * _rms_scale(x) * g_ref[...]).astype(BF16)
        h_ref[...] = h
        ff_ref[...] = _dot(h, wff_ref[...])

    acc = _dot(h_ref[...], w_ref[...])

    def rotary(t):
        outs = []
        for hd in range(IN_TN // HEAD_DIM):
            th = t[:, hd * HEAD_DIM:(hd + 1) * HEAD_DIM]
            up = pltpu.roll(th, HEAD_DIM - ROPE_HALF, 1)
            dn = pltpu.roll(th, ROPE_HALF, 1)
            outs.append(th * cos_ref[...] + up * sina_ref[...] + dn * sinb_ref[...])
        return jnp.concatenate(outs, axis=1)

    scale = HEAD_DIM ** -0.5

    @pl.when(j == 0)
    def _():
        proj_ref[...] = (rotary(acc) * scale).astype(BF16)

    @pl.when(j == 1)
    def _():
        kr = rotary(acc)
        proj_ref[...] = kr.astype(BF16)
        nblk = IN_TM // MOBA_BLOCK
        means = [jnp.mean(kr[b * MOBA_BLOCK:(b + 1) * MOBA_BLOCK, :], axis=0, keepdims=True)
                 for b in range(nblk)]
        kmean_ref[0] = jnp.concatenate(means, axis=0)

    @pl.when(j == 3)
    def _():
        proj_ref[...] = (acc * scale).astype(BF16)

    @pl.when((j == 2) | (j >= 4))
    def _():
        proj_ref[...] = acc.astype(BF16)


def _in_proj(x2, g, w, wff, cos_t, sina_t, sinb_t, seq):
    n = x2.shape[0]
    seq_tiles = seq // IN_TM
    grid = (n // IN_TM, PROJ_COLS // IN_TN)
    return pl.pallas_call(
        _in_proj_kernel,
        grid=grid,
        in_specs=[
            pl.BlockSpec((IN_TM, D_MODEL), lambda i, j: (i, 0)),
            pl.BlockSpec((1, D_MODEL), lambda i, j: (0, 0)),
            pl.BlockSpec((D_MODEL, IN_TN), lambda i, j: (0, j)),
            pl.BlockSpec((D_MODEL, LANES), lambda i, j: (0, 0)),
            pl.BlockSpec((IN_TM, HEAD_DIM), lambda i, j: (i % seq_tiles, 0)),
            pl.BlockSpec((IN_TM, HEAD_DIM), lambda i, j: (i % seq_tiles, 0)),
            pl.BlockSpec((IN_TM, HEAD_DIM), lambda i, j: (i % seq_tiles, 0)),
        ],
        out_specs=[
            pl.BlockSpec((IN_TM, IN_TN), lambda i, j: (i, j)),
            pl.BlockSpec((IN_TM, LANES), lambda i, j: (i, 0)),
            pl.BlockSpec((1, IN_TM // MOBA_BLOCK, WIDTH), lambda i, j: (i, 0, 0)),
        ],
        out_shape=[
            jax.ShapeDtypeStruct((n, PROJ_COLS), BF16),
            jax.ShapeDtypeStruct((n, LANES), F32),
            jax.ShapeDtypeStruct((n // IN_TM, IN_TM // MOBA_BLOCK, WIDTH), F32),
        ],
        scratch_shapes=[pltpu.VMEM((IN_TM, D_MODEL), BF16)],
        compiler_params=pltpu.CompilerParams(
            dimension_semantics=("arbitrary", "arbitrary"),
            vmem_limit_bytes=VMEM_LIMIT),
        name="in_proj",
    )(x2, g, w, wff, cos_t, sina_t, sinb_t)


FB_CHUNK = 256


def _fox_bias_kernel(ff_ref, b_ref, out_ref):
    seq = ff_ref.shape[0]
    r = lax.broadcasted_iota(jnp.int32, (FB_CHUNK, FB_CHUNK), 0)
    c = lax.broadcasted_iota(jnp.int32, (FB_CHUNK, FB_CHUNK), 1)
    ltri = (c <= r).astype(BF16)
    pr = lax.broadcasted_iota(jnp.int32, (LANES, LANES), 0)
    pc = lax.broadcasted_iota(jnp.int32, (LANES, LANES), 1)
    place = [((pc == 3 * pr + t) & (pr < N_HEADS)).astype(BF16) for t in range(3)]

    carry = jnp.zeros((1, LANES), F32)
    for ci in range(seq // FB_CHUNK):
        z = ff_ref[ci * FB_CHUNK:(ci + 1) * FB_CHUNK, :] + b_ref[...]
        logf = -(jnp.maximum(-z, 0.0) + jnp.log1p(jnp.exp(-jnp.abs(z))))
        hi, mid, lo = _split3(logf)
        f = _dot(ltri, hi) + _dot(ltri, mid) + _dot(ltri, lo) + carry
        carry = f[FB_CHUNK - 1:FB_CHUNK, :]
        nh, nm, nl = _split3(-f)
        aug = _dot(nh, place[0]) + _dot(nm, place[1]) + _dot(nl, place[2])
        out_ref[ci * FB_CHUNK:(ci + 1) * FB_CHUNK, :] = aug.astype(BF16)


def _fox_bias(ff, b_row, batch, seq):
    return pl.pallas_call(
        _fox_bias_kernel,
        grid=(batch,),
        in_specs=[
            pl.BlockSpec((seq, LANES), lambda b: (b, 0)),
            pl.BlockSpec((1, LANES), lambda b: (0, 0)),
        ],
        out_specs=pl.BlockSpec((seq, LANES), lambda b: (b, 0)),
        out_shape=jax.ShapeDtypeStruct((batch * seq, LANES), BF16),
        compiler_params=pltpu.CompilerParams(
            dimension_semantics=("arbitrary",), vmem_limit_bytes=VMEM_LIMIT),
        name="fox_bias",
    )(ff, b_row)


ATT_T = 256


def _transpose_v(v_ref, vt_ref, seq):
    for ci in range(seq // ATT_T):
        blk = v_ref[ci * ATT_T:(ci + 1) * ATT_T, :].astype(F32)
        vt_ref[:, ci * ATT_T:(ci + 1) * ATT_T] = blk.T.astype(BF16)


def _causal_keep():
    key = lax.broadcasted_iota(jnp.int32, (ATT_T, ATT_T), 0)
    qry = lax.broadcasted_iota(jnp.int32, (ATT_T, ATT_T), 1)
    return key <= qry


def _softmax_first(s, vt_blk):
    m = jnp.max(s, axis=0, keepdims=True)
    p = jnp.exp(s - m)
    l = jnp.sum(p, axis=0, keepdims=True)
    acc = _dot(vt_blk, p.astype(BF16))
    return m, l, acc


def _softmax_step(s, vt_blk, m, l, acc):
    m_new = jnp.maximum(m, jnp.max(s, axis=0, keepdims=True))
    alpha = jnp.exp(m - m_new)
    p = jnp.exp(s - m_new)
    l = alpha * l + jnp.sum(p, axis=0, keepdims=True)
    acc = alpha * acc + _dot(vt_blk, p.astype(BF16))
    return m_new, l, acc


def _moba_kernel(q_ref, k_ref, v_ref, kmean_ref, o_ref, vt_ref, sel_ref):
    qi = pl.program_id(2)
    seq = k_ref.shape[0]
    nblk = seq // MOBA_BLOCK

    @pl.when(qi == 0)
    def _():
        _transpose_v(v_ref, vt_ref, seq)

    q = q_ref[...]

    kh, km, kl = _split3(kmean_ref[...])
    gate = _dot_nt(kh, q) + _dot_nt(km, q) + _dot_nt(kl, q)
    blk_id = lax.broadcasted_iota(jnp.int32, (nblk, ATT_T), 0)
    past = blk_id < qi
    gate = jnp.where(past, gate, -jnp.inf)
    rank = jnp.zeros((nblk, ATT_T), jnp.int32)
    for mth in range(nblk):
        gm = gate[mth:mth + 1, :]
        ahead = (gm > gate) | ((gm == gate) & (mth < blk_id))
        rank = rank + ahead.astype(jnp.int32)
    sel_ref[...] = ((rank < MOBA_TOP_K) & past).astype(F32)

    own = pl.multiple_of(qi * ATT_T, ATT_T)
    s = _dot_nt(k_ref[pl.ds(own, ATT_T), :], q)
    s = jnp.where(_causal_keep(), s, NEG_BIG)
    m, l, acc = _softmax_first(s, vt_ref[:, pl.ds(own, ATT_T)])

    def body(nb, carry):
        m, l, acc = carry
        start = pl.multiple_of(nb * ATT_T, ATT_T)
        s = _dot_nt(k_ref[pl.ds(start, ATT_T), :], q)
        keep = sel_ref[pl.ds(nb, 1), :] > 0.5
        s = jnp.where(keep, s, NEG_BIG)
        return _softmax_step(s, vt_ref[:, pl.ds(start, ATT_T)], m, l, acc)

    m, l, acc = lax.fori_loop(0, qi, body, (m, l, acc))
    o_ref[...] = (acc / l).T.astype(BF16)


def _fox_kernel(q_ref, k_ref, v_ref, fa_ref, o_ref, vt_ref):
    h = pl.program_id(1)
    qi = pl.program_id(2)
    seq = k_ref.shape[0]

    @pl.when(qi == 0)
    def _():
        _transpose_v(v_ref, vt_ref, seq)

    lane = lax.broadcasted_iota(jnp.int32, (ATT_T, LANES), 1)
    pick = ((lane >= 3 * h) & (lane < 3 * h + 3)).astype(BF16)
    q = jnp.concatenate([q_ref[...], pick], axis=1)

    def keys(start):
        return jnp.concatenate([k_ref[pl.ds(start, ATT_T), :], fa_ref[pl.ds(start, ATT_T), :]], axis=1)

    own = pl.multiple_of(qi * ATT_T, ATT_T)
    s = _dot_nt(keys(own), q)
    s = jnp.where(_causal_keep(), s, NEG_BIG)
    m, l, acc = _softmax_first(s, vt_ref[:, pl.ds(own, ATT_T)])

    def body(nb, carry):
        m, l, acc = carry
        start = pl.multiple_of(nb * ATT_T, ATT_T)
        s = _dot_nt(keys(start), q)
        return _softmax_step(s, vt_ref[:, pl.ds(start, ATT_T)], m, l, acc)

    m, l, acc = lax.fori_loop(0, qi, body, (m, l, acc))
    o_ref[...] = (acc / l).T.astype(BF16)


def _attention(proj, kmean, faug, batch, seq):
    n = batch * seq
    nq = seq // ATT_T
    nblk = seq // MOBA_BLOCK
    grid = (batch, N_HEADS, nq)
    params = pltpu.CompilerParams(
        dimension_semantics=("arbitrary", "arbitrary", "arbitrary"),
        vmem_limit_bytes=VMEM_LIMIT)

    def q_spec(col0):
        return pl.BlockSpec((ATT_T, HEAD_DIM), lambda b, h, i: (b * nq + i, col0 + h))

    def kv_spec(col0):
        return pl.BlockSpec((seq, HEAD_DIM), lambda b, h, i: (b, col0 + h))

    out_spec = pl.BlockSpec((ATT_T, HEAD_DIM), lambda b, h, i: (b * nq + i, h))
    out_shape = jax.ShapeDtypeStruct((n, WIDTH), BF16)

    o_moba = pl.pallas_call(
        _moba_kernel,
        grid=grid,
        in_specs=[q_spec(0), kv_spec(N_HEADS), kv_spec(2 * N_HEADS),
                  pl.BlockSpec((nblk, HEAD_DIM), lambda b, h, i: (b, h))],
        out_specs=out_spec,
        out_shape=out_shape,
        scratch_shapes=[pltpu.VMEM((HEAD_DIM, seq), BF16), pltpu.VMEM((nblk, ATT_T), F32)],
        compiler_params=params,
        name="moba_attn",
    )(proj, proj, proj, kmean)

    o_fox = pl.pallas_call(
        _fox_kernel,
        grid=grid,
        in_specs=[q_spec(3 * N_HEADS), kv_spec(4 * N_HEADS), kv_spec(5 * N_HEADS),
                  pl.BlockSpec((seq, LANES), lambda b, h, i: (b, 0))],
        out_specs=out_spec,
        out_shape=out_shape,
        scratch_shapes=[pltpu.VMEM((HEAD_DIM, seq), BF16)],
        compiler_params=params,
        name="fox_attn",
    )(proj, proj, proj, faug)
    return o_moba, o_fox


MIX_TM = 256


def _sigmoid(z):
    return 1.0 / (1.0 + jnp.exp(-z))


def _mix_kernel(om_ref, of_ref, ga_ref, gb_ref, x_ref, wm_ref, wf_ref, wo_ref, g_ref, out_ref):
    y_m = _dot(om_ref[...], wm_ref[...])
    y_f = _dot(of_ref[...], wf_ref[...])
    merged = _sigmoid(ga_ref[...].astype(F32)) * y_m + _sigmoid(gb_ref[...].astype(F32)) * y_f
    mixed = _dot(merged.astype(BF16), wo_ref[...])
    out_ref[...] = x_ref[...] + mixed * _rms_scale(mixed) * g_ref[...]


def _mix_out(o_moba, o_fox, proj, x2, wm, wf, wo, g):
    n = x2.shape[0]
    ga_blk = QKV_COLS // D_MODEL
    const = dict(pipeline_mode=pl.Buffered(1))
    return pl.pallas_call(
        _mix_kernel,
        grid=(n // MIX_TM,),
        in_specs=[
            pl.BlockSpec((MIX_TM, WIDTH), lambda i: (i, 0)),
            pl.BlockSpec((MIX_TM, WIDTH), lambda i: (i, 0)),
            pl.BlockSpec((MIX_TM, D_MODEL), lambda i: (i, ga_blk)),
            pl.BlockSpec((MIX_TM, D_MODEL), lambda i: (i, ga_blk + 1)),
            pl.BlockSpec((MIX_TM, D_MODEL), lambda i: (i, 0)),
            pl.BlockSpec((WIDTH, D_MODEL), lambda i: (0, 0), **const),
            pl.BlockSpec((WIDTH, D_MODEL), lambda i: (0, 0), **const),
            pl.BlockSpec((D_MODEL, D_MODEL), lambda i: (0, 0), **const),
            pl.BlockSpec((1, D_MODEL), lambda i: (0, 0)),
        ],
        out_specs=pl.BlockSpec((MIX_TM, D_MODEL), lambda i: (i, 0)),
        out_shape=jax.ShapeDtypeStruct((n, D_MODEL), F32),
        compiler_params=pltpu.CompilerParams(
            dimension_semantics=("arbitrary",), vmem_limit_bytes=VMEM_LIMIT),
        name="mix_out",
    )(o_moba, o_fox, proj, proj, x2, wm, wf, wo, g)


MLP_TM = 512
MLP_TF = 1024


def _mlp_kernel(x_ref, gpre_ref, wu_ref, wd_ref, gpost_ref, out_ref, h_ref, acc_ref):
    f = pl.program_id(1)

    @pl.when(f == 0)
    def _():
        x = x_ref[...]
        h_ref[...] = (x * _rms_scale(x) * gpre_ref[...]).astype(BF16)
        acc_ref[...] = jnp.zeros_like(acc_ref)

    u = _dot(h_ref[...], wu_ref[...])
    a = jnp.square(jnp.maximum(u, 0.0)).astype(BF16)
    acc_ref[...] += _dot(a, wd_ref[...])

    @pl.when(f == pl.num_programs(1) - 1)
    def _():
        mo = acc_ref[...]
        out_ref[...] = x_ref[...] + mo * _rms_scale(mo) * gpost_ref[...]


def _mlp(x1, gpre, wu, wd, gpost):
    n = x1.shape[0]
    return pl.pallas_call(
        _mlp_kernel,
        grid=(n // MLP_TM, D_FF // MLP_TF),
        in_specs=[
            pl.BlockSpec((MLP_TM, D_MODEL), lambda i, f: (i, 0)),
            pl.BlockSpec((1, D_MODEL), lambda i, f: (0, 0)),
            pl.BlockSpec((D_MODEL, MLP_TF), lambda i, f: (0, f)),
            pl.BlockSpec((MLP_TF, D_MODEL), lambda i, f: (f, 0)),
            pl.BlockSpec((1, D_MODEL), lambda i, f: (0, 0)),
        ],
        out_specs=pl.BlockSpec((MLP_TM, D_MODEL), lambda i, f: (i, 0)),
        out_shape=jax.ShapeDtypeStruct((n, D_MODEL), F32),
        scratch_shapes=[pltpu.VMEM((MLP_TM, D_MODEL), BF16), pltpu.VMEM((MLP_TM, D_MODEL), F32)],
        compiler_params=pltpu.CompilerParams(
            dimension_semantics=("arbitrary", "arbitrary"), vmem_limit_bytes=VMEM_LIMIT),
        name="mlp",
    )(x1, gpre, wu, wd, gpost)


def _rope_tables(seq):
    inv_freq = ROPE_THETA ** (-jnp.arange(0, ROPE_DIM, 2, dtype=F32) / ROPE_DIM)
    ang = jnp.arange(seq, dtype=F32)[:, None] * inv_freq[None, :]
    cos, sin = jnp.cos(ang), jnp.sin(ang)
    rest = HEAD_DIM - ROPE_DIM
    cos_t = jnp.concatenate([cos, cos, jnp.ones((seq, rest), F32)], axis=1)
    zeros_h = jnp.zeros((seq, ROPE_HALF), F32)
    zeros_r = jnp.zeros((seq, rest), F32)
    sina_t = jnp.concatenate([-sin, zeros_h, zeros_r], axis=1)
    sinb_t = jnp.concatenate([zeros_h, sin, zeros_r], axis=1)
    return cos_t, sina_t, sinb_t


def _layer(x2, w_in, b_forget, w_bm, w_bf, w_out, g_mix_pre, g_mix_post,
           w_up, w_down, g_mlp_pre, g_mlp_post, batch, seq):
    ff0 = QKV_COLS
    ff1 = QKV_COLS + N_HEADS
    w = jnp.concatenate([w_in[:, :ff0], w_in[:, ff1:]], axis=1).astype(BF16)
    wff = jnp.pad(w_in[:, ff0:ff1], ((0, 0), (0, LANES - N_HEADS))).astype(BF16)
    b_row = jnp.pad(b_forget, (0, LANES - N_HEADS)).reshape(1, LANES)
    cos_t, sina_t, sinb_t = _rope_tables(seq)

    proj, ff, kmean = _in_proj(x2, g_mix_pre.reshape(1, -1), w, wff, cos_t, sina_t, sinb_t, seq)
    faug = _fox_bias(ff, b_row, batch, seq)
    kmean = kmean.reshape(batch * (seq // MOBA_BLOCK), WIDTH)
    o_moba, o_fox = _attention(proj, kmean, faug, batch, seq)
    x1 = _mix_out(o_moba, o_fox, proj, x2, w_bm.astype(BF16), w_bf.astype(BF16),
                  w_out.astype(BF16), g_mix_post.reshape(1, -1))
    return _mlp(x1, g_mlp_pre.reshape(1, -1), w_up.astype(BF16), w_down.astype(BF16),
                g_mlp_post.reshape(1, -1))


def kernel(x, w_in, b_forget, w_branch_moba, w_branch_fox, w_out, g_mix_pre, g_mix_post,
           w_up, w_down, g_mlp_pre, g_mlp_post):
    batch, seq, d = x.shape
    assert d == D_MODEL and seq % IN_TM == 0 and w_in.shape[-1] == PROJ_COLS + N_HEADS
    x2 = x.reshape(batch * seq, d)
    for l in range(w_in.shape[0]):
        x2 = _layer(x2, w_in[l], b_forget[l], w_branch_moba[l], w_branch_fox[l], w_out[l],
                    g_mix_pre[l], g_mix_post[l], w_up[l], w_down[l], g_mlp_pre[l],
                    g_mlp_post[l], batch, seq)
    return x2.reshape(batch, seq, d)
```

```python
import math

import jax
import jax.numpy as jnp
from jax import lax
from jax.experimental import pallas as pl
from jax.experimental.pallas import tpu as pltpu

D_MODEL = 2048
HEAD_DIM = 128
N_HEADS = 8
WIDTH = N_HEADS * HEAD_DIM
MOBA_BLOCK = 256
MOBA_TOP_K = 3
ROPE_THETA = 500000.0
ROPE_DIM = HEAD_DIM // 4
ROPE_HALF = ROPE_DIM // 2
D_FF = 4 * D_MODEL
RMS_EPS = 1e-6
NEG_BIG = -1e30
LANES = 128

QKV_COLS = 6 * WIDTH
GATE_COLS = 2 * D_MODEL
PROJ_COLS = QKV_COLS + GATE_COLS

VMEM_LIMIT = 56 * 1024 * 1024

BF16 = jnp.bfloat16
F32 = jnp.float32
LOG2E = math.log2(math.e)
Q_SCALE = HEAD_DIM ** -0.5 * LOG2E


def _rms_scale(x):
    return lax.rsqrt(jnp.mean(x * x, axis=-1, keepdims=True) + RMS_EPS)


def _split3(x):
    hi = x.astype(BF16)
    r1 = x - hi.astype(F32)
    mid = r1.astype(BF16)
    lo = (r1 - mid.astype(F32)).astype(BF16)
    return hi, mid, lo


def _dot(a, b):
    return jnp.dot(a, b, preferred_element_type=F32)


def _dot_nt(a, b):
    return lax.dot_general(a, b, (((1,), (1,)), ((), ())), preferred_element_type=F32)


IN_TM = 1024
IN_TN = 1024
IN_QKV_TILES = QKV_COLS // IN_TN


def _in_proj_kernel(x_ref, g_ref, wq_ref, wg_ref, wff_ref, cos_ref, sina_ref, sinb_ref,
                    proj_ref, ff_ref, kmean_ref, h_ref):
    j = pl.program_id(1)

    @pl.when(j == 0)
    def _():
        x = x_ref[...]
        h = (x * _rms_scale(x) * g_ref[...]).astype(BF16)
        h_ref[...] = h
        ff_ref[...] = _dot(h, wff_ref[...])

    def rotary(t):
        outs = []
        for hd in range(IN_TN // HEAD_DIM):
            th = t[:, hd * HEAD_DIM:(hd + 1) * HEAD_DIM]
            up = pltpu.roll(th, HEAD_DIM - ROPE_HALF, 1)
            dn = pltpu.roll(th, ROPE_HALF, 1)
            outs.append(th * cos_ref[...] + up * sina_ref[...] + dn * sinb_ref[...])
        return jnp.concatenate(outs, axis=1)

    @pl.when(j == 0)
    def _():
        acc = _dot(h_ref[...], wq_ref[...])
        proj_ref[...] = (rotary(acc) * Q_SCALE).astype(BF16)

    @pl.when(j == 1)
    def _():
        kr = rotary(_dot(h_ref[...], wq_ref[...]))
        proj_ref[...] = kr.astype(BF16)
        nblk = IN_TM // MOBA_BLOCK
        means = [jnp.mean(kr[b * MOBA_BLOCK:(b + 1) * MOBA_BLOCK, :], axis=0, keepdims=True)
                 for b in range(nblk)]
        kmean_ref[0] = jnp.concatenate(means, axis=0)

    @pl.when(j == 3)
    def _():
        proj_ref[...] = (_dot(h_ref[...], wq_ref[...]) * Q_SCALE).astype(BF16)

    @pl.when((j == 2) | (j == 4) | (j == 5))
    def _():
        proj_ref[...] = _dot(h_ref[...], wq_ref[...]).astype(BF16)

    @pl.when(j >= IN_QKV_TILES)
    def _():
        proj_ref[...] = _dot(h_ref[...], wg_ref[...]).astype(BF16)


def _in_proj(x2, g, wq, wg, wff, cos_t, sina_t, sinb_t, seq):
    n = x2.shape[0]
    seq_tiles = seq // IN_TM
    grid = (n // IN_TM, PROJ_COLS // IN_TN)
    last_q = IN_QKV_TILES - 1
    return pl.pallas_call(
        _in_proj_kernel,
        grid=grid,
        in_specs=[
            pl.BlockSpec((IN_TM, D_MODEL), lambda i, j: (i, 0)),
            pl.BlockSpec((1, D_MODEL), lambda i, j: (0, 0)),
            pl.BlockSpec((D_MODEL, IN_TN), lambda i, j: (0, jnp.minimum(j, last_q))),
            pl.BlockSpec((D_MODEL, IN_TN), lambda i, j: (0, jnp.maximum(j - IN_QKV_TILES, 0))),
            pl.BlockSpec((D_MODEL, LANES), lambda i, j: (0, 0)),
            pl.BlockSpec((IN_TM, HEAD_DIM), lambda i, j: (i % seq_tiles, 0)),
            pl.BlockSpec((IN_TM, HEAD_DIM), lambda i, j: (i % seq_tiles, 0)),
            pl.BlockSpec((IN_TM, HEAD_DIM), lambda i, j: (i % seq_tiles, 0)),
        ],
        out_specs=[
            pl.BlockSpec((IN_TM, IN_TN), lambda i, j: (i, j)),
            pl.BlockSpec((IN_TM, LANES), lambda i, j: (i, 0)),
            pl.BlockSpec((1, IN_TM // MOBA_BLOCK, WIDTH), lambda i, j: (i, 0, 0)),
        ],
        out_shape=[
            jax.ShapeDtypeStruct((n, PROJ_COLS), BF16),
            jax.ShapeDtypeStruct((n, LANES), F32),
            jax.ShapeDtypeStruct((n // IN_TM, IN_TM // MOBA_BLOCK, WIDTH), F32),
        ],
        scratch_shapes=[pltpu.VMEM((IN_TM, D_MODEL), BF16)],
        compiler_params=pltpu.CompilerParams(
            dimension_semantics=("arbitrary", "arbitrary"),
            vmem_limit_bytes=VMEM_LIMIT),
        name="in_proj",
    )(x2, g, wq, wg, wff, cos_t, sina_t, sinb_t)


FB_CHUNK = 256


def _fox_bias_kernel(ff_ref, b_ref, out_ref):
    seq = ff_ref.shape[0]
    r = lax.broadcasted_iota(jnp.int32, (FB_CHUNK, FB_CHUNK), 0)
    c = lax.broadcasted_iota(jnp.int32, (FB_CHUNK, FB_CHUNK), 1)
    ltri = (c <= r).astype(BF16)
    pr = lax.broadcasted_iota(jnp.int32, (LANES, LANES), 0)
    pc = lax.broadcasted_iota(jnp.int32, (LANES, LANES), 1)
    place = [((pc == 3 * pr + t) & (pr < N_HEADS)).astype(BF16) for t in range(3)]

    carry = jnp.zeros((1, LANES), F32)
    for ci in range(seq // FB_CHUNK):
        z = ff_ref[ci * FB_CHUNK:(ci + 1) * FB_CHUNK, :] + b_ref[...]
        logf = -(jnp.maximum(-z, 0.0) + jnp.log1p(jnp.exp(-jnp.abs(z))))
        hi, mid, lo = _split3(logf)
        f = _dot(ltri, hi) + _dot(ltri, mid) + _dot(ltri, lo) + carry
        carry = f[FB_CHUNK - 1:FB_CHUNK, :]
        nh, nm, nl = _split3(f * (-LOG2E))
        aug = _dot(nh, place[0]) + _dot(nm, place[1]) + _dot(nl, place[2])
        out_ref[ci * FB_CHUNK:(ci + 1) * FB_CHUNK, :] = aug.astype(BF16)


def _fox_bias(ff, b_row, batch, seq):
    return pl.pallas_call(
        _fox_bias_kernel,
        grid=(batch,),
        in_specs=[
            pl.BlockSpec((seq, LANES), lambda b: (b, 0)),
            pl.BlockSpec((1, LANES), lambda b: (0, 0)),
        ],
        out_specs=pl.BlockSpec((seq, LANES), lambda b: (b, 0)),
        out_shape=jax.ShapeDtypeStruct((batch * seq, LANES), BF16),
        compiler_params=pltpu.CompilerParams(
            dimension_semantics=("arbitrary",), vmem_limit_bytes=VMEM_LIMIT),
        name="fox_bias",
    )(ff, b_row)


ATT_T = 512
ATT_HG = 2
ATT_SUB = ATT_T // MOBA_BLOCK


def _attn_kernel(is_fox, q_ref, k_ref, v_ref, x_ref, o_ref, vt_ref, acc_ref, *maybe_sel):
    hg = pl.program_id(1)
    qt = pl.program_id(2)
    seq = k_ref.shape[0]
    nblk = seq // MOBA_BLOCK
    t = ATT_T

    def head_cols(hh):
        return slice(hh * HEAD_DIM, (hh + 1) * HEAD_DIM)

    @pl.when(qt == 0)
    def _():
        for hh in range(ATT_HG):
            for ci in range(nblk):
                rows = slice(ci * MOBA_BLOCK, (ci + 1) * MOBA_BLOCK)
                blk = v_ref[rows, head_cols(hh)].astype(F32)
                vt_ref[head_cols(hh), rows] = blk.T.astype(BF16)

    qs = []
    for hh in range(ATT_HG):
        qh = q_ref[:, head_cols(hh)]
        if is_fox:
            head = hg * ATT_HG + hh
            lane = lax.broadcasted_iota(jnp.int32, (t, LANES), 1)
            pick = ((lane >= 3 * head) & (lane < 3 * head + 3)).astype(BF16)
            qh = jnp.concatenate([qh, pick], axis=1)
        qs.append(qh)

    if not is_fox:
        sel_ref, = maybe_sel
        blk_id = lax.broadcasted_iota(jnp.int32, (nblk, t), 0)
        q_blk = qt * ATT_SUB + lax.broadcasted_iota(jnp.int32, (nblk, t), 1) // MOBA_BLOCK
        past = blk_id < q_blk
        for hh in range(ATT_HG):
            kh, km, kl = _split3(x_ref[:, head_cols(hh)])
            gate = _dot_nt(kh, qs[hh]) + _dot_nt(km, qs[hh]) + _dot_nt(kl, qs[hh])
            gate = jnp.where(past, gate, -jnp.inf)
            rank = jnp.zeros((nblk, t), jnp.int32)
            for mth in range(nblk):
                gm = gate[mth:mth + 1, :]
                ahead = (gm > gate) | ((gm == gate) & (mth < blk_id))
                rank = rank + ahead.astype(jnp.int32)
            sel_ref[hh] = ((rank < MOBA_TOP_K) & past).astype(F32)

    def scores(hh, start):
        kb = k_ref[pl.ds(start, t), head_cols(hh)]
        if is_fox:
            kb = jnp.concatenate([kb, x_ref[pl.ds(start, t), :]], axis=1)
        return _dot_nt(kb, qs[hh])

    def mask_rows(hh, s, first_blk, diagonal):
        parts = []
        for r in range(ATT_SUB):
            keep = sel_ref[hh, pl.ds(first_blk + r, 1), :] > 0.5
            if diagonal:
                key = r * MOBA_BLOCK + lax.broadcasted_iota(jnp.int32, (MOBA_BLOCK, t), 0)
                qry = lax.broadcasted_iota(jnp.int32, (MOBA_BLOCK, t), 1)
                keep = keep | ((key <= qry) & (qry < (r + 1) * MOBA_BLOCK))
            parts.append(jnp.where(keep, s[r * MOBA_BLOCK:(r + 1) * MOBA_BLOCK, :], NEG_BIG))
        return jnp.concatenate(parts, axis=0)

    def vt_chunk(hh, start):
        return vt_ref[head_cols(hh), pl.ds(start, t)]

    own = pl.multiple_of(qt * t, t)
    stats = []
    for hh in range(ATT_HG):
        s = scores(hh, own)
        if is_fox:
            key = lax.broadcasted_iota(jnp.int32, (t, t), 0)
            qry = lax.broadcasted_iota(jnp.int32, (t, t), 1)
            s = jnp.where(key <= qry, s, NEG_BIG)
        else:
            s = mask_rows(hh, s, qt * ATT_SUB, True)
        m = jnp.max(s, axis=0, keepdims=True)
        p = jnp.exp2(s - m)
        l = jnp.sum(p, axis=0, keepdims=True)
        acc_ref[head_cols(hh), :] = _dot(vt_chunk(hh, own), p.astype(BF16))
        stats.append((m, l))

    def body(c, stats):
        start = pl.multiple_of(c * t, t)
        new = []
        for hh in range(ATT_HG):
            m, l = stats[hh]
            s = scores(hh, start)
            if not is_fox:
                s = mask_rows(hh, s, c * ATT_SUB, False)
            m_new = jnp.maximum(m, jnp.max(s, axis=0, keepdims=True))
            alpha = jnp.exp2(m - m_new)
            p = jnp.exp2(s - m_new)
            l = alpha * l + jnp.sum(p, axis=0, keepdims=True)
            acc_ref[head_cols(hh), :] = (alpha * acc_ref[head_cols(hh), :]
                                         + _dot(vt_chunk(hh, start), p.astype(BF16)))
            new.append((m_new, l))
        return tuple(new)

    stats = lax.fori_loop(0, qt, body, tuple(stats))
    for hh in range(ATT_HG):
        _, l = stats[hh]
        o_ref[:, head_cols(hh)] = (acc_ref[head_cols(hh), :] * (1.0 / l)).T.astype(BF16)


def _moba_kernel(q_ref, k_ref, v_ref, kmean_ref, o_ref, vt_ref, acc_ref, sel_ref):
    _attn_kernel(False, q_ref, k_ref, v_ref, kmean_ref, o_ref, vt_ref, acc_ref, sel_ref)


def _fox_kernel(q_ref, k_ref, v_ref, fa_ref, o_ref, vt_ref, acc_ref):
    _attn_kernel(True, q_ref, k_ref, v_ref, fa_ref, o_ref, vt_ref, acc_ref)


def _attention(proj, kmean, faug, batch, seq):
    n = batch * seq
    nq = seq // ATT_T
    nblk = seq // MOBA_BLOCK
    hw = ATT_HG * HEAD_DIM
    grid = (batch, N_HEADS // ATT_HG, nq)
    params = pltpu.CompilerParams(
        dimension_semantics=("arbitrary", "arbitrary", "arbitrary"),
        vmem_limit_bytes=VMEM_LIMIT)

    def q_spec(col0):
        return pl.BlockSpec((ATT_T, hw), lambda b, h, i: (b * nq + i, col0 + h))

    def kv_spec(col0):
        return pl.BlockSpec((seq, hw), lambda b, h, i: (b, col0 + h))

    region = WIDTH // hw
    out_spec = pl.BlockSpec((ATT_T, hw), lambda b, h, i: (b * nq + i, h))
    out_shape = jax.ShapeDtypeStruct((n, WIDTH), BF16)
    common = [pltpu.VMEM((hw, seq), BF16), pltpu.VMEM((hw, ATT_T), F32)]

    o_moba = pl.pallas_call(
        _moba_kernel,
        grid=grid,
        in_specs=[q_spec(0), kv_spec(region), kv_spec(2 * region),
                  pl.BlockSpec((nblk, hw), lambda b, h, i: (b, h))],
        out_specs=out_spec,
        out_shape=out_shape,
        scratch_shapes=common + [pltpu.VMEM((ATT_HG, nblk, ATT_T), F32)],
        compiler_params=params,
        name="moba_attn",
    )(proj, proj, proj, kmean)

    o_fox = pl.pallas_call(
        _fox_kernel,
        grid=grid,
        in_specs=[q_spec(3 * region), kv_spec(4 * region), kv_spec(5 * region),
                  pl.BlockSpec((seq, LANES), lambda b, h, i: (b, 0))],
        out_specs=out_spec,
        out_shape=out_shape,
        scratch_shapes=common,
        compiler_params=params,
        name="fox_attn",
    )(proj, proj, proj, faug)
    return o_moba, o_fox


MIX_TM = 256


def _sigmoid(z):
    return 1.0 / (1.0 + jnp.exp(-z))


def _mix_kernel(om_ref, of_ref, ga_ref, gb_ref, x_ref, wm_ref, wf_ref, wo_ref, g_ref, out_ref):
    y_m = _dot(om_ref[...], wm_ref[...])
    y_f = _dot(of_ref[...], wf_ref[...])
    merged = _sigmoid(ga_ref[...].astype(F32)) * y_m + _sigmoid(gb_ref[...].astype(F32)) * y_f
    mixed = _dot(merged.astype(BF16), wo_ref[...])
    out_ref[...] = x_ref[...] + mixed * _rms_scale(mixed) * g_ref[...]


def _mix_out(o_moba, o_fox, proj, x2, wm, wf, wo, g):
    n = x2.shape[0]
    ga_blk = QKV_COLS // D_MODEL
    const = dict(pipeline_mode=pl.Buffered(1))
    return pl.pallas_call(
        _mix_kernel,
        grid=(n // MIX_TM,),
        in_specs=[
            pl.BlockSpec((MIX_TM, WIDTH), lambda i: (i, 0)),
            pl.BlockSpec((MIX_TM, WIDTH), lambda i: (i, 0)),
            pl.BlockSpec((MIX_TM, D_MODEL), lambda i: (i, ga_blk)),
            pl.BlockSpec((MIX_TM, D_MODEL), lambda i: (i, ga_blk + 1)),
            pl.BlockSpec((MIX_TM, D_MODEL), lambda i: (i, 0)),
            pl.BlockSpec((WIDTH, D_MODEL), lambda i: (0, 0), **const),
            pl.BlockSpec((WIDTH, D_MODEL), lambda i: (0, 0), **const),
            pl.BlockSpec((D_MODEL, D_MODEL), lambda i: (0, 0), **const),
            pl.BlockSpec((1, D_MODEL), lambda i: (0, 0)),
        ],
        out_specs=pl.BlockSpec((MIX_TM, D_MODEL), lambda i: (i, 0)),
        out_shape=jax.ShapeDtypeStruct((n, D_MODEL), F32),
        compiler_params=pltpu.CompilerParams(
            dimension_semantics=("arbitrary",), vmem_limit_bytes=VMEM_LIMIT),
        name="mix_out",
    )(o_moba, o_fox, proj, proj, x2, wm, wf, wo, g)


MLP_TM = 512
MLP_TF = 1024


def _mlp_kernel(x_ref, gpre_ref, wu_ref, wd_ref, gpost_ref, out_ref, h_ref, acc_ref):
    f = pl.program_id(1)

    @pl.when(f == 0)
    def _():
        x = x_ref[...]
        h_ref[...] = (x * _rms_scale(x) * gpre_ref[...]).astype(BF16)
        acc_ref[...] = jnp.zeros_like(acc_ref)

    u = _dot(h_ref[...], wu_ref[...])
    a = jnp.square(jnp.maximum(u, 0.0)).astype(BF16)
    acc_ref[...] += _dot(a, wd_ref[...])

    @pl.when(f == pl.num_programs(1) - 1)
    def _():
        mo = acc_ref[...]
        out_ref[...] = x_ref[...] + mo * _rms_scale(mo) * gpost_ref[...]


def _mlp(x1, gpre, wu, wd, gpost):
    n = x1.shape[0]
    return pl.pallas_call(
        _mlp_kernel,
        grid=(n // MLP_TM, D_FF // MLP_TF),
        in_specs=[
            pl.BlockSpec((MLP_TM, D_MODEL), lambda i, f: (i, 0)),
            pl.BlockSpec((1, D_MODEL), lambda i, f: (0, 0)),
            pl.BlockSpec((D_MODEL, MLP_TF), lambda i, f: (0, f)),
            pl.BlockSpec((MLP_TF, D_MODEL), lambda i, f: (f, 0)),
            pl.BlockSpec((1, D_MODEL), lambda i, f: (0, 0)),
        ],
        out_specs=pl.BlockSpec((MLP_TM, D_MODEL), lambda i, f: (i, 0)),
        out_shape=jax.ShapeDtypeStruct((n, D_MODEL), F32),
        scratch_shapes=[pltpu.VMEM((MLP_TM, D_MODEL), BF16), pltpu.VMEM((MLP_TM, D_MODEL), F32)],
        compiler_params=pltpu.CompilerParams(
            dimension_semantics=("arbitrary", "arbitrary"), vmem_limit_bytes=VMEM_LIMIT),
        name="mlp",
    )(x1, gpre, wu, wd, gpost)


def _rope_tables(seq):
    inv_freq = ROPE_THETA ** (-jnp.arange(0, ROPE_DIM, 2, dtype=F32) / ROPE_DIM)
    ang = jnp.arange(seq, dtype=F32)[:, None] * inv_freq[None, :]
    cos, sin = jnp.cos(ang), jnp.sin(ang)
    rest = HEAD_DIM - ROPE_DIM
    cos_t = jnp.concatenate([cos, cos, jnp.ones((seq, rest), F32)], axis=1)
    zeros_h = jnp.zeros((seq, ROPE_HALF), F32)
    zeros_r = jnp.zeros((seq, rest), F32)
    sina_t = jnp.concatenate([-sin, zeros_h, zeros_r], axis=1)
    sinb_t = jnp.concatenate([zeros_h, sin, zeros_r], axis=1)
    return cos_t, sina_t, sinb_t


def _layer(x2, w_in, b_forget, w_bm, w_bf, w_out, g_mix_pre, g_mix_post,
           w_up, w_down, g_mlp_pre, g_mlp_post, batch, seq):
    ff0 = QKV_COLS
    ff1 = QKV_COLS + N_HEADS
    wq = w_in[:, :ff0].astype(BF16)
    wg = w_in[:, ff1:].astype(BF16)
    wff = jnp.pad(w_in[:, ff0:ff1], ((0, 0), (0, LANES - N_HEADS))).astype(BF16)
    b_row = jnp.pad(b_forget, (0, LANES - N_HEADS)).reshape(1, LANES)
    cos_t, sina_t, sinb_t = _rope_tables(seq)

    proj, ff, kmean = _in_proj(x2, g_mix_pre.reshape(1, -1), wq, wg, wff, cos_t, sina_t, sinb_t, seq)
    faug = _fox_bias(ff, b_row, batch, seq)
    kmean = kmean.reshape(batch * (seq // MOBA_BLOCK), WIDTH)
    o_moba, o_fox = _attention(proj, kmean, faug, batch, seq)
    x1 = _mix_out(o_moba, o_fox, proj, x2, w_bm.astype(BF16), w_bf.astype(BF16),
                  w_out.astype(BF16), g_mix_post.reshape(1, -1))
    return _mlp(x1, g_mlp_pre.reshape(1, -1), w_up.astype(BF16), w_down.astype(BF16),
                g_mlp_post.reshape(1, -1))


def kernel(x, w_in, b_forget, w_branch_moba, w_branch_fox, w_out, g_mix_pre, g_mix_post,
           w_up, w_down, g_mlp_pre, g_mlp_post):
    batch, seq, d = x.shape
    assert d == D_MODEL and seq % IN_TM == 0 and seq % ATT_T == 0
    assert w_in.shape[-1] == PROJ_COLS + N_HEADS
    x2 = x.reshape(batch * seq, d)
    for l in range(w_in.shape[0]):
        x2 = _layer(x2, w_in[l], b_forget[l], w_branch_moba[l], w_branch_fox[l], w_out[l],
                    g_mix_pre[l], g_mix_post[l], w_up[l], w_down[l], g_mlp_pre[l],
                    g_mlp_post[l], batch, seq)
    return x2.reshape(batch, seq, d)
```

```python
import math

import jax
import jax.numpy as jnp
from jax import lax
from jax.experimental import pallas as pl
from jax.experimental.pallas import tpu as pltpu

D_MODEL = 2048
HEAD_DIM = 128
N_HEADS = 8
WIDTH = N_HEADS * HEAD_DIM
MOBA_BLOCK = 256
MOBA_TOP_K = 3
ROPE_THETA = 500000.0
ROPE_DIM = HEAD_DIM // 4
ROPE_HALF = ROPE_DIM // 2
D_FF = 4 * D_MODEL
RMS_EPS = 1e-6
NEG_BIG = -1e30
LANES = 128

QKV_COLS = 6 * WIDTH
GATE_COLS = 2 * D_MODEL
PROJ_COLS = QKV_COLS + GATE_COLS

VMEM_LIMIT = 56 * 1024 * 1024

BF16 = jnp.bfloat16
F32 = jnp.float32
LOG2E = math.log2(math.e)
Q_SCALE = HEAD_DIM ** -0.5 * LOG2E


def _rms_scale(x):
    return lax.rsqrt(jnp.mean(x * x, axis=-1, keepdims=True) + RMS_EPS)


def _split3(x):
    hi = x.astype(BF16)
    r1 = x - hi.astype(F32)
    mid = r1.astype(BF16)
    lo = (r1 - mid.astype(F32)).astype(BF16)
    return hi, mid, lo


def _dot(a, b):
    return jnp.dot(a, b, preferred_element_type=F32)


def _dot_nt(a, b):
    return lax.dot_general(a, b, (((1,), (1,)), ((), ())), preferred_element_type=F32)


IN_TM = 1024
IN_TN = 1024
IN_QKV_TILES = QKV_COLS // IN_TN


def _in_proj_kernel(x_ref, g_ref, wq_ref, wg_ref, wff_ref, cos_ref, sina_ref, sinb_ref,
                    proj_ref, ff_ref, kmean_ref, h_ref):
    j = pl.program_id(1)

    @pl.when(j == 0)
    def _():
        x = x_ref[...]
        h = (x * _rms_scale(x) * g_ref[...]).astype(BF16)
        h_ref[...] = h
        ff_ref[...] = _dot(h, wff_ref[...])

    def rotary(t):
        outs = []
        for hd in range(IN_TN // HEAD_DIM):
            th = t[:, hd * HEAD_DIM:(hd + 1) * HEAD_DIM]
            up = pltpu.roll(th, HEAD_DIM - ROPE_HALF, 1)
            dn = pltpu.roll(th, ROPE_HALF, 1)
            outs.append(th * cos_ref[...] + up * sina_ref[...] + dn * sinb_ref[...])
        return jnp.concatenate(outs, axis=1)

    @pl.when(j == 0)
    def _():
        acc = _dot(h_ref[...], wq_ref[...])
        proj_ref[...] = (rotary(acc) * Q_SCALE).astype(BF16)

    @pl.when(j == 1)
    def _():
        kr = rotary(_dot(h_ref[...], wq_ref[...]))
        proj_ref[...] = kr.astype(BF16)
        nblk = IN_TM // MOBA_BLOCK
        means = [jnp.mean(kr[b * MOBA_BLOCK:(b + 1) * MOBA_BLOCK, :], axis=0, keepdims=True)
                 for b in range(nblk)]
        kmean_ref[0] = jnp.concatenate(means, axis=0)

    @pl.when(j == 3)
    def _():
        proj_ref[...] = (_dot(h_ref[...], wq_ref[...]) * Q_SCALE).astype(BF16)

    @pl.when((j == 2) | (j == 4) | (j == 5))
    def _():
        proj_ref[...] = _dot(h_ref[...], wq_ref[...]).astype(BF16)

    @pl.when(j >= IN_QKV_TILES)
    def _():
        proj_ref[...] = _dot(h_ref[...], wg_ref[...]).astype(BF16)


def _in_proj(x2, g, wq, wg, wff, cos_t, sina_t, sinb_t, seq):
    n = x2.shape[0]
    seq_tiles = seq // IN_TM
    grid = (n // IN_TM, PROJ_COLS // IN_TN)
    last_q = IN_QKV_TILES - 1
    return pl.pallas_call(
        _in_proj_kernel,
        grid=grid,
        in_specs=[
            pl.BlockSpec((IN_TM, D_MODEL), lambda i, j: (i, 0)),
            pl.BlockSpec((1, D_MODEL), lambda i, j: (0, 0)),
            pl.BlockSpec((D_MODEL, IN_TN), lambda i, j: (0, jnp.minimum(j, last_q))),
            pl.BlockSpec((D_MODEL, IN_TN), lambda i, j: (0, jnp.maximum(j - IN_QKV_TILES, 0))),
            pl.BlockSpec((D_MODEL, LANES), lambda i, j: (0, 0)),
            pl.BlockSpec((IN_TM, HEAD_DIM), lambda i, j: (i % seq_tiles, 0)),
            pl.BlockSpec((IN_TM, HEAD_DIM), lambda i, j: (i % seq_tiles, 0)),
            pl.BlockSpec((IN_TM, HEAD_DIM), lambda i, j: (i % seq_tiles, 0)),
        ],
        out_specs=[
            pl.BlockSpec((IN_TM, IN_TN), lambda i, j: (i, j)),
            pl.BlockSpec((IN_TM, LANES), lambda i, j: (i, 0)),
            pl.BlockSpec((1, IN_TM // MOBA_BLOCK, WIDTH), lambda i, j: (i, 0, 0)),
        ],
        out_shape=[
            jax.ShapeDtypeStruct((n, PROJ_COLS), BF16),
            jax.ShapeDtypeStruct((n, LANES), F32),
            jax.ShapeDtypeStruct((n // IN_TM, IN_TM // MOBA_BLOCK, WIDTH), F32),
        ],
        scratch_shapes=[pltpu.VMEM((IN_TM, D_MODEL), BF16)],
        compiler_params=pltpu.CompilerParams(
            dimension_semantics=("arbitrary", "arbitrary"),
            vmem_limit_bytes=VMEM_LIMIT),
        name="in_proj",
    )(x2, g, wq, wg, wff, cos_t, sina_t, sinb_t)


FB_CHUNK = 256


def _fox_bias_kernel(ff_ref, b_ref, out_ref):
    seq = ff_ref.shape[0]
    r = lax.broadcasted_iota(jnp.int32, (FB_CHUNK, FB_CHUNK), 0)
    c = lax.broadcasted_iota(jnp.int32, (FB_CHUNK, FB_CHUNK), 1)
    ltri = (c <= r).astype(BF16)
    pr = lax.broadcasted_iota(jnp.int32, (LANES, LANES), 0)
    pc = lax.broadcasted_iota(jnp.int32, (LANES, LANES), 1)
    place = [((pc == 3 * pr + t) & (pr < N_HEADS)).astype(BF16) for t in range(3)]

    carry = jnp.zeros((1, LANES), F32)
    for ci in range(seq // FB_CHUNK):
        z = ff_ref[ci * FB_CHUNK:(ci + 1) * FB_CHUNK, :] + b_ref[...]
        logf = -(jnp.maximum(-z, 0.0) + jnp.log1p(jnp.exp(-jnp.abs(z))))
        hi, mid, lo = _split3(logf)
        f = _dot(ltri, hi) + _dot(ltri, mid) + _dot(ltri, lo) + carry
        carry = f[FB_CHUNK - 1:FB_CHUNK, :]
        nh, nm, nl = _split3(f * (-LOG2E))
        aug = _dot(nh, place[0]) + _dot(nm, place[1]) + _dot(nl, place[2])
        out_ref[ci * FB_CHUNK:(ci + 1) * FB_CHUNK, :] = aug.astype(BF16)


def _fox_bias(ff, b_row, batch, seq):
    return pl.pallas_call(
        _fox_bias_kernel,
        grid=(batch,),
        in_specs=[
            pl.BlockSpec((seq, LANES), lambda b: (b, 0)),
            pl.BlockSpec((1, LANES), lambda b: (0, 0)),
        ],
        out_specs=pl.BlockSpec((seq, LANES), lambda b: (b, 0)),
        out_shape=jax.ShapeDtypeStruct((batch * seq, LANES), BF16),
        compiler_params=pltpu.CompilerParams(
            dimension_semantics=("arbitrary",), vmem_limit_bytes=VMEM_LIMIT),
        name="fox_bias",
    )(ff, b_row)


ATT_T = 512
ATT_HG = 2
ATT_SUB = ATT_T // MOBA_BLOCK


def _attn_kernel(is_fox, q_ref, k_ref, v_ref, x_ref, o_ref, vt_ref, acc_ref, sa_ref, sb_ref,
                 *maybe_sel):
    hg = pl.program_id(1)
    qt = pl.program_id(2)
    seq = k_ref.shape[0]
    nblk = seq // MOBA_BLOCK
    t = ATT_T

    def head_cols(hh):
        return slice(hh * HEAD_DIM, (hh + 1) * HEAD_DIM)

    @pl.when(qt == 0)
    def _():
        for hh in range(ATT_HG):
            for ci in range(nblk):
                rows = slice(ci * MOBA_BLOCK, (ci + 1) * MOBA_BLOCK)
                blk = v_ref[rows, head_cols(hh)].astype(F32)
                vt_ref[head_cols(hh), rows] = blk.T.astype(BF16)

    qs = []
    for hh in range(ATT_HG):
        qh = q_ref[:, head_cols(hh)]
        if is_fox:
            head = hg * ATT_HG + hh
            lane = lax.broadcasted_iota(jnp.int32, (t, LANES), 1)
            pick = ((lane >= 3 * head) & (lane < 3 * head + 3)).astype(BF16)
            qh = jnp.concatenate([qh, pick], axis=1)
        qs.append(qh)

    if not is_fox:
        sel_ref, = maybe_sel
        blk_id = lax.broadcasted_iota(jnp.int32, (nblk, t), 0)
        q_blk = qt * ATT_SUB + lax.broadcasted_iota(jnp.int32, (nblk, t), 1) // MOBA_BLOCK
        past = blk_id < q_blk
        for hh in range(ATT_HG):
            kh, km, kl = _split3(x_ref[:, head_cols(hh)])
            gate = _dot_nt(kh, qs[hh]) + _dot_nt(km, qs[hh]) + _dot_nt(kl, qs[hh])
            gate = jnp.where(past, gate, -jnp.inf)
            rank = jnp.zeros((nblk, t), jnp.int32)
            for mth in range(nblk):
                gm = gate[mth:mth + 1, :]
                ahead = (gm > gate) | ((gm == gate) & (mth < blk_id))
                rank = rank + ahead.astype(jnp.int32)
            sel_ref[hh] = ((rank < MOBA_TOP_K) & past).astype(F32)

    def scores(hh, start):
        kb = k_ref[pl.ds(start, t), head_cols(hh)]
        if is_fox:
            kb = jnp.concatenate([kb, x_ref[pl.ds(start, t), :]], axis=1)
        return _dot_nt(kb, qs[hh])

    def mask_rows(hh, s, first_blk, diagonal):
        parts = []
        for r in range(ATT_SUB):
            keep = sel_ref[hh, pl.ds(first_blk + r, 1), :] > 0.5
            if diagonal:
                key = r * MOBA_BLOCK + lax.broadcasted_iota(jnp.int32, (MOBA_BLOCK, t), 0)
                qry = lax.broadcasted_iota(jnp.int32, (MOBA_BLOCK, t), 1)
                keep = keep | ((key <= qry) & (qry < (r + 1) * MOBA_BLOCK))
            parts.append(jnp.where(keep, s[r * MOBA_BLOCK:(r + 1) * MOBA_BLOCK, :], NEG_BIG))
        return jnp.concatenate(parts, axis=0)

    def vt_chunk(hh, start):
        return vt_ref[head_cols(hh), pl.ds(start, t)]

    def produce(dst_ref, start, first_blk, diagonal):
        cmax = []
        for hh in range(ATT_HG):
            s = scores(hh, start)
            if is_fox:
                if diagonal:
                    key = lax.broadcasted_iota(jnp.int32, (t, t), 0)
                    qry = lax.broadcasted_iota(jnp.int32, (t, t), 1)
                    s = jnp.where(key <= qry, s, NEG_BIG)
            else:
                s = mask_rows(hh, s, first_blk, diagonal)
            dst_ref[hh] = s
            cmax.append(jnp.max(s, axis=0, keepdims=True))
        return tuple(cmax)

    def consume(src_ref, start, cmax, stats):
        new = []
        for hh in range(ATT_HG):
            m, l = stats[hh]
            m_new = jnp.maximum(m, cmax[hh])
            alpha = jnp.exp2(m - m_new)
            p = jnp.exp2(src_ref[hh] - m_new)
            l = alpha * l + jnp.sum(p, axis=0, keepdims=True)
            acc_ref[head_cols(hh), :] = (alpha * acc_ref[head_cols(hh), :]
                                         + _dot(vt_chunk(hh, start), p.astype(BF16)))
            new.append((m_new, l))
        return tuple(new)

    own = pl.multiple_of(qt * t, t)
    acc_ref[...] = jnp.zeros_like(acc_ref)
    stats = tuple((jnp.full((1, t), NEG_BIG, F32), jnp.zeros((1, t), F32)) for _ in range(ATT_HG))
    cmax = produce(sa_ref, own, qt * ATT_SUB, True)

    def step(dst_ref, src_ref):
        def run(i, cmax, stats):
            nxt = pl.multiple_of(i * t, t)
            cur = pl.multiple_of(jnp.where(i == 0, qt, i - 1) * t, t)
            cmax_next = produce(dst_ref, nxt, i * ATT_SUB, False)
            return cmax_next, consume(src_ref, cur, cmax, stats)
        return run

    def body(i, carry):
        return lax.cond(i % 2 == 0, step(sb_ref, sa_ref), step(sa_ref, sb_ref), i, *carry)

    cmax, stats = lax.fori_loop(0, qt, body, (cmax, stats))
    last = pl.multiple_of(jnp.maximum(qt - 1, 0) * t, t)
    stats = lax.cond(qt % 2 == 0,
                     lambda c, s: consume(sa_ref, last, c, s),
                     lambda c, s: consume(sb_ref, last, c, s), cmax, stats)
    for hh in range(ATT_HG):
        _, l = stats[hh]
        o_ref[:, head_cols(hh)] = (acc_ref[head_cols(hh), :] * (1.0 / l)).T.astype(BF16)


def _moba_kernel(*refs):
    _attn_kernel(False, *refs)


def _fox_kernel(*refs):
    _attn_kernel(True, *refs)


def _attention(proj, kmean, faug, batch, seq):
    n = batch * seq
    nq = seq // ATT_T
    nblk = seq // MOBA_BLOCK
    hw = ATT_HG * HEAD_DIM
    grid = (batch, N_HEADS // ATT_HG, nq)
    params = pltpu.CompilerParams(
        dimension_semantics=("arbitrary", "arbitrary", "arbitrary"),
        vmem_limit_bytes=VMEM_LIMIT)

    def q_spec(col0):
        return pl.BlockSpec((ATT_T, hw), lambda b, h, i: (b * nq + i, col0 + h))

    def kv_spec(col0):
        return pl.BlockSpec((seq, hw), lambda b, h, i: (b, col0 + h))

    region = WIDTH // hw
    out_spec = pl.BlockSpec((ATT_T, hw), lambda b, h, i: (b * nq + i, h))
    out_shape = jax.ShapeDtypeStruct((n, WIDTH), BF16)
    common = [pltpu.VMEM((hw, seq), BF16), pltpu.VMEM((hw, ATT_T), F32),
              pltpu.VMEM((ATT_HG, ATT_T, ATT_T), F32), pltpu.VMEM((ATT_HG, ATT_T, ATT_T), F32)]

    o_moba = pl.pallas_call(
        _moba_kernel,
        grid=grid,
        in_specs=[q_spec(0), kv_spec(region), kv_spec(2 * region),
                  pl.BlockSpec((nblk, hw), lambda b, h, i: (b, h))],
        out_specs=out_spec,
        out_shape=out_shape,
        scratch_shapes=common + [pltpu.VMEM((ATT_HG, nblk, ATT_T), F32)],
        compiler_params=params,
        name="moba_attn",
    )(proj, proj, proj, kmean)

    o_fox = pl.pallas_call(
        _fox_kernel,
        grid=grid,
        in_specs=[q_spec(3 * region), kv_spec(4 * region), kv_spec(5 * region),
                  pl.BlockSpec((seq, LANES), lambda b, h, i: (b, 0))],
        out_specs=out_spec,
        out_shape=out_shape,
        scratch_shapes=common,
        compiler_params=params,
        name="fox_attn",
    )(proj, proj, proj, faug)
    return o_moba, o_fox


MIX_TM = 256


def _sigmoid(z):
    return 1.0 / (1.0 + jnp.exp(-z))


def _mix_kernel(om_ref, of_ref, ga_ref, gb_ref, x_ref, wm_ref, wf_ref, wo_ref, g_ref, out_ref):
    y_m = _dot(om_ref[...], wm_ref[...])
    y_f = _dot(of_ref[...], wf_ref[...])
    merged = _sigmoid(ga_ref[...].astype(F32)) * y_m + _sigmoid(gb_ref[...].astype(F32)) * y_f
    mixed = _dot(merged.astype(BF16), wo_ref[...])
    out_ref[...] = x_ref[...] + mixed * _rms_scale(mixed) * g_ref[...]


def _mix_out(o_moba, o_fox, proj, x2, wm, wf, wo, g):
    n = x2.shape[0]
    ga_blk = QKV_COLS // D_MODEL
    const = dict(pipeline_mode=pl.Buffered(1))
    return pl.pallas_call(
        _mix_kernel,
        grid=(n // MIX_TM,),
        in_specs=[
            pl.BlockSpec((MIX_TM, WIDTH), lambda i: (i, 0)),
            pl.BlockSpec((MIX_TM, WIDTH), lambda i: (i, 0)),
            pl.BlockSpec((MIX_TM, D_MODEL), lambda i: (i, ga_blk)),
            pl.BlockSpec((MIX_TM, D_MODEL), lambda i: (i, ga_blk + 1)),
            pl.BlockSpec((MIX_TM, D_MODEL), lambda i: (i, 0)),
            pl.BlockSpec((WIDTH, D_MODEL), lambda i: (0, 0), **const),
            pl.BlockSpec((WIDTH, D_MODEL), lambda i: (0, 0), **const),
            pl.BlockSpec((D_MODEL, D_MODEL), lambda i: (0, 0), **const),
            pl.BlockSpec((1, D_MODEL), lambda i: (0, 0)),
        ],
        out_specs=pl.BlockSpec((MIX_TM, D_MODEL), lambda i: (i, 0)),
        out_shape=jax.ShapeDtypeStruct((n, D_MODEL), F32),
        compiler_params=pltpu.CompilerParams(
            dimension_semantics=("arbitrary",), vmem_limit_bytes=VMEM_LIMIT),
        name="mix_out",
    )(o_moba, o_fox, proj, proj, x2, wm, wf, wo, g)


MLP_TM = 512
MLP_TF = 1024


def _mlp_kernel(x_ref, gpre_ref, wu_ref, wd_ref, gpost_ref, out_ref, h_ref, acc_ref):
    f = pl.program_id(1)

    @pl.when(f == 0)
    def _():
        x = x_ref[...]
        h_ref[...] = (x * _rms_scale(x) * gpre_ref[...]).astype(BF16)
        acc_ref[...] = jnp.zeros_like(acc_ref)

    u = _dot(h_ref[...], wu_ref[...])
    a = jnp.square(jnp.maximum(u, 0.0)).astype(BF16)
    acc_ref[...] += _dot(a, wd_ref[...])

    @pl.when(f == pl.num_programs(1) - 1)
    def _():
        mo = acc_ref[...]
        out_ref[...] = x_ref[...] + mo * _rms_scale(mo) * gpost_ref[...]


def _mlp(x1, gpre, wu, wd, gpost):
    n = x1.shape[0]
    return pl.pallas_call(
        _mlp_kernel,
        grid=(n // MLP_TM, D_FF // MLP_TF),
        in_specs=[
            pl.BlockSpec((MLP_TM, D_MODEL), lambda i, f: (i, 0)),
            pl.BlockSpec((1, D_MODEL), lambda i, f: (0, 0)),
            pl.BlockSpec((D_MODEL, MLP_TF), lambda i, f: (0, f)),
            pl.BlockSpec((MLP_TF, D_MODEL), lambda i, f: (f, 0)),
            pl.BlockSpec((1, D_MODEL), lambda i, f: (0, 0)),
        ],
        out_specs=pl.BlockSpec((MLP_TM, D_MODEL), lambda i, f: (i, 0)),
        out_shape=jax.ShapeDtypeStruct((n, D_MODEL), F32),
        scratch_shapes=[pltpu.VMEM((MLP_TM, D_MODEL), BF16), pltpu.VMEM((MLP_TM, D_MODEL), F32)],
        compiler_params=pltpu.CompilerParams(
            dimension_semantics=("arbitrary", "arbitrary"), vmem_limit_bytes=VMEM_LIMIT),
        name="mlp",
    )(x1, gpre, wu, wd, gpost)


def _rope_tables(seq):
    inv_freq = ROPE_THETA ** (-jnp.arange(0, ROPE_DIM, 2, dtype=F32) / ROPE_DIM)
    ang = jnp.arange(seq, dtype=F32)[:, None] * inv_freq[None, :]
    cos, sin = jnp.cos(ang), jnp.sin(ang)
    rest = HEAD_DIM - ROPE_DIM
    cos_t = jnp.concatenate([cos, cos, jnp.ones((seq, rest), F32)], axis=1)
    zeros_h = jnp.zeros((seq, ROPE_HALF), F32)
    zeros_r = jnp.zeros((seq, rest), F32)
    sina_t = jnp.concatenate([-sin, zeros_h, zeros_r], axis=1)
    sinb_t = jnp.concatenate([zeros_h, sin, zeros_r], axis=1)
    return cos_t, sina_t, sinb_t


def _layer(x2, w_in, b_forget, w_bm, w_bf, w_out, g_mix_pre, g_mix_post,
           w_up, w_down, g_mlp_pre, g_mlp_post, batch, seq):
    ff0 = QKV_COLS
    ff1 = QKV_COLS + N_HEADS
    wq = w_in[:, :ff0].astype(BF16)
    wg = w_in[:, ff1:].astype(BF16)
    wff = jnp.pad(w_in[:, ff0:ff1], ((0, 0), (0, LANES - N_HEADS))).astype(BF16)
    b_row = jnp.pad(b_forget, (0, LANES - N_HEADS)).reshape(1, LANES)
    cos_t, sina_t, sinb_t = _rope_tables(seq)

    proj, ff, kmean = _in_proj(x2, g_mix_pre.reshape(1, -1), wq, wg, wff, cos_t, sina_t, sinb_t, seq)
    faug = _fox_bias(ff, b_row, batch, seq)
    kmean = kmean.reshape(batch * (seq // MOBA_BLOCK), WIDTH)
    o_moba, o_fox = _attention(proj, kmean, faug, batch, seq)
    x1 = _mix_out(o_moba, o_fox, proj, x2, w_bm.astype(BF16), w_bf.astype(BF16),
                  w_out.astype(BF16), g_mix_post.reshape(1, -1))
    return _mlp(x1, g_mlp_pre.reshape(1, -1), w_up.astype(BF16), w_down.astype(BF16),
                g_mlp_post.reshape(1, -1))


def kernel(x, w_in, b_forget, w_branch_moba, w_branch_fox, w_out, g_mix_pre, g_mix_post,
           w_up, w_down, g_mlp_pre, g_mlp_post):
    batch, seq, d = x.shape
    assert d == D_MODEL and seq % IN_TM == 0 and seq % ATT_T == 0
    assert w_in.shape[-1] == PROJ_COLS + N_HEADS
    x2 = x.reshape(batch * seq, d)
    for l in range(w_in.shape[0]):
        x2 = _layer(x2, w_in[l], b_forget[l], w_branch_moba[l], w_branch_fox[l], w_out[l],
                    g_mix_pre[l], g_mix_post[l], w_up[l], w_down[l], g_mlp_pre[l],
                    g_mlp_post[l], batch, seq)
    return x2.reshape(batch, seq, d)
```

```python
import math

import jax
import jax.numpy as jnp
from jax import lax
from jax.experimental import pallas as pl
from jax.experimental.pallas import tpu as pltpu

D_MODEL = 2048
HEAD_DIM = 128
N_HEADS = 8
WIDTH = N_HEADS * HEAD_DIM
MOBA_BLOCK = 256
MOBA_TOP_K = 3
ROPE_THETA = 500000.0
ROPE_DIM = HEAD_DIM // 4
ROPE_HALF = ROPE_DIM // 2
D_FF = 4 * D_MODEL
RMS_EPS = 1e-6
NEG_BIG = -1e30
LANES = 128

QKV_COLS = 6 * WIDTH
GATE_COLS = 2 * D_MODEL
PROJ_COLS = QKV_COLS + GATE_COLS

VMEM_LIMIT = 56 * 1024 * 1024

BF16 = jnp.bfloat16
F32 = jnp.float32
LOG2E = math.log2(math.e)
Q_SCALE = HEAD_DIM ** -0.5 * LOG2E


def _rms_scale(x):
    return lax.rsqrt(jnp.mean(x * x, axis=-1, keepdims=True) + RMS_EPS)


def _split3(x):
    hi = x.astype(BF16)
    r1 = x - hi.astype(F32)
    mid = r1.astype(BF16)
    lo = (r1 - mid.astype(F32)).astype(BF16)
    return hi, mid, lo


def _dot(a, b):
    return jnp.dot(a, b, preferred_element_type=F32)


def _dot_nt(a, b):
    return lax.dot_general(a, b, (((1,), (1,)), ((), ())), preferred_element_type=F32)


IN_TM = 1024
IN_TN = 1024
IN_QKV_TILES = QKV_COLS // IN_TN


def _in_proj_kernel(x_ref, g_ref, wq_ref, wg_ref, wff_ref, cos_ref, sina_ref, sinb_ref,
                    proj_ref, ff_ref, kmean_ref, h_ref):
    j = pl.program_id(1)

    @pl.when(j == 0)
    def _():
        x = x_ref[...]
        h = (x * _rms_scale(x) * g_ref[...]).astype(BF16)
        h_ref[...] = h
        ff_ref[...] = _dot(h, wff_ref[...])

    def rotary(t):
        outs = []
        for hd in range(IN_TN // HEAD_DIM):
            th = t[:, hd * HEAD_DIM:(hd + 1) * HEAD_DIM]
            up = pltpu.roll(th, HEAD_DIM - ROPE_HALF, 1)
            dn = pltpu.roll(th, ROPE_HALF, 1)
            outs.append(th * cos_ref[...] + up * sina_ref[...] + dn * sinb_ref[...])
        return jnp.concatenate(outs, axis=1)

    @pl.when(j == 0)
    def _():
        acc = _dot(h_ref[...], wq_ref[...])
        proj_ref[...] = (rotary(acc) * Q_SCALE).astype(BF16)

    @pl.when(j == 1)
    def _():
        kr = rotary(_dot(h_ref[...], wq_ref[...]))
        proj_ref[...] = kr.astype(BF16)
        nblk = IN_TM // MOBA_BLOCK
        means = [jnp.mean(kr[b * MOBA_BLOCK:(b + 1) * MOBA_BLOCK, :], axis=0, keepdims=True)
                 for b in range(nblk)]
        kmean_ref[0] = jnp.concatenate(means, axis=0)

    @pl.when(j == 3)
    def _():
        proj_ref[...] = (_dot(h_ref[...], wq_ref[...]) * Q_SCALE).astype(BF16)

    @pl.when((j == 2) | (j == 4) | (j == 5))
    def _():
        proj_ref[...] = _dot(h_ref[...], wq_ref[...]).astype(BF16)

    @pl.when(j >= IN_QKV_TILES)
    def _():
        proj_ref[...] = _dot(h_ref[...], wg_ref[...]).astype(BF16)


def _in_proj(x2, g, wq, wg, wff, cos_t, sina_t, sinb_t, seq):
    n = x2.shape[0]
    seq_tiles = seq // IN_TM
    grid = (n // IN_TM, PROJ_COLS // IN_TN)
    last_q = IN_QKV_TILES - 1
    return pl.pallas_call(
        _in_proj_kernel,
        grid=grid,
        in_specs=[
            pl.BlockSpec((IN_TM, D_MODEL), lambda i, j: (i, 0)),
            pl.BlockSpec((1, D_MODEL), lambda i, j: (0, 0)),
            pl.BlockSpec((D_MODEL, IN_TN), lambda i, j: (0, jnp.minimum(j, last_q))),
            pl.BlockSpec((D_MODEL, IN_TN), lambda i, j: (0, jnp.maximum(j - IN_QKV_TILES, 0))),
            pl.BlockSpec((D_MODEL, LANES), lambda i, j: (0, 0)),
            pl.BlockSpec((IN_TM, HEAD_DIM), lambda i, j: (i % seq_tiles, 0)),
            pl.BlockSpec((IN_TM, HEAD_DIM), lambda i, j: (i % seq_tiles, 0)),
            pl.BlockSpec((IN_TM, HEAD_DIM), lambda i, j: (i % seq_tiles, 0)),
        ],
        out_specs=[
            pl.BlockSpec((IN_TM, IN_TN), lambda i, j: (i, j)),
            pl.BlockSpec((IN_TM, LANES), lambda i, j: (i, 0)),
            pl.BlockSpec((1, IN_TM // MOBA_BLOCK, WIDTH), lambda i, j: (i, 0, 0)),
        ],
        out_shape=[
            jax.ShapeDtypeStruct((n, PROJ_COLS), BF16),
            jax.ShapeDtypeStruct((n, LANES), F32),
            jax.ShapeDtypeStruct((n // IN_TM, IN_TM // MOBA_BLOCK, WIDTH), F32),
        ],
        scratch_shapes=[pltpu.VMEM((IN_TM, D_MODEL), BF16)],
        compiler_params=pltpu.CompilerParams(
            dimension_semantics=("arbitrary", "arbitrary"),
            vmem_limit_bytes=VMEM_LIMIT),
        name="in_proj",
    )(x2, g, wq, wg, wff, cos_t, sina_t, sinb_t)


FB_CHUNK = 256


def _fox_bias_kernel(ff_ref, b_ref, out_ref):
    seq = ff_ref.shape[0]
    r = lax.broadcasted_iota(jnp.int32, (FB_CHUNK, FB_CHUNK), 0)
    c = lax.broadcasted_iota(jnp.int32, (FB_CHUNK, FB_CHUNK), 1)
    ltri = (c <= r).astype(BF16)
    pr = lax.broadcasted_iota(jnp.int32, (LANES, LANES), 0)
    pc = lax.broadcasted_iota(jnp.int32, (LANES, LANES), 1)
    place = [((pc == 3 * pr + t) & (pr < N_HEADS)).astype(BF16) for t in range(3)]

    carry = jnp.zeros((1, LANES), F32)
    for ci in range(seq // FB_CHUNK):
        z = ff_ref[ci * FB_CHUNK:(ci + 1) * FB_CHUNK, :] + b_ref[...]
        logf = -(jnp.maximum(-z, 0.0) + jnp.log1p(jnp.exp(-jnp.abs(z))))
        hi, mid, lo = _split3(logf)
        f = _dot(ltri, hi) + _dot(ltri, mid) + _dot(ltri, lo) + carry
        carry = f[FB_CHUNK - 1:FB_CHUNK, :]
        nh, nm, nl = _split3(f * (-LOG2E))
        aug = _dot(nh, place[0]) + _dot(nm, place[1]) + _dot(nl, place[2])
        out_ref[ci * FB_CHUNK:(ci + 1) * FB_CHUNK, :] = aug.astype(BF16)


def _fox_bias(ff, b_row, batch, seq):
    return pl.pallas_call(
        _fox_bias_kernel,
        grid=(batch,),
        in_specs=[
            pl.BlockSpec((seq, LANES), lambda b: (b, 0)),
            pl.BlockSpec((1, LANES), lambda b: (0, 0)),
        ],
        out_specs=pl.BlockSpec((seq, LANES), lambda b: (b, 0)),
        out_shape=jax.ShapeDtypeStruct((batch * seq, LANES), BF16),
        compiler_params=pltpu.CompilerParams(
            dimension_semantics=("arbitrary",), vmem_limit_bytes=VMEM_LIMIT),
        name="fox_bias",
    )(ff, b_row)


ATT_T = 512
ATT_HG = 4
ATT_SUB = ATT_T // MOBA_BLOCK


def _attn_kernel(is_fox, q_ref, k_ref, v_ref, x_ref, o_ref, vt_ref, acc_ref, sa_ref, sb_ref,
                 *maybe_sel):
    hg = pl.program_id(1)
    qt = pl.program_id(2)
    seq = k_ref.shape[0]
    nblk = seq // MOBA_BLOCK
    t = ATT_T

    def head_cols(hh):
        return slice(hh * HEAD_DIM, (hh + 1) * HEAD_DIM)

    @pl.when(qt == 0)
    def _():
        for hh in range(ATT_HG):
            for ci in range(nblk):
                rows = slice(ci * MOBA_BLOCK, (ci + 1) * MOBA_BLOCK)
                blk = v_ref[rows, head_cols(hh)].astype(F32)
                vt_ref[head_cols(hh), rows] = blk.T.astype(BF16)

    qs = []
    for hh in range(ATT_HG):
        qh = q_ref[:, head_cols(hh)]
        if is_fox:
            head = hg * ATT_HG + hh
            lane = lax.broadcasted_iota(jnp.int32, (t, LANES), 1)
            pick = ((lane >= 3 * head) & (lane < 3 * head + 3)).astype(BF16)
            qh = jnp.concatenate([qh, pick], axis=1)
        qs.append(qh)

    if not is_fox:
        sel_ref, = maybe_sel
        blk_id = lax.broadcasted_iota(jnp.int32, (nblk, t), 0)
        q_blk = qt * ATT_SUB + lax.broadcasted_iota(jnp.int32, (nblk, t), 1) // MOBA_BLOCK
        past = blk_id < q_blk
        for hh in range(ATT_HG):
            kh, km, kl = _split3(x_ref[:, head_cols(hh)])
            gate = _dot_nt(kh, qs[hh]) + _dot_nt(km, qs[hh]) + _dot_nt(kl, qs[hh])
            gate = jnp.where(past, gate, -jnp.inf)
            rank = jnp.zeros((nblk, t), jnp.int32)
            for mth in range(nblk):
                gm = gate[mth:mth + 1, :]
                ahead = (gm > gate) | ((gm == gate) & (mth < blk_id))
                rank = rank + ahead.astype(jnp.int32)
            sel_ref[hh] = ((rank < MOBA_TOP_K) & past).astype(F32)

    def scores(hh, start):
        kb = k_ref[pl.ds(start, t), head_cols(hh)]
        if is_fox:
            kb = jnp.concatenate([kb, x_ref[pl.ds(start, t), :]], axis=1)
        return _dot_nt(kb, qs[hh])

    def mask_rows(hh, s, first_blk, diagonal):
        parts = []
        for r in range(ATT_SUB):
            keep = sel_ref[hh, pl.ds(first_blk + r, 1), :] > 0.5
            if diagonal:
                key = r * MOBA_BLOCK + lax.broadcasted_iota(jnp.int32, (MOBA_BLOCK, t), 0)
                qry = lax.broadcasted_iota(jnp.int32, (MOBA_BLOCK, t), 1)
                keep = keep | ((key <= qry) & (qry < (r + 1) * MOBA_BLOCK))
            parts.append(jnp.where(keep, s[r * MOBA_BLOCK:(r + 1) * MOBA_BLOCK, :], NEG_BIG))
        return jnp.concatenate(parts, axis=0)

    def vt_chunk(hh, start):
        return vt_ref[head_cols(hh), pl.ds(start, t)]

    def produce(dst_ref, start, first_blk, diagonal):
        cmax = []
        for hh in range(ATT_HG):
            s = scores(hh, start)
            if is_fox:
                if diagonal:
                    key = lax.broadcasted_iota(jnp.int32, (t, t), 0)
                    qry = lax.broadcasted_iota(jnp.int32, (t, t), 1)
                    s = jnp.where(key <= qry, s, NEG_BIG)
            else:
                s = mask_rows(hh, s, first_blk, diagonal)
            dst_ref[hh] = s
            cmax.append(jnp.max(s, axis=0, keepdims=True))
        return tuple(cmax)

    def consume(src_ref, start, cmax, stats):
        new = []
        for hh in range(ATT_HG):
            m, l = stats[hh]
            m_new = jnp.maximum(m, cmax[hh])
            alpha = jnp.exp2(m - m_new)
            p = jnp.exp2(src_ref[hh] - m_new)
            l = alpha * l + jnp.sum(p, axis=0, keepdims=True)
            acc_ref[head_cols(hh), :] = (alpha * acc_ref[head_cols(hh), :]
                                         + _dot(vt_chunk(hh, start), p.astype(BF16)))
            new.append((m_new, l))
        return tuple(new)

    own = pl.multiple_of(qt * t, t)
    acc_ref[...] = jnp.zeros_like(acc_ref)
    stats = tuple((jnp.full((1, t), NEG_BIG, F32), jnp.zeros((1, t), F32)) for _ in range(ATT_HG))
    cmax = produce(sa_ref, own, qt * ATT_SUB, True)

    def step(dst_ref, src_ref):
        def run(i, cmax, stats):
            nxt = pl.multiple_of(i * t, t)
            cur = pl.multiple_of(jnp.where(i == 0, qt, i - 1) * t, t)
            cmax_next = produce(dst_ref, nxt, i * ATT_SUB, False)
            return cmax_next, consume(src_ref, cur, cmax, stats)
        return run

    def body(i, carry):
        return lax.cond(i % 2 == 0, step(sb_ref, sa_ref), step(sa_ref, sb_ref), i, *carry)

    cmax, stats = lax.fori_loop(0, qt, body, (cmax, stats))
    last = pl.multiple_of(jnp.maximum(qt - 1, 0) * t, t)
    stats = lax.cond(qt % 2 == 0,
                     lambda c, s: consume(sa_ref, last, c, s),
                     lambda c, s: consume(sb_ref, last, c, s), cmax, stats)
    for hh in range(ATT_HG):
        _, l = stats[hh]
        o_ref[:, head_cols(hh)] = (acc_ref[head_cols(hh), :] * (1.0 / l)).T.astype(BF16)


def _moba_kernel(*refs):
    _attn_kernel(False, *refs)


def _fox_kernel(*refs):
    _attn_kernel(True, *refs)


def _attention(proj, kmean, faug, batch, seq):
    n = batch * seq
    nq = seq // ATT_T
    nblk = seq // MOBA_BLOCK
    hw = ATT_HG * HEAD_DIM
    grid = (batch, N_HEADS // ATT_HG, nq)
    params = pltpu.CompilerParams(
        dimension_semantics=("arbitrary", "arbitrary", "arbitrary"),
        vmem_limit_bytes=VMEM_LIMIT)

    def q_spec(col0):
        return pl.BlockSpec((ATT_T, hw), lambda b, h, i: (b * nq + i, col0 + h))

    def kv_spec(col0):
        return pl.BlockSpec((seq, hw), lambda b, h, i: (b, col0 + h))

    region = WIDTH // hw
    out_spec = pl.BlockSpec((ATT_T, hw), lambda b, h, i: (b * nq + i, h))
    out_shape = jax.ShapeDtypeStruct((n, WIDTH), BF16)
    common = [pltpu.VMEM((hw, seq), BF16), pltpu.VMEM((hw, ATT_T), F32),
              pltpu.VMEM((ATT_HG, ATT_T, ATT_T), F32), pltpu.VMEM((ATT_HG, ATT_T, ATT_T), F32)]

    o_moba = pl.pallas_call(
        _moba_kernel,
        grid=grid,
        in_specs=[q_spec(0), kv_spec(region), kv_spec(2 * region),
                  pl.BlockSpec((nblk, hw), lambda b, h, i: (b, h))],
        out_specs=out_spec,
        out_shape=out_shape,
        scratch_shapes=common + [pltpu.VMEM((ATT_HG, nblk, ATT_T), F32)],
        compiler_params=params,
        name="moba_attn",
    )(proj, proj, proj, kmean)

    o_fox = pl.pallas_call(
        _fox_kernel,
        grid=grid,
        in_specs=[q_spec(3 * region), kv_spec(4 * region), kv_spec(5 * region),
                  pl.BlockSpec((seq, LANES), lambda b, h, i: (b, 0))],
        out_specs=out_spec,
        out_shape=out_shape,
        scratch_shapes=common,
        compiler_params=params,
        name="fox_attn",
    )(proj, proj, proj, faug)
    return o_moba, o_fox


MIX_TM = 256


def _sigmoid(z):
    return 1.0 / (1.0 + jnp.exp(-z))


def _mix_kernel(om_ref, of_ref, ga_ref, gb_ref, x_ref, wm_ref, wf_ref, wo_ref, g_ref, out_ref):
    y_m = _dot(om_ref[...], wm_ref[...])
    y_f = _dot(of_ref[...], wf_ref[...])
    merged = _sigmoid(ga_ref[...].astype(F32)) * y_m + _sigmoid(gb_ref[...].astype(F32)) * y_f
    mixed = _dot(merged.astype(BF16), wo_ref[...])
    out_ref[...] = x_ref[...] + mixed * _rms_scale(mixed) * g_ref[...]


def _mix_out(o_moba, o_fox, proj, x2, wm, wf, wo, g):
    n = x2.shape[0]
    ga_blk = QKV_COLS // D_MODEL
    const = dict(pipeline_mode=pl.Buffered(1))
    return pl.pallas_call(
        _mix_kernel,
        grid=(n // MIX_TM,),
        in_specs=[
            pl.BlockSpec((MIX_TM, WIDTH), lambda i: (i, 0)),
            pl.BlockSpec((MIX_TM, WIDTH), lambda i: (i, 0)),
            pl.BlockSpec((MIX_TM, D_MODEL), lambda i: (i, ga_blk)),
            pl.BlockSpec((MIX_TM, D_MODEL), lambda i: (i, ga_blk + 1)),
            pl.BlockSpec((MIX_TM, D_MODEL), lambda i: (i, 0)),
            pl.BlockSpec((WIDTH, D_MODEL), lambda i: (0, 0), **const),
            pl.BlockSpec((WIDTH, D_MODEL), lambda i: (0, 0), **const),
            pl.BlockSpec((D_MODEL, D_MODEL), lambda i: (0, 0), **const),
            pl.BlockSpec((1, D_MODEL), lambda i: (0, 0)),
        ],
        out_specs=pl.BlockSpec((MIX_TM, D_MODEL), lambda i: (i, 0)),
        out_shape=jax.ShapeDtypeStruct((n, D_MODEL), F32),
        compiler_params=pltpu.CompilerParams(
            dimension_semantics=("arbitrary",), vmem_limit_bytes=VMEM_LIMIT),
        name="mix_out",
    )(o_moba, o_fox, proj, proj, x2, wm, wf, wo, g)


MLP_TM = 512
MLP_TF = 1024


def _mlp_kernel(x_ref, gpre_ref, wu_ref, wd_ref, gpost_ref, out_ref, h_ref, acc_ref):
    f = pl.program_id(1)

    @pl.when(f == 0)
    def _():
        x = x_ref[...]
        h_ref[...] = (x * _rms_scale(x) * gpre_ref[...]).astype(BF16)
        acc_ref[...] = jnp.zeros_like(acc_ref)

    u = _dot(h_ref[...], wu_ref[...])
    a = jnp.square(jnp.maximum(u, 0.0)).astype(BF16)
    acc_ref[...] += _dot(a, wd_ref[...])

    @pl.when(f == pl.num_programs(1) - 1)
    def _():
        mo = acc_ref[...]
        out_ref[...] = x_ref[...] + mo * _rms_scale(mo) * gpost_ref[...]


def _mlp(x1, gpre, wu, wd, gpost):
    n = x1.shape[0]
    return pl.pallas_call(
        _mlp_kernel,
        grid=(n // MLP_TM, D_FF // MLP_TF),
        in_specs=[
            pl.BlockSpec((MLP_TM, D_MODEL), lambda i, f: (i, 0)),
            pl.BlockSpec((1, D_MODEL), lambda i, f: (0, 0)),
            pl.BlockSpec((D_MODEL, MLP_TF), lambda i, f: (0, f)),
            pl.BlockSpec((MLP_TF, D_MODEL), lambda i, f: (f, 0)),
            pl.BlockSpec((1, D_MODEL), lambda i, f: (0, 0)),
        ],
        out_specs=pl.BlockSpec((MLP_TM, D_MODEL), lambda i, f: (i, 0)),
        out_shape=jax.ShapeDtypeStruct((n, D_MODEL), F32),
        scratch_shapes=[pltpu.VMEM((MLP_TM, D_MODEL), BF16), pltpu.VMEM((MLP_TM, D_MODEL), F32)],
        compiler_params=pltpu.CompilerParams(
            dimension_semantics=("arbitrary", "arbitrary"), vmem_limit_bytes=VMEM_LIMIT),
        name="mlp",
    )(x1, gpre, wu, wd, gpost)


def _rope_tables(seq):
    inv_freq = ROPE_THETA ** (-jnp.arange(0, ROPE_DIM, 2, dtype=F32) / ROPE_DIM)
    ang = jnp.arange(seq, dtype=F32)[:, None] * inv_freq[None, :]
    cos, sin = jnp.cos(ang), jnp.sin(ang)
    rest = HEAD_DIM - ROPE_DIM
    cos_t = jnp.concatenate([cos, cos, jnp.ones((seq, rest), F32)], axis=1)
    zeros_h = jnp.zeros((seq, ROPE_HALF), F32)
    zeros_r = jnp.zeros((seq, rest), F32)
    sina_t = jnp.concatenate([-sin, zeros_h, zeros_r], axis=1)
    sinb_t = jnp.concatenate([zeros_h, sin, zeros_r], axis=1)
    return cos_t, sina_t, sinb_t


def _layer(x2, w_in, b_forget, w_bm, w_bf, w_out, g_mix_pre, g_mix_post,
           w_up, w_down, g_mlp_pre, g_mlp_post, batch, seq):
    ff0 = QKV_COLS
    ff1 = QKV_COLS + N_HEADS
    wq = w_in[:, :ff0].astype(BF16)
    wg = w_in[:, ff1:].astype(BF16)
    wff = jnp.pad(w_in[:, ff0:ff1], ((0, 0), (0, LANES - N_HEADS))).astype(BF16)
    b_row = jnp.pad(b_forget, (0, LANES - N_HEADS)).reshape(1, LANES)
    cos_t, sina_t, sinb_t = _rope_tables(seq)

    proj, ff, kmean = _in_proj(x2, g_mix_pre.reshape(1, -1), wq, wg, wff, cos_t, sina_t, sinb_t, seq)
    faug = _fox_bias(ff, b_row, batch, seq)
    kmean = kmean.reshape(batch * (seq // MOBA_BLOCK), WIDTH)
    o_moba, o_fox = _attention(proj, kmean, faug, batch, seq)
    x1 = _mix_out(o_moba, o_fox, proj, x2, w_bm.astype(BF16), w_bf.astype(BF16),
                  w_out.astype(BF16), g_mix_post.reshape(1, -1))
    return _mlp(x1, g_mlp_pre.reshape(1, -1), w_up.astype(BF16), w_down.astype(BF16),
                g_mlp_post.reshape(1, -1))


def kernel(x, w_in, b_forget, w_branch_moba, w_branch_fox, w_out, g_mix_pre, g_mix_post,
           w_up, w_down, g_mlp_pre, g_mlp_post):
    batch, seq, d = x.shape
    assert d == D_MODEL and seq % IN_TM == 0 and seq % ATT_T == 0
    assert w_in.shape[-1] == PROJ_COLS + N_HEADS
    x2 = x.reshape(batch * seq, d)
    for l in range(w_in.shape[0]):
        x2 = _layer(x2, w_in[l], b_forget[l], w_branch_moba[l], w_branch_fox[l], w_out[l],
                    g_mix_pre[l], g_mix_post[l], w_up[l], w_down[l], g_mlp_pre[l],
                    g_mlp_post[l], batch, seq)
    return x2.reshape(batch, seq, d)
```

```python
import functools
import math

import jax
import jax.numpy as jnp
from jax import lax
from jax.experimental import pallas as pl
from jax.experimental.pallas import tpu as pltpu

D_MODEL = 2048
HEAD_DIM = 128
N_HEADS = 8
WIDTH = N_HEADS * HEAD_DIM
MOBA_BLOCK = 256
MOBA_TOP_K = 3
ROPE_THETA = 500000.0
ROPE_DIM = HEAD_DIM // 4
ROPE_HALF = ROPE_DIM // 2
D_FF = 4 * D_MODEL
RMS_EPS = 1e-6
NEG_BIG = -1e30
LANES = 128

QKV_COLS = 6 * WIDTH
GATE_COLS = 2 * D_MODEL
PROJ_COLS = QKV_COLS + GATE_COLS

VMEM_LIMIT = 56 * 1024 * 1024

BF16 = jnp.bfloat16
F32 = jnp.float32
LOG2E = math.log2(math.e)
Q_SCALE = HEAD_DIM ** -0.5 * LOG2E


def _rms_scale(x):
    return lax.rsqrt(jnp.mean(x * x, axis=-1, keepdims=True) + RMS_EPS)


def _split3(x):
    hi = x.astype(BF16)
    r1 = x - hi.astype(F32)
    mid = r1.astype(BF16)
    lo = (r1 - mid.astype(F32)).astype(BF16)
    return hi, mid, lo


def _dot(a, b):
    return jnp.dot(a, b, preferred_element_type=F32)


def _dot_nt(a, b):
    return lax.dot_general(a, b, (((1,), (1,)), ((), ())), preferred_element_type=F32)


WP_ROWS = 256


def _w_prep_kernel(w_ref, tail_ref, wq_ref, wg_ref, wff_ref):
    wq_ref[...] = w_ref[:, :QKV_COLS].astype(BF16)
    wff_ref[...] = w_ref[:, QKV_COLS:QKV_COLS + LANES].astype(BF16)
    g = pltpu.roll(w_ref[:, QKV_COLS:QKV_COLS + GATE_COLS], GATE_COLS - N_HEADS, 1)
    body = GATE_COLS - LANES
    wg_ref[:, :body] = g[:, :body].astype(BF16)
    lane = lax.broadcasted_iota(jnp.int32, (WP_ROWS, LANES), 1)
    last = jnp.where(lane >= LANES - N_HEADS, tail_ref[...], g[:, body:])
    wg_ref[:, body:] = last.astype(BF16)


def _w_prep(w_in):
    d = w_in.shape[0]
    win = QKV_COLS + GATE_COLS
    tail = jnp.pad(w_in[:, win:], ((0, 0), (LANES - N_HEADS, 0)))
    return pl.pallas_call(
        _w_prep_kernel,
        grid=(d // WP_ROWS,),
        in_specs=[pl.BlockSpec((WP_ROWS, win), lambda i: (i, 0)),
                  pl.BlockSpec((WP_ROWS, LANES), lambda i: (i, 0))],
        out_specs=[pl.BlockSpec((WP_ROWS, QKV_COLS), lambda i: (i, 0)),
                   pl.BlockSpec((WP_ROWS, GATE_COLS), lambda i: (i, 0)),
                   pl.BlockSpec((WP_ROWS, LANES), lambda i: (i, 0))],
        out_shape=[jax.ShapeDtypeStruct((d, QKV_COLS), BF16),
                   jax.ShapeDtypeStruct((d, GATE_COLS), BF16),
                   jax.ShapeDtypeStruct((d, LANES), BF16)],
        compiler_params=pltpu.CompilerParams(
            dimension_semantics=("arbitrary",), vmem_limit_bytes=VMEM_LIMIT),
        name="w_prep",
    )(w_in, tail)


IN_TM = 1024
IN_TN = 1024
IN_QKV_TILES = QKV_COLS // IN_TN


def _in_proj_kernel(x_ref, g_ref, wq_ref, wg_ref, wff_ref, cos_ref, sina_ref, sinb_ref,
                    proj_ref, ff_ref, kmean_ref, h_ref):
    j = pl.program_id(1)

    @pl.when(j == 0)
    def _():
        x = x_ref[...]
        h = (x * _rms_scale(x) * g_ref[...]).astype(BF16)
        h_ref[...] = h
        ff_ref[...] = _dot(h, wff_ref[...])

    def rotary(t):
        outs = []
        for hd in range(IN_TN // HEAD_DIM):
            th = t[:, hd * HEAD_DIM:(hd + 1) * HEAD_DIM]
            up = pltpu.roll(th, HEAD_DIM - ROPE_HALF, 1)
            dn = pltpu.roll(th, ROPE_HALF, 1)
            outs.append(th * cos_ref[...] + up * sina_ref[...] + dn * sinb_ref[...])
        return jnp.concatenate(outs, axis=1)

    @pl.when(j == 0)
    def _():
        acc = _dot(h_ref[...], wq_ref[...])
        proj_ref[...] = (rotary(acc) * Q_SCALE).astype(BF16)

    @pl.when(j == 1)
    def _():
        kr = rotary(_dot(h_ref[...], wq_ref[...]))
        proj_ref[...] = kr.astype(BF16)
        nblk = IN_TM // MOBA_BLOCK
        means = [jnp.mean(kr[b * MOBA_BLOCK:(b + 1) * MOBA_BLOCK, :], axis=0, keepdims=True)
                 for b in range(nblk)]
        kmean_ref[0] = jnp.concatenate(means, axis=0)

    @pl.when(j == 3)
    def _():
        proj_ref[...] = (_dot(h_ref[...], wq_ref[...]) * Q_SCALE).astype(BF16)

    @pl.when((j == 2) | (j == 4) | (j == 5))
    def _():
        proj_ref[...] = _dot(h_ref[...], wq_ref[...]).astype(BF16)

    @pl.when(j >= IN_QKV_TILES)
    def _():
        proj_ref[...] = _dot(h_ref[...], wg_ref[...]).astype(BF16)


def _in_proj(x2, g, wq, wg, wff, cos_t, sina_t, sinb_t, seq):
    n = x2.shape[0]
    seq_tiles = seq // IN_TM
    grid = (n // IN_TM, PROJ_COLS // IN_TN)
    last_q = IN_QKV_TILES - 1
    return pl.pallas_call(
        _in_proj_kernel,
        grid=grid,
        in_specs=[
            pl.BlockSpec((IN_TM, D_MODEL), lambda i, j: (i, 0)),
            pl.BlockSpec((1, D_MODEL), lambda i, j: (0, 0)),
            pl.BlockSpec((D_MODEL, IN_TN), lambda i, j: (0, jnp.minimum(j, last_q))),
            pl.BlockSpec((D_MODEL, IN_TN), lambda i, j: (0, jnp.maximum(j - IN_QKV_TILES, 0))),
            pl.BlockSpec((D_MODEL, LANES), lambda i, j: (0, 0)),
            pl.BlockSpec((IN_TM, HEAD_DIM), lambda i, j: (i % seq_tiles, 0)),
            pl.BlockSpec((IN_TM, HEAD_DIM), lambda i, j: (i % seq_tiles, 0)),
            pl.BlockSpec((IN_TM, HEAD_DIM), lambda i, j: (i % seq_tiles, 0)),
        ],
        out_specs=[
            pl.BlockSpec((IN_TM, IN_TN), lambda i, j: (i, j)),
            pl.BlockSpec((IN_TM, LANES), lambda i, j: (i, 0)),
            pl.BlockSpec((1, IN_TM // MOBA_BLOCK, WIDTH), lambda i, j: (i, 0, 0)),
        ],
        out_shape=[
            jax.ShapeDtypeStruct((n, PROJ_COLS), BF16),
            jax.ShapeDtypeStruct((n, LANES), F32),
            jax.ShapeDtypeStruct((n // IN_TM, IN_TM // MOBA_BLOCK, WIDTH), F32),
        ],
        scratch_shapes=[pltpu.VMEM((IN_TM, D_MODEL), BF16)],
        compiler_params=pltpu.CompilerParams(
            dimension_semantics=("arbitrary", "arbitrary"),
            vmem_limit_bytes=VMEM_LIMIT),
        name="in_proj",
    )(x2, g, wq, wg, wff, cos_t, sina_t, sinb_t)


FB_CHUNK = 256


def _fox_bias_kernel(ff_ref, b_ref, out_ref):
    seq = ff_ref.shape[0]
    r = lax.broadcasted_iota(jnp.int32, (FB_CHUNK, FB_CHUNK), 0)
    c = lax.broadcasted_iota(jnp.int32, (FB_CHUNK, FB_CHUNK), 1)
    ltri = (c <= r).astype(BF16)
    pr = lax.broadcasted_iota(jnp.int32, (LANES, LANES), 0)
    pc = lax.broadcasted_iota(jnp.int32, (LANES, LANES), 1)
    place = [((pc == 3 * pr + t) & (pr < N_HEADS)).astype(BF16) for t in range(3)]

    carry = jnp.zeros((1, LANES), F32)
    for ci in range(seq // FB_CHUNK):
        z = ff_ref[ci * FB_CHUNK:(ci + 1) * FB_CHUNK, :] + b_ref[...]
        logf = -(jnp.maximum(-z, 0.0) + jnp.log1p(jnp.exp(-jnp.abs(z))))
        hi, mid, lo = _split3(logf)
        f = _dot(ltri, hi) + _dot(ltri, mid) + _dot(ltri, lo) + carry
        carry = f[FB_CHUNK - 1:FB_CHUNK, :]
        nh, nm, nl = _split3(f * (-LOG2E))
        aug = _dot(nh, place[0]) + _dot(nm, place[1]) + _dot(nl, place[2])
        out_ref[ci * FB_CHUNK:(ci + 1) * FB_CHUNK, :] = aug.astype(BF16)


def _fox_bias(ff, b_row, batch, seq):
    return pl.pallas_call(
        _fox_bias_kernel,
        grid=(batch,),
        in_specs=[
            pl.BlockSpec((seq, LANES), lambda b: (b, 0)),
            pl.BlockSpec((1, LANES), lambda b: (0, 0)),
        ],
        out_specs=pl.BlockSpec((seq, LANES), lambda b: (b, 0)),
        out_shape=jax.ShapeDtypeStruct((batch * seq, LANES), BF16),
        compiler_params=pltpu.CompilerParams(
            dimension_semantics=("arbitrary",), vmem_limit_bytes=VMEM_LIMIT),
        name="fox_bias",
    )(ff, b_row)


ATT_T = 512
ATT_HG = 4
ATT_SUB = ATT_T // MOBA_BLOCK


def _attn_kernel(is_fox, n_cast, *refs):
    q_ref, k_ref, v_ref, x_ref = refs[:4]
    cast_src = refs[4:4 + n_cast]
    o_ref = refs[4 + n_cast]
    cast_dst = refs[5 + n_cast:5 + 2 * n_cast]
    vt_ref, acc_ref, sa_ref, sb_ref, *maybe_sel = refs[5 + 2 * n_cast:]
    hg = pl.program_id(1)
    qt = pl.program_id(2)
    seq = k_ref.shape[0]
    nblk = seq // MOBA_BLOCK
    t = ATT_T

    for src, dst in zip(cast_src, cast_dst):
        dst[...] = src[...].astype(BF16)

    def head_cols(hh):
        return slice(hh * HEAD_DIM, (hh + 1) * HEAD_DIM)

    @pl.when(qt == 0)
    def _():
        for hh in range(ATT_HG):
            for ci in range(nblk):
                rows = slice(ci * MOBA_BLOCK, (ci + 1) * MOBA_BLOCK)
                blk = v_ref[rows, head_cols(hh)].astype(F32)
                vt_ref[head_cols(hh), rows] = blk.T.astype(BF16)

    qs = []
    for hh in range(ATT_HG):
        qh = q_ref[:, head_cols(hh)]
        if is_fox:
            head = hg * ATT_HG + hh
            lane = lax.broadcasted_iota(jnp.int32, (t, LANES), 1)
            pick = ((lane >= 3 * head) & (lane < 3 * head + 3)).astype(BF16)
            qh = jnp.concatenate([qh, pick], axis=1)
        qs.append(qh)

    if not is_fox:
        sel_ref, = maybe_sel
        blk_id = lax.broadcasted_iota(jnp.int32, (nblk, t), 0)
        q_blk = qt * ATT_SUB + lax.broadcasted_iota(jnp.int32, (nblk, t), 1) // MOBA_BLOCK
        past = blk_id < q_blk
        for hh in range(ATT_HG):
            kh, km, kl = _split3(x_ref[:, head_cols(hh)])
            gate = _dot_nt(kh, qs[hh]) + _dot_nt(km, qs[hh]) + _dot_nt(kl, qs[hh])
            gate = jnp.where(past, gate, -jnp.inf)
            rank = jnp.zeros((nblk, t), jnp.int32)
            for mth in range(nblk):
                gm = gate[mth:mth + 1, :]
                ahead = (gm > gate) | ((gm == gate) & (mth < blk_id))
                rank = rank + ahead.astype(jnp.int32)
            sel_ref[hh] = ((rank < MOBA_TOP_K) & past).astype(F32)

    def scores(hh, start):
        kb = k_ref[pl.ds(start, t), head_cols(hh)]
        if is_fox:
            kb = jnp.concatenate([kb, x_ref[pl.ds(start, t), :]], axis=1)
        return _dot_nt(kb, qs[hh])

    def mask_rows(hh, s, first_blk, diagonal):
        parts = []
        for r in range(ATT_SUB):
            keep = sel_ref[hh, pl.ds(first_blk + r, 1), :] > 0.5
            if diagonal:
                key = r * MOBA_BLOCK + lax.broadcasted_iota(jnp.int32, (MOBA_BLOCK, t), 0)
                qry = lax.broadcasted_iota(jnp.int32, (MOBA_BLOCK, t), 1)
                keep = keep | ((key <= qry) & (qry < (r + 1) * MOBA_BLOCK))
            parts.append(jnp.where(keep, s[r * MOBA_BLOCK:(r + 1) * MOBA_BLOCK, :], NEG_BIG))
        return jnp.concatenate(parts, axis=0)

    def vt_chunk(hh, start):
        return vt_ref[head_cols(hh), pl.ds(start, t)]

    def produce(dst_ref, start, first_blk, diagonal):
        cmax = []
        for hh in range(ATT_HG):
            s = scores(hh, start)
            if is_fox:
                if diagonal:
                    key = lax.broadcasted_iota(jnp.int32, (t, t), 0)
                    qry = lax.broadcasted_iota(jnp.int32, (t, t), 1)
                    s = jnp.where(key <= qry, s, NEG_BIG)
            else:
                s = mask_rows(hh, s, first_blk, diagonal)
            dst_ref[hh] = s
            cmax.append(jnp.max(s, axis=0, keepdims=True))
        return tuple(cmax)

    def consume(src_ref, start, cmax, stats):
        new = []
        for hh in range(ATT_HG):
            m, l = stats[hh]
            m_new = jnp.maximum(m, cmax[hh])
            alpha = jnp.exp2(m - m_new)
            p = jnp.exp2(src_ref[hh] - m_new)
            l = alpha * l + jnp.sum(p, axis=0, keepdims=True)
            acc_ref[head_cols(hh), :] = (alpha * acc_ref[head_cols(hh), :]
                                         + _dot(vt_chunk(hh, start), p.astype(BF16)))
            new.append((m_new, l))
        return tuple(new)

    own = pl.multiple_of(qt * t, t)
    acc_ref[...] = jnp.zeros_like(acc_ref)
    stats = tuple((jnp.full((1, t), NEG_BIG, F32), jnp.zeros((1, t), F32)) for _ in range(ATT_HG))
    cmax = produce(sa_ref, own, qt * ATT_SUB, True)

    def step(dst_ref, src_ref):
        def run(i, cmax, stats):
            nxt = pl.multiple_of(i * t, t)
            cur = pl.multiple_of(jnp.where(i == 0, qt, i - 1) * t, t)
            cmax_next = produce(dst_ref, nxt, i * ATT_SUB, False)
            return cmax_next, consume(src_ref, cur, cmax, stats)
        return run

    def body(i, carry):
        return lax.cond(i % 2 == 0, step(sb_ref, sa_ref), step(sa_ref, sb_ref), i, *carry)

    cmax, stats = lax.fori_loop(0, qt, body, (cmax, stats))
    last = pl.multiple_of(jnp.maximum(qt - 1, 0) * t, t)
    stats = lax.cond(qt % 2 == 0,
                     lambda c, s: consume(sa_ref, last, c, s),
                     lambda c, s: consume(sb_ref, last, c, s), cmax, stats)
    for hh in range(ATT_HG):
        _, l = stats[hh]
        o_ref[:, head_cols(hh)] = (acc_ref[head_cols(hh), :] * (1.0 / l)).T.astype(BF16)


def _attention(proj, kmean, faug, moba_riders, fox_riders, batch, seq):
    n = batch * seq
    nq = seq // ATT_T
    nblk = seq // MOBA_BLOCK
    hw = ATT_HG * HEAD_DIM
    n_hg = N_HEADS // ATT_HG
    grid = (batch, n_hg, nq)
    n_steps = batch * n_hg * nq
    params = pltpu.CompilerParams(
        dimension_semantics=("arbitrary", "arbitrary", "arbitrary"),
        vmem_limit_bytes=VMEM_LIMIT)

    def q_spec(col0):
        return pl.BlockSpec((ATT_T, hw), lambda b, h, i: (b * nq + i, col0 + h))

    def kv_spec(col0):
        return pl.BlockSpec((seq, hw), lambda b, h, i: (b, col0 + h))

    def rider_spec(w):
        rows = w.shape[0] // n_steps
        assert rows * n_steps == w.shape[0] and rows % 16 == 0
        return pl.BlockSpec((rows, w.shape[1]), lambda b, h, i: ((b * n_hg + h) * nq + i, 0))

    region = WIDTH // hw
    out_spec = pl.BlockSpec((ATT_T, hw), lambda b, h, i: (b * nq + i, h))
    out_shape = jax.ShapeDtypeStruct((n, WIDTH), BF16)
    common = [pltpu.VMEM((hw, seq), BF16), pltpu.VMEM((hw, ATT_T), F32),
              pltpu.VMEM((ATT_HG, ATT_T, ATT_T), F32), pltpu.VMEM((ATT_HG, ATT_T, ATT_T), F32)]

    def call(is_fox, name, specs, args, riders, extra_scratch):
        rider_specs = [rider_spec(w) for w in riders]
        return pl.pallas_call(
            functools.partial(_attn_kernel, is_fox, len(riders)),
            grid=grid,
            in_specs=specs + rider_specs,
            out_specs=[out_spec] + rider_specs,
            out_shape=[out_shape] + [jax.ShapeDtypeStruct(w.shape, BF16) for w in riders],
            scratch_shapes=common + extra_scratch,
            compiler_params=params,
            name=name,
        )(*args, *riders)

    o_moba, *moba_cast = call(
        False, "moba_attn",
        [q_spec(0), kv_spec(region), kv_spec(2 * region),
         pl.BlockSpec((nblk, hw), lambda b, h, i: (b, h))],
        (proj, proj, proj, kmean), moba_riders, [pltpu.VMEM((ATT_HG, nblk, ATT_T), F32)])
    o_fox, *fox_cast = call(
        True, "fox_attn",
        [q_spec(3 * region), kv_spec(4 * region), kv_spec(5 * region),
         pl.BlockSpec((seq, LANES), lambda b, h, i: (b, 0))],
        (proj, proj, proj, faug), fox_riders, [])
    return o_moba, o_fox, moba_cast, fox_cast


MIX_TM = 256


def _sigmoid(z):
    return 1.0 / (1.0 + jnp.exp(-z))


def _mix_kernel(om_ref, of_ref, ga_ref, gb_ref, x_ref, wm_ref, wf_ref, wo_ref, g_ref, out_ref):
    y_m = _dot(om_ref[...], wm_ref[...])
    y_f = _dot(of_ref[...], wf_ref[...])
    merged = _sigmoid(ga_ref[...].astype(F32)) * y_m + _sigmoid(gb_ref[...].astype(F32)) * y_f
    mixed = _dot(merged.astype(BF16), wo_ref[...])
    out_ref[...] = x_ref[...] + mixed * _rms_scale(mixed) * g_ref[...]


def _mix_out(o_moba, o_fox, proj, x2, wm, wf, wo, g):
    n = x2.shape[0]
    ga_blk = QKV_COLS // D_MODEL
    const = dict(pipeline_mode=pl.Buffered(1))
    return pl.pallas_call(
        _mix_kernel,
        grid=(n // MIX_TM,),
        in_specs=[
            pl.BlockSpec((MIX_TM, WIDTH), lambda i: (i, 0)),
            pl.BlockSpec((MIX_TM, WIDTH), lambda i: (i, 0)),
            pl.BlockSpec((MIX_TM, D_MODEL), lambda i: (i, ga_blk)),
            pl.BlockSpec((MIX_TM, D_MODEL), lambda i: (i, ga_blk + 1)),
            pl.BlockSpec((MIX_TM, D_MODEL), lambda i: (i, 0)),
            pl.BlockSpec((WIDTH, D_MODEL), lambda i: (0, 0), **const),
            pl.BlockSpec((WIDTH, D_MODEL), lambda i: (0, 0), **const),
            pl.BlockSpec((D_MODEL, D_MODEL), lambda i: (0, 0), **const),
            pl.BlockSpec((1, D_MODEL), lambda i: (0, 0)),
        ],
        out_specs=pl.BlockSpec((MIX_TM, D_MODEL), lambda i: (i, 0)),
        out_shape=jax.ShapeDtypeStruct((n, D_MODEL), F32),
        compiler_params=pltpu.CompilerParams(
            dimension_semantics=("arbitrary",), vmem_limit_bytes=VMEM_LIMIT),
        name="mix_out",
    )(o_moba, o_fox, proj, proj, x2, wm, wf, wo, g)


MLP_TM = 512
MLP_TF = 1024


def _mlp_kernel(x_ref, gpre_ref, wu_ref, wd_ref, gpost_ref, out_ref, h_ref, acc_ref):
    f = pl.program_id(1)

    @pl.when(f == 0)
    def _():
        x = x_ref[...]
        h_ref[...] = (x * _rms_scale(x) * gpre_ref[...]).astype(BF16)
        acc_ref[...] = jnp.zeros_like(acc_ref)

    u = _dot(h_ref[...], wu_ref[...])
    a = jnp.square(jnp.maximum(u, 0.0)).astype(BF16)
    acc_ref[...] += _dot(a, wd_ref[...])

    @pl.when(f == pl.num_programs(1) - 1)
    def _():
        mo = acc_ref[...]
        out_ref[...] = x_ref[...] + mo * _rms_scale(mo) * gpost_ref[...]


def _mlp(x1, gpre, wu, wd, gpost):
    n = x1.shape[0]
    return pl.pallas_call(
        _mlp_kernel,
        grid=(n // MLP_TM, D_FF // MLP_TF),
        in_specs=[
            pl.BlockSpec((MLP_TM, D_MODEL), lambda i, f: (i, 0)),
            pl.BlockSpec((1, D_MODEL), lambda i, f: (0, 0)),
            pl.BlockSpec((D_MODEL, MLP_TF), lambda i, f: (0, f)),
            pl.BlockSpec((MLP_TF, D_MODEL), lambda i, f: (f, 0)),
            pl.BlockSpec((1, D_MODEL), lambda i, f: (0, 0)),
        ],
        out_specs=pl.BlockSpec((MLP_TM, D_MODEL), lambda i, f: (i, 0)),
        out_shape=jax.ShapeDtypeStruct((n, D_MODEL), F32),
        scratch_shapes=[pltpu.VMEM((MLP_TM, D_MODEL), BF16), pltpu.VMEM((MLP_TM, D_MODEL), F32)],
        compiler_params=pltpu.CompilerParams(
            dimension_semantics=("arbitrary", "arbitrary"), vmem_limit_bytes=VMEM_LIMIT),
        name="mlp",
    )(x1, gpre, wu, wd, gpost)


def _rope_tables(seq):
    inv_freq = ROPE_THETA ** (-jnp.arange(0, ROPE_DIM, 2, dtype=F32) / ROPE_DIM)
    ang = jnp.arange(seq, dtype=F32)[:, None] * inv_freq[None, :]
    cos, sin = jnp.cos(ang), jnp.sin(ang)
    rest = HEAD_DIM - ROPE_DIM
    cos_t = jnp.concatenate([cos, cos, jnp.ones((seq, rest), F32)], axis=1)
    zeros_h = jnp.zeros((seq, ROPE_HALF), F32)
    zeros_r = jnp.zeros((seq, rest), F32)
    sina_t = jnp.concatenate([-sin, zeros_h, zeros_r], axis=1)
    sinb_t = jnp.concatenate([zeros_h, sin, zeros_r], axis=1)
    return cos_t, sina_t, sinb_t


def _layer(x2, w_in, b_forget, w_bm, w_bf, w_out, g_mix_pre, g_mix_post,
           w_up, w_down, g_mlp_pre, g_mlp_post, batch, seq):
    wq, wg, wff = _w_prep(w_in)
    b_row = jnp.pad(b_forget, (0, LANES - N_HEADS)).reshape(1, LANES)
    cos_t, sina_t, sinb_t = _rope_tables(seq)

    proj, ff, kmean = _in_proj(x2, g_mix_pre.reshape(1, -1), wq, wg, wff, cos_t, sina_t, sinb_t, seq)
    faug = _fox_bias(ff, b_row, batch, seq)
    kmean = kmean.reshape(batch * (seq // MOBA_BLOCK), WIDTH)
    o_moba, o_fox, (wm, wf, wo), (wu, wd) = _attention(
        proj, kmean, faug, (w_bm, w_bf, w_out), (w_up, w_down), batch, seq)
    x1 = _mix_out(o_moba, o_fox, proj, x2, wm, wf, wo, g_mix_post.reshape(1, -1))
    return _mlp(x1, g_mlp_pre.reshape(1, -1), wu, wd, g_mlp_post.reshape(1, -1))


def kernel(x, w_in, b_forget, w_branch_moba, w_branch_fox, w_out, g_mix_pre, g_mix_post,
           w_up, w_down, g_mlp_pre, g_mlp_post):
    batch, seq, d = x.shape
    assert d == D_MODEL and seq % IN_TM == 0 and seq % ATT_T == 0
    assert w_in.shape[-1] == PROJ_COLS + N_HEADS
    x2 = x.reshape(batch * seq, d)
    for l in range(w_in.shape[0]):
        x2 = _layer(x2, w_in[l], b_forget[l], w_branch_moba[l], w_branch_fox[l], w_out[l],
                    g_mix_pre[l], g_mix_post[l], w_up[l], w_down[l], g_mlp_pre[l],
                    g_mlp_post[l], batch, seq)
    return x2.reshape(batch, seq, d)
```

```python
import functools
import math

import jax
import jax.numpy as jnp
from jax import lax
from jax.experimental import pallas as pl
from jax.experimental.pallas import tpu as pltpu

D_MODEL = 2048
HEAD_DIM = 128
N_HEADS = 8
WIDTH = N_HEADS * HEAD_DIM
MOBA_BLOCK = 256
MOBA_TOP_K = 3
ROPE_THETA = 500000.0
ROPE_DIM = HEAD_DIM // 4
ROPE_HALF = ROPE_DIM // 2
D_FF = 4 * D_MODEL
RMS_EPS = 1e-6
NEG_BIG = -1e30
LANES = 128
SUBLANES = 8

QKV_COLS = 6 * WIDTH
GATE_COLS = 2 * D_MODEL
PROJ_COLS = QKV_COLS + GATE_COLS

VMEM_LIMIT = 56 * 1024 * 1024

BF16 = jnp.bfloat16
F32 = jnp.float32
LOG2E = math.log2(math.e)
Q_SCALE = HEAD_DIM ** -0.5 * LOG2E


def _rms_scale(x):
    return lax.rsqrt(jnp.mean(x * x, axis=-1, keepdims=True) + RMS_EPS)


def _split3(x):
    hi = x.astype(BF16)
    r1 = x - hi.astype(F32)
    mid = r1.astype(BF16)
    lo = (r1 - mid.astype(F32)).astype(BF16)
    return hi, mid, lo


def _dot(a, b):
    return jnp.dot(a, b, preferred_element_type=F32)


def _dot_nt(a, b):
    return lax.dot_general(a, b, (((1,), (1,)), ((), ())), preferred_element_type=F32)


WP_COLS = 512


def _w_prep_kernel(wt_ref, fft_ref, w_ref, wff_ref):
    w_ref[...] = wt_ref[...].T.astype(BF16)

    @pl.when(pl.program_id(0) == 0)
    def _():
        wff_ref[...] = fft_ref[...].T.astype(BF16)


def _w_prep(w_in):
    wt = w_in.T
    d = wt.shape[1]
    n_qkv = QKV_COLS // WP_COLS
    gate0 = QKV_COLS + N_HEADS

    def src_row(i):
        return pl.multiple_of(jnp.where(i < n_qkv, i * WP_COLS, gate0 + (i - n_qkv) * WP_COLS),
                              SUBLANES)

    assert gate0 % SUBLANES == 0

    return pl.pallas_call(
        _w_prep_kernel,
        grid=(PROJ_COLS // WP_COLS,),
        in_specs=[pl.BlockSpec((pl.Element(WP_COLS), pl.Element(d)), lambda i: (src_row(i), 0)),
                  pl.BlockSpec((LANES, d), lambda i: (QKV_COLS // LANES, 0))],
        out_specs=[pl.BlockSpec((d, WP_COLS), lambda i: (0, i)),
                   pl.BlockSpec((d, LANES), lambda i: (0, 0))],
        out_shape=[jax.ShapeDtypeStruct((d, PROJ_COLS), BF16),
                   jax.ShapeDtypeStruct((d, LANES), BF16)],
        compiler_params=pltpu.CompilerParams(
            dimension_semantics=("arbitrary",), vmem_limit_bytes=VMEM_LIMIT),
        name="w_prep",
    )(wt, wt)


IN_TM = 1024
IN_TN = 1024
IN_QKV_TILES = QKV_COLS // IN_TN


def _in_proj_kernel(x_ref, g_ref, w_ref, wff_ref, cos_ref, sina_ref, sinb_ref,
                    proj_ref, ff_ref, kmean_ref, h_ref):
    j = pl.program_id(1)

    @pl.when(j == 0)
    def _():
        x = x_ref[...]
        h = (x * _rms_scale(x) * g_ref[...]).astype(BF16)
        h_ref[...] = h
        ff_ref[...] = _dot(h, wff_ref[...])

    def rotary(t):
        outs = []
        for hd in range(IN_TN // HEAD_DIM):
            th = t[:, hd * HEAD_DIM:(hd + 1) * HEAD_DIM]
            up = pltpu.roll(th, HEAD_DIM - ROPE_HALF, 1)
            dn = pltpu.roll(th, ROPE_HALF, 1)
            outs.append(th * cos_ref[...] + up * sina_ref[...] + dn * sinb_ref[...])
        return jnp.concatenate(outs, axis=1)

    acc = _dot(h_ref[...], w_ref[...])

    @pl.when(j == 0)
    def _():
        proj_ref[...] = (rotary(acc) * Q_SCALE).astype(BF16)

    @pl.when(j == 1)
    def _():
        kr = rotary(acc)
        proj_ref[...] = kr.astype(BF16)
        nblk = IN_TM // MOBA_BLOCK
        means = [jnp.mean(kr[b * MOBA_BLOCK:(b + 1) * MOBA_BLOCK, :], axis=0, keepdims=True)
                 for b in range(nblk)]
        kmean_ref[0] = jnp.concatenate(means, axis=0)

    @pl.when(j == 3)
    def _():
        proj_ref[...] = (acc * Q_SCALE).astype(BF16)

    @pl.when((j == 2) | (j >= 4))
    def _():
        proj_ref[...] = acc.astype(BF16)


def _in_proj(x2, g, w, wff, cos_t, sina_t, sinb_t, seq):
    n = x2.shape[0]
    seq_tiles = seq // IN_TM
    grid = (n // IN_TM, PROJ_COLS // IN_TN)
    return pl.pallas_call(
        _in_proj_kernel,
        grid=grid,
        in_specs=[
            pl.BlockSpec((IN_TM, D_MODEL), lambda i, j: (i, 0)),
            pl.BlockSpec((1, D_MODEL), lambda i, j: (0, 0)),
            pl.BlockSpec((D_MODEL, IN_TN), lambda i, j: (0, j)),
            pl.BlockSpec((D_MODEL, LANES), lambda i, j: (0, 0)),
            pl.BlockSpec((IN_TM, HEAD_DIM), lambda i, j: (i % seq_tiles, 0)),
            pl.BlockSpec((IN_TM, HEAD_DIM), lambda i, j: (i % seq_tiles, 0)),
            pl.BlockSpec((IN_TM, HEAD_DIM), lambda i, j: (i % seq_tiles, 0)),
        ],
        out_specs=[
            pl.BlockSpec((IN_TM, IN_TN), lambda i, j: (i, j)),
            pl.BlockSpec((IN_TM, LANES), lambda i, j: (i, 0)),
            pl.BlockSpec((1, IN_TM // MOBA_BLOCK, WIDTH), lambda i, j: (i, 0, 0)),
        ],
        out_shape=[
            jax.ShapeDtypeStruct((n, PROJ_COLS), BF16),
            jax.ShapeDtypeStruct((n, LANES), F32),
            jax.ShapeDtypeStruct((n // IN_TM, IN_TM // MOBA_BLOCK, WIDTH), F32),
        ],
        scratch_shapes=[pltpu.VMEM((IN_TM, D_MODEL), BF16)],
        compiler_params=pltpu.CompilerParams(
            dimension_semantics=("arbitrary", "arbitrary"),
            vmem_limit_bytes=VMEM_LIMIT),
        name="in_proj",
    )(x2, g, w, wff, cos_t, sina_t, sinb_t)


FB_CHUNK = 256


def _fox_bias_kernel(ff_ref, b_ref, out_ref):
    seq = ff_ref.shape[0]
    r = lax.broadcasted_iota(jnp.int32, (FB_CHUNK, FB_CHUNK), 0)
    c = lax.broadcasted_iota(jnp.int32, (FB_CHUNK, FB_CHUNK), 1)
    ltri = (c <= r).astype(BF16)
    pr = lax.broadcasted_iota(jnp.int32, (LANES, LANES), 0)
    pc = lax.broadcasted_iota(jnp.int32, (LANES, LANES), 1)
    place = [((pc == 3 * pr + t) & (pr < N_HEADS)).astype(BF16) for t in range(3)]

    carry = jnp.zeros((1, LANES), F32)
    for ci in range(seq // FB_CHUNK):
        z = ff_ref[ci * FB_CHUNK:(ci + 1) * FB_CHUNK, :] + b_ref[...]
        logf = -(jnp.maximum(-z, 0.0) + jnp.log1p(jnp.exp(-jnp.abs(z))))
        hi, mid, lo = _split3(logf)
        f = _dot(ltri, hi) + _dot(ltri, mid) + _dot(ltri, lo) + carry
        carry = f[FB_CHUNK - 1:FB_CHUNK, :]
        nh, nm, nl = _split3(f * (-LOG2E))
        aug = _dot(nh, place[0]) + _dot(nm, place[1]) + _dot(nl, place[2])
        out_ref[ci * FB_CHUNK:(ci + 1) * FB_CHUNK, :] = aug.astype(BF16)


def _fox_bias(ff, b_row, batch, seq):
    return pl.pallas_call(
        _fox_bias_kernel,
        grid=(batch,),
        in_specs=[
            pl.BlockSpec((seq, LANES), lambda b: (b, 0)),
            pl.BlockSpec((1, LANES), lambda b: (0, 0)),
        ],
        out_specs=pl.BlockSpec((seq, LANES), lambda b: (b, 0)),
        out_shape=jax.ShapeDtypeStruct((batch * seq, LANES), BF16),
        compiler_params=pltpu.CompilerParams(
            dimension_semantics=("arbitrary",), vmem_limit_bytes=VMEM_LIMIT),
        name="fox_bias",
    )(ff, b_row)


ATT_T = 512
ATT_HG = 4
ATT_SUB = ATT_T // MOBA_BLOCK


def _attn_kernel(is_fox, n_cast, *refs):
    q_ref, k_ref, v_ref, x_ref = refs[:4]
    cast_src = refs[4:4 + n_cast]
    o_ref = refs[4 + n_cast]
    cast_dst = refs[5 + n_cast:5 + 2 * n_cast]
    vt_ref, acc_ref, sa_ref, sb_ref, *maybe_sel = refs[5 + 2 * n_cast:]
    hg = pl.program_id(1)
    qt = pl.program_id(2)
    seq = k_ref.shape[0]
    nblk = seq // MOBA_BLOCK
    t = ATT_T

    for src, dst in zip(cast_src, cast_dst):
        dst[...] = src[...].astype(BF16)

    def head_cols(hh):
        return slice(hh * HEAD_DIM, (hh + 1) * HEAD_DIM)

    @pl.when(qt == 0)
    def _():
        for hh in range(ATT_HG):
            for ci in range(nblk):
                rows = slice(ci * MOBA_BLOCK, (ci + 1) * MOBA_BLOCK)
                blk = v_ref[rows, head_cols(hh)].astype(F32)
                vt_ref[head_cols(hh), rows] = blk.T.astype(BF16)

    qs = []
    for hh in range(ATT_HG):
        qh = q_ref[:, head_cols(hh)]
        if is_fox:
            head = hg * ATT_HG + hh
            lane = lax.broadcasted_iota(jnp.int32, (t, LANES), 1)
            pick = ((lane >= 3 * head) & (lane < 3 * head + 3)).astype(BF16)
            qh = jnp.concatenate([qh, pick], axis=1)
        qs.append(qh)

    if not is_fox:
        sel_ref, = maybe_sel
        blk_id = lax.broadcasted_iota(jnp.int32, (nblk, t), 0)
        q_blk = qt * ATT_SUB + lax.broadcasted_iota(jnp.int32, (nblk, t), 1) // MOBA_BLOCK
        past = blk_id < q_blk
        for hh in range(ATT_HG):
            kh, km, kl = _split3(x_ref[:, head_cols(hh)])
            gate = _dot_nt(kh, qs[hh]) + _dot_nt(km, qs[hh]) + _dot_nt(kl, qs[hh])
            gate = jnp.where(past, gate, -jnp.inf)
            rank = jnp.zeros((nblk, t), jnp.int32)
            for mth in range(nblk):
                gm = gate[mth:mth + 1, :]
                ahead = (gm > gate) | ((gm == gate) & (mth < blk_id))
                rank = rank + ahead.astype(jnp.int32)
            sel_ref[hh] = ((rank < MOBA_TOP_K) & past).astype(F32)

    def scores(hh, start):
        kb = k_ref[pl.ds(start, t), head_cols(hh)]
        if is_fox:
            kb = jnp.concatenate([kb, x_ref[pl.ds(start, t), :]], axis=1)
        return _dot_nt(kb, qs[hh])

    def mask_rows(hh, s, first_blk, diagonal):
        parts = []
        for r in range(ATT_SUB):
            keep = sel_ref[hh, pl.ds(first_blk + r, 1), :] > 0.5
            if diagonal:
                key = r * MOBA_BLOCK + lax.broadcasted_iota(jnp.int32, (MOBA_BLOCK, t), 0)
                qry = lax.broadcasted_iota(jnp.int32, (MOBA_BLOCK, t), 1)
                keep = keep | ((key <= qry) & (qry < (r + 1) * MOBA_BLOCK))
            parts.append(jnp.where(keep, s[r * MOBA_BLOCK:(r + 1) * MOBA_BLOCK, :], NEG_BIG))
        return jnp.concatenate(parts, axis=0)

    def vt_chunk(hh, start):
        return vt_ref[head_cols(hh), pl.ds(start, t)]

    def produce(dst_ref, start, first_blk, diagonal):
        cmax = []
        for hh in range(ATT_HG):
            s = scores(hh, start)
            if is_fox:
                if diagonal:
                    key = lax.broadcasted_iota(jnp.int32, (t, t), 0)
                    qry = lax.broadcasted_iota(jnp.int32, (t, t), 1)
                    s = jnp.where(key <= qry, s, NEG_BIG)
            else:
                s = mask_rows(hh, s, first_blk, diagonal)
            dst_ref[hh] = s
            cmax.append(jnp.max(s, axis=0, keepdims=True))
        return tuple(cmax)

    def consume(src_ref, start, cmax, stats):
        new = []
        for hh in range(ATT_HG):
            m, l = stats[hh]
            m_new = jnp.maximum(m, cmax[hh])
            alpha = jnp.exp2(m - m_new)
            p = jnp.exp2(src_ref[hh] - m_new)
            l = alpha * l + jnp.sum(p, axis=0, keepdims=True)
            acc_ref[head_cols(hh), :] = (alpha * acc_ref[head_cols(hh), :]
                                         + _dot(vt_chunk(hh, start), p.astype(BF16)))
            new.append((m_new, l))
        return tuple(new)

    own = pl.multiple_of(qt * t, t)
    acc_ref[...] = jnp.zeros_like(acc_ref)
    stats = tuple((jnp.full((1, t), NEG_BIG, F32), jnp.zeros((1, t), F32)) for _ in range(ATT_HG))
    cmax = produce(sa_ref, own, qt * ATT_SUB, True)

    def step(dst_ref, src_ref):
        def run(i, cmax, stats):
            nxt = pl.multiple_of(i * t, t)
            cur = pl.multiple_of(jnp.where(i == 0, qt, i - 1) * t, t)
            cmax_next = produce(dst_ref, nxt, i * ATT_SUB, False)
            return cmax_next, consume(src_ref, cur, cmax, stats)
        return run

    def body(i, carry):
        return lax.cond(i % 2 == 0, step(sb_ref, sa_ref), step(sa_ref, sb_ref), i, *carry)

    cmax, stats = lax.fori_loop(0, qt, body, (cmax, stats))
    last = pl.multiple_of(jnp.maximum(qt - 1, 0) * t, t)
    stats = lax.cond(qt % 2 == 0,
                     lambda c, s: consume(sa_ref, last, c, s),
                     lambda c, s: consume(sb_ref, last, c, s), cmax, stats)
    for hh in range(ATT_HG):
        _, l = stats[hh]
        o_ref[:, head_cols(hh)] = (acc_ref[head_cols(hh), :] * (1.0 / l)).T.astype(BF16)


def _attention(proj, kmean, faug, moba_riders, fox_riders, batch, seq):
    n = batch * seq
    nq = seq // ATT_T
    nblk = seq // MOBA_BLOCK
    hw = ATT_HG * HEAD_DIM
    n_hg = N_HEADS // ATT_HG
    grid = (batch, n_hg, nq)
    n_steps = batch * n_hg * nq
    params = pltpu.CompilerParams(
        dimension_semantics=("arbitrary", "arbitrary", "arbitrary"),
        vmem_limit_bytes=VMEM_LIMIT)

    def q_spec(col0):
        return pl.BlockSpec((ATT_T, hw), lambda b, h, i: (b * nq + i, col0 + h))

    def kv_spec(col0):
        return pl.BlockSpec((seq, hw), lambda b, h, i: (b, col0 + h))

    def rider_spec(w):
        rows = w.shape[0] // n_steps
        assert rows * n_steps == w.shape[0] and rows % 16 == 0
        return pl.BlockSpec((rows, w.shape[1]), lambda b, h, i: ((b * n_hg + h) * nq + i, 0))

    region = WIDTH // hw
    out_spec = pl.BlockSpec((ATT_T, hw), lambda b, h, i: (b * nq + i, h))
    out_shape = jax.ShapeDtypeStruct((n, WIDTH), BF16)
    common = [pltpu.VMEM((hw, seq), BF16), pltpu.VMEM((hw, ATT_T), F32),
              pltpu.VMEM((ATT_HG, ATT_T, ATT_T), F32), pltpu.VMEM((ATT_HG, ATT_T, ATT_T), F32)]

    def call(is_fox, name, specs, args, riders, extra_scratch):
        rider_specs = [rider_spec(w) for w in riders]
        return pl.pallas_call(
            functools.partial(_attn_kernel, is_fox, len(riders)),
            grid=grid,
            in_specs=specs + rider_specs,
            out_specs=[out_spec] + rider_specs,
            out_shape=[out_shape] + [jax.ShapeDtypeStruct(w.shape, BF16) for w in riders],
            scratch_shapes=common + extra_scratch,
            compiler_params=params,
            name=name,
        )(*args, *riders)

    o_moba, *moba_cast = call(
        False, "moba_attn",
        [q_spec(0), kv_spec(region), kv_spec(2 * region),
         pl.BlockSpec((nblk, hw), lambda b, h, i: (b, h))],
        (proj, proj, proj, kmean), moba_riders, [pltpu.VMEM((ATT_HG, nblk, ATT_T), F32)])
    o_fox, *fox_cast = call(
        True, "fox_attn",
        [q_spec(3 * region), kv_spec(4 * region), kv_spec(5 * region),
         pl.BlockSpec((seq, LANES), lambda b, h, i: (b, 0))],
        (proj, proj, proj, faug), fox_riders, [])
    return o_moba, o_fox, moba_cast, fox_cast


MIX_TM = 256


def _sigmoid(z):
    return 1.0 / (1.0 + jnp.exp(-z))


def _mix_kernel(om_ref, of_ref, ga_ref, gb_ref, x_ref, wm_ref, wf_ref, wo_ref, g_ref, out_ref):
    y_m = _dot(om_ref[...], wm_ref[...])
    y_f = _dot(of_ref[...], wf_ref[...])
    merged = _sigmoid(ga_ref[...].astype(F32)) * y_m + _sigmoid(gb_ref[...].astype(F32)) * y_f
    mixed = _dot(merged.astype(BF16), wo_ref[...])
    out_ref[...] = x_ref[...] + mixed * _rms_scale(mixed) * g_ref[...]


def _mix_out(o_moba, o_fox, proj, x2, wm, wf, wo, g):
    n = x2.shape[0]
    ga_blk = QKV_COLS // D_MODEL
    const = dict(pipeline_mode=pl.Buffered(1))
    return pl.pallas_call(
        _mix_kernel,
        grid=(n // MIX_TM,),
        in_specs=[
            pl.BlockSpec((MIX_TM, WIDTH), lambda i: (i, 0)),
            pl.BlockSpec((MIX_TM, WIDTH), lambda i: (i, 0)),
            pl.BlockSpec((MIX_TM, D_MODEL), lambda i: (i, ga_blk)),
            pl.BlockSpec((MIX_TM, D_MODEL), lambda i: (i, ga_blk + 1)),
            pl.BlockSpec((MIX_TM, D_MODEL), lambda i: (i, 0)),
            pl.BlockSpec((WIDTH, D_MODEL), lambda i: (0, 0), **const),
            pl.BlockSpec((WIDTH, D_MODEL), lambda i: (0, 0), **const),
            pl.BlockSpec((D_MODEL, D_MODEL), lambda i: (0, 0), **const),
            pl.BlockSpec((1, D_MODEL), lambda i: (0, 0)),
        ],
        out_specs=pl.BlockSpec((MIX_TM, D_MODEL), lambda i: (i, 0)),
        out_shape=jax.ShapeDtypeStruct((n, D_MODEL), F32),
        compiler_params=pltpu.CompilerParams(
            dimension_semantics=("arbitrary",), vmem_limit_bytes=VMEM_LIMIT),
        name="mix_out",
    )(o_moba, o_fox, proj, proj, x2, wm, wf, wo, g)


MLP_TM = 512
MLP_TF = 1024


def _mlp_kernel(x_ref, gpre_ref, wu_ref, wd_ref, gpost_ref, out_ref, h_ref, acc_ref):
    f = pl.program_id(1)

    @pl.when(f == 0)
    def _():
        x = x_ref[...]
        h_ref[...] = (x * _rms_scale(x) * gpre_ref[...]).astype(BF16)
        acc_ref[...] = jnp.zeros_like(acc_ref)

    u = _dot(h_ref[...], wu_ref[...])
    a = jnp.square(jnp.maximum(u, 0.0)).astype(BF16)
    acc_ref[...] += _dot(a, wd_ref[...])

    @pl.when(f == pl.num_programs(1) - 1)
    def _():
        mo = acc_ref[...]
        out_ref[...] = x_ref[...] + mo * _rms_scale(mo) * gpost_ref[...]


def _mlp(x1, gpre, wu, wd, gpost):
    n = x1.shape[0]
    return pl.pallas_call(
        _mlp_kernel,
        grid=(n // MLP_TM, D_FF // MLP_TF),
        in_specs=[
            pl.BlockSpec((MLP_TM, D_MODEL), lambda i, f: (i, 0)),
            pl.BlockSpec((1, D_MODEL), lambda i, f: (0, 0)),
            pl.BlockSpec((D_MODEL, MLP_TF), lambda i, f: (0, f)),
            pl.BlockSpec((MLP_TF, D_MODEL), lambda i, f: (f, 0)),
            pl.BlockSpec((1, D_MODEL), lambda i, f: (0, 0)),
        ],
        out_specs=pl.BlockSpec((MLP_TM, D_MODEL), lambda i, f: (i, 0)),
        out_shape=jax.ShapeDtypeStruct((n, D_MODEL), F32),
        scratch_shapes=[pltpu.VMEM((MLP_TM, D_MODEL), BF16), pltpu.VMEM((MLP_TM, D_MODEL), F32)],
        compiler_params=pltpu.CompilerParams(
            dimension_semantics=("arbitrary", "arbitrary"), vmem_limit_bytes=VMEM_LIMIT),
        name="mlp",
    )(x1, gpre, wu, wd, gpost)


def _rope_tables(seq):
    inv_freq = ROPE_THETA ** (-jnp.arange(0, ROPE_DIM, 2, dtype=F32) / ROPE_DIM)
    ang = jnp.arange(seq, dtype=F32)[:, None] * inv_freq[None, :]
    cos, sin = jnp.cos(ang), jnp.sin(ang)
    rest = HEAD_DIM - ROPE_DIM
    cos_t = jnp.concatenate([cos, cos, jnp.ones((seq, rest), F32)], axis=1)
    zeros_h = jnp.zeros((seq, ROPE_HALF), F32)
    zeros_r = jnp.zeros((seq, rest), F32)
    sina_t = jnp.concatenate([-sin, zeros_h, zeros_r], axis=1)
    sinb_t = jnp.concatenate([zeros_h, sin, zeros_r], axis=1)
    return cos_t, sina_t, sinb_t


def _layer(x2, w_in, b_forget, w_bm, w_bf, w_out, g_mix_pre, g_mix_post,
           w_up, w_down, g_mlp_pre, g_mlp_post, batch, seq):
    w, wff = _w_prep(w_in)
    b_row = jnp.pad(b_forget, (0, LANES - N_HEADS)).reshape(1, LANES)
    cos_t, sina_t, sinb_t = _rope_tables(seq)

    proj, ff, kmean = _in_proj(x2, g_mix_pre.reshape(1, -1), w, wff, cos_t, sina_t, sinb_t, seq)
    faug = _fox_bias(ff, b_row, batch, seq)
    kmean = kmean.reshape(batch * (seq // MOBA_BLOCK), WIDTH)
    o_moba, o_fox, (wm, wf, wo), (wu, wd) = _attention(
        proj, kmean, faug, (w_bm, w_bf, w_out), (w_up, w_down), batch, seq)
    x1 = _mix_out(o_moba, o_fox, proj, x2, wm, wf, wo, g_mix_post.reshape(1, -1))
    return _mlp(x1, g_mlp_pre.reshape(1, -1), wu, wd, g_mlp_post.reshape(1, -1))


def kernel(x, w_in, b_forget, w_branch_moba, w_branch_fox, w_out, g_mix_pre, g_mix_post,
           w_up, w_down, g_mlp_pre, g_mlp_post):
    batch, seq, d = x.shape
    assert d == D_MODEL and seq % IN_TM == 0 and seq % ATT_T == 0
    assert w_in.shape[-1] == PROJ_COLS + N_HEADS
    x2 = x.reshape(batch * seq, d)
    for l in range(w_in.shape[0]):
        x2 = _layer(x2, w_in[l], b_forget[l], w_branch_moba[l], w_branch_fox[l], w_out[l],
                    g_mix_pre[l], g_mix_post[l], w_up[l], w_down[l], g_mlp_pre[l],
                    g_mlp_post[l], batch, seq)
    return x2.reshape(batch, seq, d)
```

```python
import functools
import math

import jax
import jax.numpy as jnp
from jax import lax
from jax.experimental import pallas as pl
from jax.experimental.pallas import tpu as pltpu

D_MODEL = 2048
HEAD_DIM = 128
N_HEADS = 8
WIDTH = N_HEADS * HEAD_DIM
MOBA_BLOCK = 256
MOBA_TOP_K = 3
ROPE_THETA = 500000.0
ROPE_DIM = HEAD_DIM // 4
ROPE_HALF = ROPE_DIM // 2
D_FF = 4 * D_MODEL
RMS_EPS = 1e-6
NEG_BIG = -1e30
LANES = 128
SUBLANES = 8

QKV_COLS = 6 * WIDTH
GATE_COLS = 2 * D_MODEL
PROJ_COLS = QKV_COLS + GATE_COLS

VMEM_LIMIT = 56 * 1024 * 1024

BF16 = jnp.bfloat16
F32 = jnp.float32
LOG2E = math.log2(math.e)
Q_SCALE = HEAD_DIM ** -0.5 * LOG2E


def _rms_scale(x):
    return lax.rsqrt(jnp.mean(x * x, axis=-1, keepdims=True) + RMS_EPS)


def _split3(x):
    hi = x.astype(BF16)
    r1 = x - hi.astype(F32)
    mid = r1.astype(BF16)
    lo = (r1 - mid.astype(F32)).astype(BF16)
    return hi, mid, lo


def _dot(a, b):
    return jnp.dot(a, b, preferred_element_type=F32)


def _dot_nt(a, b):
    return lax.dot_general(a, b, (((1,), (1,)), ((), ())), preferred_element_type=F32)


WP_COLS = 512


def _w_prep_kernel(wt_ref, fft_ref, w_ref, wff_ref):
    i = pl.program_id(0)

    @pl.when(i >= 2 * WIDTH // WP_COLS)
    def _():
        w_ref[...] = wt_ref[...].T.astype(BF16)

    @pl.when(i < 2 * WIDTH // WP_COLS)
    def _():
        blk = wt_ref[...]
        mid = HEAD_DIM // 2 + ROPE_HALF
        parts = []
        for hd in range(WP_COLS // HEAD_DIM):
            b = hd * HEAD_DIM
            parts += [blk[b:b + ROPE_HALF], blk[b + ROPE_DIM:b + mid],
                      blk[b + ROPE_HALF:b + ROPE_DIM], blk[b + mid:b + HEAD_DIM]]
        w_ref[...] = jnp.concatenate(parts, axis=0).T.astype(BF16)

    @pl.when(i == 0)
    def _():
        wff_ref[...] = fft_ref[...].T.astype(BF16)


def _w_prep(w_in):
    wt = w_in.T
    d = wt.shape[1]
    n_qkv = QKV_COLS // WP_COLS
    gate0 = QKV_COLS + N_HEADS

    def src_row(i):
        return pl.multiple_of(jnp.where(i < n_qkv, i * WP_COLS, gate0 + (i - n_qkv) * WP_COLS),
                              SUBLANES)

    assert gate0 % SUBLANES == 0

    return pl.pallas_call(
        _w_prep_kernel,
        grid=(PROJ_COLS // WP_COLS,),
        in_specs=[pl.BlockSpec((pl.Element(WP_COLS), pl.Element(d)), lambda i: (src_row(i), 0)),
                  pl.BlockSpec((LANES, d), lambda i: (QKV_COLS // LANES, 0))],
        out_specs=[pl.BlockSpec((d, WP_COLS), lambda i: (0, i)),
                   pl.BlockSpec((d, LANES), lambda i: (0, 0))],
        out_shape=[jax.ShapeDtypeStruct((d, PROJ_COLS), BF16),
                   jax.ShapeDtypeStruct((d, LANES), BF16)],
        compiler_params=pltpu.CompilerParams(
            dimension_semantics=("arbitrary",), vmem_limit_bytes=VMEM_LIMIT),
        name="w_prep",
    )(wt, wt)


IN_TM = 1024
IN_TN = 1024
IN_QKV_TILES = QKV_COLS // IN_TN


def _in_proj_kernel(x_ref, g_ref, w_ref, wff_ref, cos_ref, sin_ref,
                    proj_ref, ff_ref, kmean_ref, h_ref):
    j = pl.program_id(1)

    @pl.when(j == 0)
    def _():
        x = x_ref[...]
        h = (x * _rms_scale(x) * g_ref[...]).astype(BF16)
        h_ref[...] = h
        ff_ref[...] = _dot(h, wff_ref[...])

    def tile():
        return _dot(h_ref[...], w_ref[...])

    def rotary(t):
        outs = []
        for hd in range(IN_TN // HEAD_DIM):
            th = t[:, hd * HEAD_DIM:(hd + 1) * HEAD_DIM]
            outs.append(th * cos_ref[...] + pltpu.roll(th, HEAD_DIM // 2, 1) * sin_ref[...])
        return jnp.concatenate(outs, axis=1)

    @pl.when(j == 0)
    def _():
        proj_ref[...] = (rotary(tile()) * Q_SCALE).astype(BF16)

    @pl.when(j == 1)
    def _():
        kr = rotary(tile())
        proj_ref[...] = kr.astype(BF16)
        nblk = IN_TM // MOBA_BLOCK
        means = [jnp.mean(kr[b * MOBA_BLOCK:(b + 1) * MOBA_BLOCK, :], axis=0, keepdims=True)
                 for b in range(nblk)]
        kmean_ref[0] = jnp.concatenate(means, axis=0)

    @pl.when(j == 3)
    def _():
        proj_ref[...] = (tile() * Q_SCALE).astype(BF16)

    @pl.when((j == 2) | (j >= 4))
    def _():
        proj_ref[...] = tile().astype(BF16)


def _in_proj(x2, g, w, wff, cos_t, sin_t, seq):
    n = x2.shape[0]
    seq_tiles = seq // IN_TM
    grid = (n // IN_TM, PROJ_COLS // IN_TN)
    return pl.pallas_call(
        _in_proj_kernel,
        grid=grid,
        in_specs=[
            pl.BlockSpec((IN_TM, D_MODEL), lambda i, j: (i, 0)),
            pl.BlockSpec((1, D_MODEL), lambda i, j: (0, 0)),
            pl.BlockSpec((D_MODEL, IN_TN), lambda i, j: (0, j)),
            pl.BlockSpec((D_MODEL, LANES), lambda i, j: (0, 0)),
            pl.BlockSpec((IN_TM, HEAD_DIM), lambda i, j: (i % seq_tiles, 0)),
            pl.BlockSpec((IN_TM, HEAD_DIM), lambda i, j: (i % seq_tiles, 0)),
        ],
        out_specs=[
            pl.BlockSpec((IN_TM, IN_TN), lambda i, j: (i, j)),
            pl.BlockSpec((IN_TM, LANES), lambda i, j: (i, 0)),
            pl.BlockSpec((1, IN_TM // MOBA_BLOCK, WIDTH), lambda i, j: (i, 0, 0)),
        ],
        out_shape=[
            jax.ShapeDtypeStruct((n, PROJ_COLS), BF16),
            jax.ShapeDtypeStruct((n, LANES), F32),
            jax.ShapeDtypeStruct((n // IN_TM, IN_TM // MOBA_BLOCK, WIDTH), F32),
        ],
        scratch_shapes=[pltpu.VMEM((IN_TM, D_MODEL), BF16)],
        compiler_params=pltpu.CompilerParams(
            dimension_semantics=("arbitrary", "arbitrary"),
            vmem_limit_bytes=VMEM_LIMIT),
        name="in_proj",
    )(x2, g, w, wff, cos_t, sin_t)


FB_CHUNK = 256


def _fox_bias_kernel(ff_ref, b_ref, out_ref):
    seq = ff_ref.shape[0]
    r = lax.broadcasted_iota(jnp.int32, (FB_CHUNK, FB_CHUNK), 0)
    c = lax.broadcasted_iota(jnp.int32, (FB_CHUNK, FB_CHUNK), 1)
    ltri = (c <= r).astype(BF16)
    pr = lax.broadcasted_iota(jnp.int32, (LANES, LANES), 0)
    pc = lax.broadcasted_iota(jnp.int32, (LANES, LANES), 1)
    place = [((pc == 3 * pr + t) & (pr < N_HEADS)).astype(BF16) for t in range(3)]

    local = []
    for ci in range(seq // FB_CHUNK):
        z = ff_ref[ci * FB_CHUNK:(ci + 1) * FB_CHUNK, :] + b_ref[...]
        logf = -(jnp.maximum(-z, 0.0) + jnp.log1p(jnp.exp(-jnp.abs(z))))
        hi, mid, lo = _split3(logf)
        local.append(_dot(ltri, hi) + _dot(ltri, mid) + _dot(ltri, lo))
    carry = jnp.zeros((1, LANES), F32)
    for ci, loc in enumerate(local):
        f = loc + carry
        carry = carry + loc[FB_CHUNK - 1:FB_CHUNK, :]
        nh, nm, nl = _split3(f * (-LOG2E))
        aug = _dot(nh, place[0]) + _dot(nm, place[1]) + _dot(nl, place[2])
        out_ref[ci * FB_CHUNK:(ci + 1) * FB_CHUNK, :] = aug.astype(BF16)


def _fox_bias(ff, b_row, batch, seq):
    return pl.pallas_call(
        _fox_bias_kernel,
        grid=(batch,),
        in_specs=[
            pl.BlockSpec((seq, LANES), lambda b: (b, 0)),
            pl.BlockSpec((1, LANES), lambda b: (0, 0)),
        ],
        out_specs=pl.BlockSpec((seq, LANES), lambda b: (b, 0)),
        out_shape=jax.ShapeDtypeStruct((batch * seq, LANES), BF16),
        compiler_params=pltpu.CompilerParams(
            dimension_semantics=("arbitrary",), vmem_limit_bytes=VMEM_LIMIT),
        name="fox_bias",
    )(ff, b_row)


ATT_T = 512
ATT_HG = 4
ATT_SUB = ATT_T // MOBA_BLOCK


def _attn_kernel(is_fox, n_cast, *refs):
    q_ref, k_ref, v_ref, x_ref = refs[:4]
    cast_src = refs[4:4 + n_cast]
    o_ref = refs[4 + n_cast]
    cast_dst = refs[5 + n_cast:5 + 2 * n_cast]
    vt_ref, acc_ref, sa_ref, sb_ref, *maybe_sel = refs[5 + 2 * n_cast:]
    hg = pl.program_id(1)
    qt = pl.program_id(2)
    seq = k_ref.shape[0]
    nblk = seq // MOBA_BLOCK
    t = ATT_T

    for src, dst in zip(cast_src, cast_dst):
        dst[...] = src[...].astype(BF16)

    def head_cols(hh):
        return slice(hh * HEAD_DIM, (hh + 1) * HEAD_DIM)

    @pl.when(qt == 0)
    def _():
        for hh in range(ATT_HG):
            for ci in range(nblk):
                rows = slice(ci * MOBA_BLOCK, (ci + 1) * MOBA_BLOCK)
                blk = v_ref[rows, head_cols(hh)].astype(F32)
                vt_ref[head_cols(hh), rows] = blk.T.astype(BF16)

    qs = []
    for hh in range(ATT_HG):
        qh = q_ref[:, head_cols(hh)]
        if is_fox:
            head = hg * ATT_HG + hh
            lane = lax.broadcasted_iota(jnp.int32, (t, LANES), 1)
            pick = ((lane >= 3 * head) & (lane < 3 * head + 3)).astype(BF16)
            qh = jnp.concatenate([qh, pick], axis=1)
        qs.append(qh)

    if not is_fox:
        sel_ref, = maybe_sel
        blk_id = lax.broadcasted_iota(jnp.int32, (nblk, t), 0)
        q_blk = qt * ATT_SUB + lax.broadcasted_iota(jnp.int32, (nblk, t), 1) // MOBA_BLOCK
        past = blk_id < q_blk
        for hh in range(ATT_HG):
            kh, km, kl = _split3(x_ref[:, head_cols(hh)])
            gate = _dot_nt(kh, qs[hh]) + _dot_nt(km, qs[hh]) + _dot_nt(kl, qs[hh])
            gate = jnp.where(past, gate, -jnp.inf)
            rank = jnp.zeros((nblk, t), jnp.int32)
            for mth in range(nblk):
                gm = gate[mth:mth + 1, :]
                ahead = (gm > gate) | ((gm == gate) & (mth < blk_id))
                rank = rank + ahead.astype(jnp.int32)
            sel_ref[hh] = ((rank < MOBA_TOP_K) & past).astype(F32)

    def scores(hh, start):
        kb = k_ref[pl.ds(start, t), head_cols(hh)]
        if is_fox:
            kb = jnp.concatenate([kb, x_ref[pl.ds(start, t), :]], axis=1)
        return _dot_nt(kb, qs[hh])

    def mask_rows(hh, s, first_blk, diagonal):
        parts = []
        for r in range(ATT_SUB):
            keep = sel_ref[hh, pl.ds(first_blk + r, 1), :] > 0.5
            if diagonal:
                key = r * MOBA_BLOCK + lax.broadcasted_iota(jnp.int32, (MOBA_BLOCK, t), 0)
                qry = lax.broadcasted_iota(jnp.int32, (MOBA_BLOCK, t), 1)
                keep = keep | ((key <= qry) & (qry < (r + 1) * MOBA_BLOCK))
            parts.append(jnp.where(keep, s[r * MOBA_BLOCK:(r + 1) * MOBA_BLOCK, :], NEG_BIG))
        return jnp.concatenate(parts, axis=0)

    def vt_chunk(hh, start):
        return vt_ref[head_cols(hh), pl.ds(start, t)]

    def produce(dst_ref, start, first_blk, diagonal):
        cmax = []
        for hh in range(ATT_HG):
            s = scores(hh, start)
            if is_fox:
                if diagonal:
                    key = lax.broadcasted_iota(jnp.int32, (t, t), 0)
                    qry = lax.broadcasted_iota(jnp.int32, (t, t), 1)
                    s = jnp.where(key <= qry, s, NEG_BIG)
            else:
                s = mask_rows(hh, s, first_blk, diagonal)
            dst_ref[hh] = s
            cmax.append(jnp.max(s, axis=0, keepdims=True))
        return tuple(cmax)

    def consume(src_ref, start, cmax, stats):
        new = []
        for hh in range(ATT_HG):
            m, l = stats[hh]
            m_new = jnp.maximum(m, cmax[hh])
            alpha = jnp.exp2(m - m_new)
            p = jnp.exp2(src_ref[hh] - m_new)
            l = alpha * l + jnp.sum(p, axis=0, keepdims=True)
            acc_ref[head_cols(hh), :] = (alpha * acc_ref[head_cols(hh), :]
                                         + _dot(vt_chunk(hh, start), p.astype(BF16)))
            new.append((m_new, l))
        return tuple(new)

    own = pl.multiple_of(qt * t, t)
    acc_ref[...] = jnp.zeros_like(acc_ref)
    stats = tuple((jnp.full((1, t), NEG_BIG, F32), jnp.zeros((1, t), F32)) for _ in range(ATT_HG))
    cmax = produce(sa_ref, own, qt * ATT_SUB, True)

    def step(dst_ref, src_ref):
        def run(i, cmax, stats):
            nxt = pl.multiple_of(i * t, t)
            cur = pl.multiple_of(jnp.where(i == 0, qt, i - 1) * t, t)
            cmax_next = produce(dst_ref, nxt, i * ATT_SUB, False)
            return cmax_next, consume(src_ref, cur, cmax, stats)
        return run

    def body(i, carry):
        return lax.cond(i % 2 == 0, step(sb_ref, sa_ref), step(sa_ref, sb_ref), i, *carry)

    cmax, stats = lax.fori_loop(0, qt, body, (cmax, stats))
    last = pl.multiple_of(jnp.maximum(qt - 1, 0) * t, t)
    stats = lax.cond(qt % 2 == 0,
                     lambda c, s: consume(sa_ref, last, c, s),
                     lambda c, s: consume(sb_ref, last, c, s), cmax, stats)
    for hh in range(ATT_HG):
        _, l = stats[hh]
        o_ref[:, head_cols(hh)] = (acc_ref[head_cols(hh), :] * (1.0 / l)).T.astype(BF16)


def _attention(proj, kmean, faug, moba_riders, fox_riders, batch, seq):
    n = batch * seq
    nq = seq // ATT_T
    nblk = seq // MOBA_BLOCK
    hw = ATT_HG * HEAD_DIM
    n_hg = N_HEADS // ATT_HG
    grid = (batch, n_hg, nq)
    n_steps = batch * n_hg * nq
    params = pltpu.CompilerParams(
        dimension_semantics=("arbitrary", "arbitrary", "arbitrary"),
        vmem_limit_bytes=VMEM_LIMIT)

    def q_spec(col0):
        return pl.BlockSpec((ATT_T, hw), lambda b, h, i: (b * nq + i, col0 + h))

    def kv_spec(col0):
        return pl.BlockSpec((seq, hw), lambda b, h, i: (b, col0 + h))

    def rider_spec(w):
        rows = w.shape[0] // n_steps
        assert rows * n_steps == w.shape[0] and rows % 16 == 0
        return pl.BlockSpec((rows, w.shape[1]), lambda b, h, i: ((b * n_hg + h) * nq + i, 0))

    region = WIDTH // hw
    out_spec = pl.BlockSpec((ATT_T, hw), lambda b, h, i: (b * nq + i, h))
    out_shape = jax.ShapeDtypeStruct((n, WIDTH), BF16)
    common = [pltpu.VMEM((hw, seq), BF16), pltpu.VMEM((hw, ATT_T), F32),
              pltpu.VMEM((ATT_HG, ATT_T, ATT_T), F32), pltpu.VMEM((ATT_HG, ATT_T, ATT_T), F32)]

    def call(is_fox, name, specs, args, riders, extra_scratch):
        rider_specs = [rider_spec(w) for w in riders]
        return pl.pallas_call(
            functools.partial(_attn_kernel, is_fox, len(riders)),
            grid=grid,
            in_specs=specs + rider_specs,
            out_specs=[out_spec] + rider_specs,
            out_shape=[out_shape] + [jax.ShapeDtypeStruct(w.shape, BF16) for w in riders],
            scratch_shapes=common + extra_scratch,
            compiler_params=params,
            name=name,
        )(*args, *riders)

    o_moba, *moba_cast = call(
        False, "moba_attn",
        [q_spec(0), kv_spec(region), kv_spec(2 * region),
         pl.BlockSpec((nblk, hw), lambda b, h, i: (b, h))],
        (proj, proj, proj, kmean), moba_riders, [pltpu.VMEM((ATT_HG, nblk, ATT_T), F32)])
    o_fox, *fox_cast = call(
        True, "fox_attn",
        [q_spec(3 * region), kv_spec(4 * region), kv_spec(5 * region),
         pl.BlockSpec((seq, LANES), lambda b, h, i: (b, 0))],
        (proj, proj, proj, faug), fox_riders, [])
    return o_moba, o_fox, moba_cast, fox_cast


MIX_TM = 256


def _sigmoid(z):
    return 1.0 / (1.0 + jnp.exp(-z))


def _mix_kernel(om_ref, of_ref, ga_ref, gb_ref, x_ref, wm_ref, wf_ref, wo_ref, g_ref, out_ref):
    y_m = _dot(om_ref[...], wm_ref[...])
    y_f = _dot(of_ref[...], wf_ref[...])
    merged = _sigmoid(ga_ref[...].astype(F32)) * y_m + _sigmoid(gb_ref[...].astype(F32)) * y_f
    mixed = _dot(merged.astype(BF16), wo_ref[...])
    out_ref[...] = x_ref[...] + mixed * _rms_scale(mixed) * g_ref[...]


def _mix_out(o_moba, o_fox, proj, x2, wm, wf, wo, g):
    n = x2.shape[0]
    ga_blk = QKV_COLS // D_MODEL
    const = dict(pipeline_mode=pl.Buffered(1))
    return pl.pallas_call(
        _mix_kernel,
        grid=(n // MIX_TM,),
        in_specs=[
            pl.BlockSpec((MIX_TM, WIDTH), lambda i: (i, 0)),
            pl.BlockSpec((MIX_TM, WIDTH), lambda i: (i, 0)),
            pl.BlockSpec((MIX_TM, D_MODEL), lambda i: (i, ga_blk)),
            pl.BlockSpec((MIX_TM, D_MODEL), lambda i: (i, ga_blk + 1)),
            pl.BlockSpec((MIX_TM, D_MODEL), lambda i: (i, 0)),
            pl.BlockSpec((WIDTH, D_MODEL), lambda i: (0, 0), **const),
            pl.BlockSpec((WIDTH, D_MODEL), lambda i: (0, 0), **const),
            pl.BlockSpec((D_MODEL, D_MODEL), lambda i: (0, 0), **const),
            pl.BlockSpec((1, D_MODEL), lambda i: (0, 0)),
        ],
        out_specs=pl.BlockSpec((MIX_TM, D_MODEL), lambda i: (i, 0)),
        out_shape=jax.ShapeDtypeStruct((n, D_MODEL), F32),
        compiler_params=pltpu.CompilerParams(
            dimension_semantics=("arbitrary",), vmem_limit_bytes=VMEM_LIMIT),
        name="mix_out",
    )(o_moba, o_fox, proj, proj, x2, wm, wf, wo, g)


MLP_TM = 512
MLP_TF = 1024


def _mlp_kernel(x_ref, gpre_ref, wu_ref, wd_ref, gpost_ref, out_ref, h_ref, acc_ref):
    f = pl.program_id(1)

    @pl.when(f == 0)
    def _():
        x = x_ref[...]
        h_ref[...] = (x * _rms_scale(x) * gpre_ref[...]).astype(BF16)
        acc_ref[...] = jnp.zeros_like(acc_ref)

    u = _dot(h_ref[...], wu_ref[...])
    a = jnp.square(jnp.maximum(u, 0.0)).astype(BF16)
    acc_ref[...] += _dot(a, wd_ref[...])

    @pl.when(f == pl.num_programs(1) - 1)
    def _():
        mo = acc_ref[...]
        out_ref[...] = x_ref[...] + mo * _rms_scale(mo) * gpost_ref[...]


def _mlp(x1, gpre, wu, wd, gpost):
    n = x1.shape[0]
    return pl.pallas_call(
        _mlp_kernel,
        grid=(n // MLP_TM, D_FF // MLP_TF),
        in_specs=[
            pl.BlockSpec((MLP_TM, D_MODEL), lambda i, f: (i, 0)),
            pl.BlockSpec((1, D_MODEL), lambda i, f: (0, 0)),
            pl.BlockSpec((D_MODEL, MLP_TF), lambda i, f: (0, f)),
            pl.BlockSpec((MLP_TF, D_MODEL), lambda i, f: (f, 0)),
            pl.BlockSpec((1, D_MODEL), lambda i, f: (0, 0)),
        ],
        out_specs=pl.BlockSpec((MLP_TM, D_MODEL), lambda i, f: (i, 0)),
        out_shape=jax.ShapeDtypeStruct((n, D_MODEL), F32),
        scratch_shapes=[pltpu.VMEM((MLP_TM, D_MODEL), BF16), pltpu.VMEM((MLP_TM, D_MODEL), F32)],
        compiler_params=pltpu.CompilerParams(
            dimension_semantics=("arbitrary", "arbitrary"), vmem_limit_bytes=VMEM_LIMIT),
        name="mlp",
    )(x1, gpre, wu, wd, gpost)


def _rope_tables(seq):
    inv_freq = ROPE_THETA ** (-jnp.arange(0, ROPE_DIM, 2, dtype=F32) / ROPE_DIM)
    ang = jnp.arange(seq, dtype=F32)[:, None] * inv_freq[None, :]
    cos, sin = jnp.cos(ang), jnp.sin(ang)
    gap = HEAD_DIM // 2 - ROPE_HALF
    ones = jnp.ones((seq, gap), F32)
    zeros = jnp.zeros((seq, gap), F32)
    cos_t = jnp.concatenate([cos, ones, cos, ones], axis=1)
    sin_t = jnp.concatenate([-sin, zeros, sin, zeros], axis=1)
    return cos_t, sin_t


def _layer(x2, w_in, b_forget, w_bm, w_bf, w_out, g_mix_pre, g_mix_post,
           w_up, w_down, g_mlp_pre, g_mlp_post, batch, seq):
    w, wff = _w_prep(w_in)
    b_row = jnp.pad(b_forget, (0, LANES - N_HEADS)).reshape(1, LANES)
    cos_t, sin_t = _rope_tables(seq)

    proj, ff, kmean = _in_proj(x2, g_mix_pre.reshape(1, -1), w, wff, cos_t, sin_t, seq)
    faug = _fox_bias(ff, b_row, batch, seq)
    kmean = kmean.reshape(batch * (seq // MOBA_BLOCK), WIDTH)
    o_moba, o_fox, (wm, wf, wo), (wu, wd) = _attention(
        proj, kmean, faug, (w_bm, w_bf, w_out), (w_up, w_down), batch, seq)
    x1 = _mix_out(o_moba, o_fox, proj, x2, wm, wf, wo, g_mix_post.reshape(1, -1))
    return _mlp(x1, g_mlp_pre.reshape(1, -1), wu, wd, g_mlp_post.reshape(1, -1))


def kernel(x, w_in, b_forget, w_branch_moba, w_branch_fox, w_out, g_mix_pre, g_mix_post,
           w_up, w_down, g_mlp_pre, g_mlp_post):
    batch, seq, d = x.shape
    assert d == D_MODEL and seq % IN_TM == 0 and seq % ATT_T == 0
    assert w_in.shape[-1] == PROJ_COLS + N_HEADS
    x2 = x.reshape(batch * seq, d)
    for l in range(w_in.shape[0]):
        x2 = _layer(x2, w_in[l], b_forget[l], w_branch_moba[l], w_branch_fox[l], w_out[l],
                    g_mix_pre[l], g_mix_post[l], w_up[l], w_down[l], g_mlp_pre[l],
                    g_mlp_post[l], batch, seq)
    return x2.reshape(batch, seq, d)
```

```python
import functools
import math

import jax
import jax.numpy as jnp
from jax import lax
from jax.experimental import pallas as pl
from jax.experimental.pallas import tpu as pltpu

D_MODEL = 2048
HEAD_DIM = 128
N_HEADS = 8
WIDTH = N_HEADS * HEAD_DIM
MOBA_BLOCK = 256
MOBA_TOP_K = 3
ROPE_THETA = 500000.0
ROPE_DIM = HEAD_DIM // 4
ROPE_HALF = ROPE_DIM // 2
D_FF = 4 * D_MODEL
RMS_EPS = 1e-6
NEG_BIG = -1e30
LANES = 128
SUBLANES = 8

QKV_COLS = 6 * WIDTH
GATE_COLS = 2 * D_MODEL
PROJ_COLS = QKV_COLS + GATE_COLS

VMEM_LIMIT = 56 * 1024 * 1024

BF16 = jnp.bfloat16
F32 = jnp.float32
LOG2E = math.log2(math.e)
Q_SCALE = HEAD_DIM ** -0.5 * LOG2E


def _rms_scale(x):
    return lax.rsqrt(jnp.mean(x * x, axis=-1, keepdims=True) + RMS_EPS)


def _split3(x):
    hi = x.astype(BF16)
    r1 = x - hi.astype(F32)
    mid = r1.astype(BF16)
    lo = (r1 - mid.astype(F32)).astype(BF16)
    return hi, mid, lo


def _dot(a, b):
    return jnp.dot(a, b, preferred_element_type=F32)


def _dot_nt(a, b):
    return lax.dot_general(a, b, (((1,), (1,)), ((), ())), preferred_element_type=F32)


IN_TM = 1024
IN_TN = 1024
IN_ROT_TILES = 2 * WIDTH // IN_TN


def _permute_rotary_rows(blk):
    mid = HEAD_DIM // 2 + ROPE_HALF
    parts = []
    for hd in range(blk.shape[0] // HEAD_DIM):
        b = hd * HEAD_DIM
        parts += [blk[b:b + ROPE_HALF], blk[b + ROPE_DIM:b + mid],
                  blk[b + ROPE_HALF:b + ROPE_DIM], blk[b + mid:b + HEAD_DIM]]
    return jnp.concatenate(parts, axis=0)


def _in_proj_kernel(is_head, *refs):
    if is_head:
        (x_ref, g_ref, w_ref, wff_ref, cos_ref, sin_ref,
         proj_ref, ff_ref, kmean_ref, wb_ref, wffb_ref, h_ref) = refs
        j = pl.program_id(0)
        when = pl.when
    else:
        (x_ref, g_ref, w_ref, wff_ref, cos_ref, sin_ref, proj0_ref, ff0_ref, kmean0_ref,
         proj_ref, ff_ref, kmean_ref, h_ref) = refs
        i = pl.program_id(0)
        j = pl.program_id(1)

        @pl.when(i == 0)
        def _():
            proj_ref[...] = proj0_ref[...]

        @pl.when((i == 0) & (j == 0))
        def _():
            ff_ref[...] = ff0_ref[...]
            kmean_ref[...] = kmean0_ref[...]

        def when(cond):
            return pl.when((i > 0) & cond)

    @when(j == 0)
    def _():
        x = x_ref[...]
        h = (x * _rms_scale(x) * g_ref[...]).astype(BF16)
        h_ref[...] = h
        wff = wff_ref[...].astype(BF16)
        if is_head:
            wffb_ref[...] = wff
        ff_ref[...] = _dot_nt(h, wff)

    def tile(permute):
        w = w_ref[...]
        if is_head:
            w = (_permute_rotary_rows(w) if permute else w).astype(BF16)
            wb_ref[...] = w
        return _dot_nt(h_ref[...], w)

    def rotary(t):
        outs = []
        for hd in range(IN_TN // HEAD_DIM):
            th = t[:, hd * HEAD_DIM:(hd + 1) * HEAD_DIM]
            outs.append(th * cos_ref[...] + pltpu.roll(th, HEAD_DIM // 2, 1) * sin_ref[...])
        return jnp.concatenate(outs, axis=1)

    @when(j == 0)
    def _():
        proj_ref[...] = (rotary(tile(True)) * Q_SCALE).astype(BF16)

    @when(j == 1)
    def _():
        kr = rotary(tile(True))
        proj_ref[...] = kr.astype(BF16)
        nblk = IN_TM // MOBA_BLOCK
        means = [jnp.mean(kr[b * MOBA_BLOCK:(b + 1) * MOBA_BLOCK, :], axis=0, keepdims=True)
                 for b in range(nblk)]
        kmean_ref[0] = jnp.concatenate(means, axis=0)

    @when(j == 3)
    def _():
        proj_ref[...] = (tile(False) * Q_SCALE).astype(BF16)

    @when((j == 2) | (j >= 4))
    def _():
        proj_ref[...] = tile(False).astype(BF16)


def _in_proj(x2, g, w_in, cos_t, sin_t, seq):
    assert IN_ROT_TILES == 2
    n = x2.shape[0]
    seq_tiles = seq // IN_TM
    n_rows = n // IN_TM
    n_cols = PROJ_COLS // IN_TN
    n_qkv = QKV_COLS // IN_TN
    gate0 = QKV_COLS + N_HEADS
    assert gate0 % SUBLANES == 0
    wt = w_in.T

    def src_row(j):
        return pl.multiple_of(jnp.where(j < n_qkv, j * IN_TN, gate0 + (j - n_qkv) * IN_TN), SUBLANES)

    def out_shape(rows):
        return [jax.ShapeDtypeStruct((rows, PROJ_COLS), BF16),
                jax.ShapeDtypeStruct((rows, LANES), F32),
                jax.ShapeDtypeStruct((rows // IN_TM, IN_TM // MOBA_BLOCK, WIDTH), F32)]

    scratch = [pltpu.VMEM((IN_TM, D_MODEL), BF16)]
    once = dict(pipeline_mode=pl.Buffered(1))

    proj0, ff0, kmean0, wb, wffb = pl.pallas_call(
        functools.partial(_in_proj_kernel, True),
        grid=(n_cols,),
        in_specs=[
            pl.BlockSpec((IN_TM, D_MODEL), lambda j: (0, 0), **once),
            pl.BlockSpec((1, D_MODEL), lambda j: (0, 0)),
            pl.BlockSpec((pl.Element(IN_TN), pl.Element(D_MODEL)), lambda j: (src_row(j), 0)),
            pl.BlockSpec((LANES, D_MODEL), lambda j: (QKV_COLS // LANES, 0), **once),
            pl.BlockSpec((IN_TM, HEAD_DIM), lambda j: (0, 0), **once),
            pl.BlockSpec((IN_TM, HEAD_DIM), lambda j: (0, 0), **once),
        ],
        out_specs=[
            pl.BlockSpec((IN_TM, IN_TN), lambda j: (0, j)),
            pl.BlockSpec((IN_TM, LANES), lambda j: (0, 0)),
            pl.BlockSpec((1, IN_TM // MOBA_BLOCK, WIDTH), lambda j: (0, 0, 0)),
            pl.BlockSpec((IN_TN, D_MODEL), lambda j: (j, 0)),
            pl.BlockSpec((LANES, D_MODEL), lambda j: (0, 0)),
        ],
        out_shape=out_shape(IN_TM) + [jax.ShapeDtypeStruct((PROJ_COLS, D_MODEL), BF16),
                                      jax.ShapeDtypeStruct((LANES, D_MODEL), BF16)],
        scratch_shapes=scratch,
        compiler_params=pltpu.CompilerParams(
            dimension_semantics=("arbitrary",), vmem_limit_bytes=VMEM_LIMIT),
        name="in_proj_head",
    )(x2, g, wt, wt, cos_t, sin_t)

    def row(i):
        return jnp.maximum(i, 1)

    return pl.pallas_call(
        functools.partial(_in_proj_kernel, False),
        grid=(n_rows, n_cols),
        in_specs=[
            pl.BlockSpec((IN_TM, D_MODEL), lambda i, j: (row(i), 0)),
            pl.BlockSpec((1, D_MODEL), lambda i, j: (0, 0)),
            pl.BlockSpec((IN_TN, D_MODEL), lambda i, j: (j, 0)),
            pl.BlockSpec((LANES, D_MODEL), lambda i, j: (0, 0)),
            pl.BlockSpec((IN_TM, HEAD_DIM), lambda i, j: (row(i) % seq_tiles, 0)),
            pl.BlockSpec((IN_TM, HEAD_DIM), lambda i, j: (row(i) % seq_tiles, 0)),
            pl.BlockSpec((IN_TM, IN_TN), lambda i, j: (0, jnp.where(i == 0, j, n_cols - 1))),
            pl.BlockSpec((IN_TM, LANES), lambda i, j: (0, 0)),
            pl.BlockSpec((1, IN_TM // MOBA_BLOCK, WIDTH), lambda i, j: (0, 0, 0)),
        ],
        out_specs=[
            pl.BlockSpec((IN_TM, IN_TN), lambda i, j: (i, j)),
            pl.BlockSpec((IN_TM, LANES), lambda i, j: (i, 0)),
            pl.BlockSpec((1, IN_TM // MOBA_BLOCK, WIDTH), lambda i, j: (i, 0, 0)),
        ],
        out_shape=out_shape(n),
        scratch_shapes=scratch,
        compiler_params=pltpu.CompilerParams(
            dimension_semantics=("arbitrary", "arbitrary"), vmem_limit_bytes=VMEM_LIMIT),
        name="in_proj",
    )(x2, g, wb, wffb, cos_t, sin_t, proj0, ff0, kmean0)


FB_CHUNK = 256


def _fox_bias_kernel(ff_ref, b_ref, out_ref):
    seq = ff_ref.shape[0]
    r = lax.broadcasted_iota(jnp.int32, (FB_CHUNK, FB_CHUNK), 0)
    c = lax.broadcasted_iota(jnp.int32, (FB_CHUNK, FB_CHUNK), 1)
    ltri = (c <= r).astype(BF16)
    pr = lax.broadcasted_iota(jnp.int32, (LANES, LANES), 0)
    pc = lax.broadcasted_iota(jnp.int32, (LANES, LANES), 1)
    place = [((pc == 3 * pr + t) & (pr < N_HEADS)).astype(BF16) for t in range(3)]

    local = []
    for ci in range(seq // FB_CHUNK):
        z = ff_ref[ci * FB_CHUNK:(ci + 1) * FB_CHUNK, :] + b_ref[...]
        logf = -(jnp.maximum(-z, 0.0) + jnp.log1p(jnp.exp(-jnp.abs(z))))
        hi, mid, lo = _split3(logf)
        local.append(_dot(ltri, hi) + _dot(ltri, mid) + _dot(ltri, lo))
    carry = jnp.zeros((1, LANES), F32)
    for ci, loc in enumerate(local):
        f = loc + carry
        carry = carry + loc[FB_CHUNK - 1:FB_CHUNK, :]
        nh, nm, nl = _split3(f * (-LOG2E))
        aug = _dot(nh, place[0]) + _dot(nm, place[1]) + _dot(nl, place[2])
        out_ref[ci * FB_CHUNK:(ci + 1) * FB_CHUNK, :] = aug.astype(BF16)


def _fox_bias(ff, b_row, batch, seq):
    return pl.pallas_call(
        _fox_bias_kernel,
        grid=(batch,),
        in_specs=[
            pl.BlockSpec((seq, LANES), lambda b: (b, 0)),
            pl.BlockSpec((1, LANES), lambda b: (0, 0)),
        ],
        out_specs=pl.BlockSpec((seq, LANES), lambda b: (b, 0)),
        out_shape=jax.ShapeDtypeStruct((batch * seq, LANES), BF16),
        compiler_params=pltpu.CompilerParams(
            dimension_semantics=("arbitrary",), vmem_limit_bytes=VMEM_LIMIT),
        name="fox_bias",
    )(ff, b_row)


ATT_T = 512
ATT_HG = 4
ATT_SUB = ATT_T // MOBA_BLOCK


def _attn_kernel(is_fox, n_cast, *refs):
    q_ref, k_ref, v_ref, x_ref = refs[:4]
    cast_src = refs[4:4 + n_cast]
    o_ref = refs[4 + n_cast]
    cast_dst = refs[5 + n_cast:5 + 2 * n_cast]
    vt_ref, acc_ref, sa_ref, sb_ref, *maybe_sel = refs[5 + 2 * n_cast:]
    hg = pl.program_id(1)
    qt = pl.program_id(2)
    seq = k_ref.shape[0]
    nblk = seq // MOBA_BLOCK
    t = ATT_T

    for src, dst in zip(cast_src, cast_dst):
        dst[...] = src[...].astype(BF16)

    def head_cols(hh):
        return slice(hh * HEAD_DIM, (hh + 1) * HEAD_DIM)

    @pl.when(qt == 0)
    def _():
        for hh in range(ATT_HG):
            for ci in range(nblk):
                rows = slice(ci * MOBA_BLOCK, (ci + 1) * MOBA_BLOCK)
                blk = v_ref[rows, head_cols(hh)].astype(F32)
                vt_ref[head_cols(hh), rows] = blk.T.astype(BF16)

    qs = []
    for hh in range(ATT_HG):
        qh = q_ref[:, head_cols(hh)]
        if is_fox:
            head = hg * ATT_HG + hh
            lane = lax.broadcasted_iota(jnp.int32, (t, LANES), 1)
            pick = ((lane >= 3 * head) & (lane < 3 * head + 3)).astype(BF16)
            qh = jnp.concatenate([qh, pick], axis=1)
        qs.append(qh)

    if not is_fox:
        sel_ref, = maybe_sel
        blk_id = lax.broadcasted_iota(jnp.int32, (nblk, t), 0)
        q_blk = qt * ATT_SUB + lax.broadcasted_iota(jnp.int32, (nblk, t), 1) // MOBA_BLOCK
        past = blk_id < q_blk
        for hh in range(ATT_HG):
            kh, km, kl = _split3(x_ref[:, head_cols(hh)])
            gate = _dot_nt(kh, qs[hh]) + _dot_nt(km, qs[hh]) + _dot_nt(kl, qs[hh])
            gate = jnp.where(past, gate, -jnp.inf)
            rank = jnp.zeros((nblk, t), jnp.int32)
            for mth in range(nblk):
                gm = gate[mth:mth + 1, :]
                ahead = (gm > gate) | ((gm == gate) & (mth < blk_id))
                rank = rank + ahead.astype(jnp.int32)
            sel_ref[hh] = ((rank < MOBA_TOP_K) & past).astype(F32)

    def scores(hh, start):
        kb = k_ref[pl.ds(start, t), head_cols(hh)]
        if is_fox:
            kb = jnp.concatenate([kb, x_ref[pl.ds(start, t), :]], axis=1)
        return _dot_nt(kb, qs[hh])

    def mask_rows(hh, s, first_blk, diagonal):
        parts = []
        for r in range(ATT_SUB):
            keep = sel_ref[hh, pl.ds(first_blk + r, 1), :] > 0.5
            if diagonal:
                key = r * MOBA_BLOCK + lax.broadcasted_iota(jnp.int32, (MOBA_BLOCK, t), 0)
                qry = lax.broadcasted_iota(jnp.int32, (MOBA_BLOCK, t), 1)
                keep = keep | ((key <= qry) & (qry < (r + 1) * MOBA_BLOCK))
            parts.append(jnp.where(keep, s[r * MOBA_BLOCK:(r + 1) * MOBA_BLOCK, :], NEG_BIG))
        return jnp.concatenate(parts, axis=0)

    def vt_chunk(hh, start):
        return vt_ref[head_cols(hh), pl.ds(start, t)]

    def produce(dst_ref, start, first_blk, diagonal):
        cmax = []
        for hh in range(ATT_HG):
            s = scores(hh, start)
            if is_fox:
                if diagonal:
                    key = lax.broadcasted_iota(jnp.int32, (t, t), 0)
                    qry = lax.broadcasted_iota(jnp.int32, (t, t), 1)
                    s = jnp.where(key <= qry, s, NEG_BIG)
            else:
                s = mask_rows(hh, s, first_blk, diagonal)
            dst_ref[hh] = s
            cmax.append(jnp.max(s, axis=0, keepdims=True))
        return tuple(cmax)

    def consume(src_ref, start, cmax, stats):
        new = []
        for hh in range(ATT_HG):
            m, l = stats[hh]
            m_new = jnp.maximum(m, cmax[hh])
            alpha = jnp.exp2(m - m_new)
            p = jnp.exp2(src_ref[hh] - m_new)
            l = alpha * l + jnp.sum(p, axis=0, keepdims=True)
            acc_ref[head_cols(hh), :] = (alpha * acc_ref[head_cols(hh), :]
                                         + _dot(vt_chunk(hh, start), p.astype(BF16)))
            new.append((m_new, l))
        return tuple(new)

    own = pl.multiple_of(qt * t, t)
    acc_ref[...] = jnp.zeros_like(acc_ref)
    stats = tuple((jnp.full((1, t), NEG_BIG, F32), jnp.zeros((1, t), F32)) for _ in range(ATT_HG))
    cmax = produce(sa_ref, own, qt * ATT_SUB, True)

    def step(dst_ref, src_ref):
        def run(i, cmax, stats):
            nxt = pl.multiple_of(i * t, t)
            cur = pl.multiple_of(jnp.where(i == 0, qt, i - 1) * t, t)
            cmax_next = produce(dst_ref, nxt, i * ATT_SUB, False)
            return cmax_next, consume(src_ref, cur, cmax, stats)
        return run

    def body(i, carry):
        return lax.cond(i % 2 == 0, step(sb_ref, sa_ref), step(sa_ref, sb_ref), i, *carry)

    cmax, stats = lax.fori_loop(0, qt, body, (cmax, stats))
    last = pl.multiple_of(jnp.maximum(qt - 1, 0) * t, t)
    stats = lax.cond(qt % 2 == 0,
                     lambda c, s: consume(sa_ref, last, c, s),
                     lambda c, s: consume(sb_ref, last, c, s), cmax, stats)
    for hh in range(ATT_HG):
        _, l = stats[hh]
        o_ref[:, head_cols(hh)] = (acc_ref[head_cols(hh), :] * (1.0 / l)).T.astype(BF16)


def _attention(proj, kmean, faug, moba_riders, fox_riders, batch, seq):
    n = batch * seq
    nq = seq // ATT_T
    nblk = seq // MOBA_BLOCK
    hw = ATT_HG * HEAD_DIM
    n_hg = N_HEADS // ATT_HG
    grid = (batch, n_hg, nq)
    n_steps = batch * n_hg * nq
    params = pltpu.CompilerParams(
        dimension_semantics=("arbitrary", "arbitrary", "arbitrary"),
        vmem_limit_bytes=VMEM_LIMIT)

    def q_spec(col0):
        return pl.BlockSpec((ATT_T, hw), lambda b, h, i: (b * nq + i, col0 + h))

    def kv_spec(col0):
        return pl.BlockSpec((seq, hw), lambda b, h, i: (b, col0 + h))

    def rider_spec(w):
        rows = w.shape[0] // n_steps
        assert rows * n_steps == w.shape[0] and rows % 16 == 0
        return pl.BlockSpec((rows, w.shape[1]), lambda b, h, i: ((b * n_hg + h) * nq + i, 0))

    region = WIDTH // hw
    out_spec = pl.BlockSpec((ATT_T, hw), lambda b, h, i: (b * nq + i, h))
    out_shape = jax.ShapeDtypeStruct((n, WIDTH), BF16)
    common = [pltpu.VMEM((hw, seq), BF16), pltpu.VMEM((hw, ATT_T), F32),
              pltpu.VMEM((ATT_HG, ATT_T, ATT_T), F32), pltpu.VMEM((ATT_HG, ATT_T, ATT_T), F32)]

    def call(is_fox, name, specs, args, riders, extra_scratch):
        rider_specs = [rider_spec(w) for w in riders]
        return pl.pallas_call(
            functools.partial(_attn_kernel, is_fox, len(riders)),
            grid=grid,
            in_specs=specs + rider_specs,
            out_specs=[out_spec] + rider_specs,
            out_shape=[out_shape] + [jax.ShapeDtypeStruct(w.shape, BF16) for w in riders],
            scratch_shapes=common + extra_scratch,
            compiler_params=params,
            name=name,
        )(*args, *riders)

    o_moba, *moba_cast = call(
        False, "moba_attn",
        [q_spec(0), kv_spec(region), kv_spec(2 * region),
         pl.BlockSpec((nblk, hw), lambda b, h, i: (b, h))],
        (proj, proj, proj, kmean), moba_riders, [pltpu.VMEM((ATT_HG, nblk, ATT_T), F32)])
    o_fox, *fox_cast = call(
        True, "fox_attn",
        [q_spec(3 * region), kv_spec(4 * region), kv_spec(5 * region),
         pl.BlockSpec((seq, LANES), lambda b, h, i: (b, 0))],
        (proj, proj, proj, faug), fox_riders, [])
    return o_moba, o_fox, moba_cast, fox_cast


MIX_TM = 256


def _sigmoid(z):
    return 1.0 / (1.0 + jnp.exp(-z))


def _mix_kernel(om_ref, of_ref, ga_ref, gb_ref, x_ref, wm_ref, wf_ref, wo_ref, g_ref, out_ref):
    y_m = _dot(om_ref[...], wm_ref[...])
    y_f = _dot(of_ref[...], wf_ref[...])
    merged = _sigmoid(ga_ref[...].astype(F32)) * y_m + _sigmoid(gb_ref[...].astype(F32)) * y_f
    mixed = _dot(merged.astype(BF16), wo_ref[...])
    out_ref[...] = x_ref[...] + mixed * _rms_scale(mixed) * g_ref[...]


def _mix_out(o_moba, o_fox, proj, x2, wm, wf, wo, g):
    n = x2.shape[0]
    ga_blk = QKV_COLS // D_MODEL
    const = dict(pipeline_mode=pl.Buffered(1))
    return pl.pallas_call(
        _mix_kernel,
        grid=(n // MIX_TM,),
        in_specs=[
            pl.BlockSpec((MIX_TM, WIDTH), lambda i: (i, 0)),
            pl.BlockSpec((MIX_TM, WIDTH), lambda i: (i, 0)),
            pl.BlockSpec((MIX_TM, D_MODEL), lambda i: (i, ga_blk)),
            pl.BlockSpec((MIX_TM, D_MODEL), lambda i: (i, ga_blk + 1)),
            pl.BlockSpec((MIX_TM, D_MODEL), lambda i: (i, 0)),
            pl.BlockSpec((WIDTH, D_MODEL), lambda i: (0, 0), **const),
            pl.BlockSpec((WIDTH, D_MODEL), lambda i: (0, 0), **const),
            pl.BlockSpec((D_MODEL, D_MODEL), lambda i: (0, 0), **const),
            pl.BlockSpec((1, D_MODEL), lambda i: (0, 0)),
        ],
        out_specs=pl.BlockSpec((MIX_TM, D_MODEL), lambda i: (i, 0)),
        out_shape=jax.ShapeDtypeStruct((n, D_MODEL), F32),
        compiler_params=pltpu.CompilerParams(
            dimension_semantics=("arbitrary",), vmem_limit_bytes=VMEM_LIMIT),
        name="mix_out",
    )(o_moba, o_fox, proj, proj, x2, wm, wf, wo, g)


MLP_TM = 512
MLP_TF = 1024


def _mlp_kernel(x_ref, gpre_ref, wu_ref, wd_ref, gpost_ref, out_ref, h_ref, acc_ref):
    f = pl.program_id(1)

    @pl.when(f == 0)
    def _():
        x = x_ref[...]
        h_ref[...] = (x * _rms_scale(x) * gpre_ref[...]).astype(BF16)
        acc_ref[...] = jnp.zeros_like(acc_ref)

    u = _dot(h_ref[...], wu_ref[...])
    a = jnp.square(jnp.maximum(u, 0.0)).astype(BF16)
    acc_ref[...] += _dot(a, wd_ref[...])

    @pl.when(f == pl.num_programs(1) - 1)
    def _():
        mo = acc_ref[...]
        out_ref[...] = x_ref[...] + mo * _rms_scale(mo) * gpost_ref[...]


def _mlp(x1, gpre, wu, wd, gpost):
    n = x1.shape[0]
    return pl.pallas_call(
        _mlp_kernel,
        grid=(n // MLP_TM, D_FF // MLP_TF),
        in_specs=[
            pl.BlockSpec((MLP_TM, D_MODEL), lambda i, f: (i, 0)),
            pl.BlockSpec((1, D_MODEL), lambda i, f: (0, 0)),
            pl.BlockSpec((D_MODEL, MLP_TF), lambda i, f: (0, f)),
            pl.BlockSpec((MLP_TF, D_MODEL), lambda i, f: (f, 0)),
            pl.BlockSpec((1, D_MODEL), lambda i, f: (0, 0)),
        ],
        out_specs=pl.BlockSpec((MLP_TM, D_MODEL), lambda i, f: (i, 0)),
        out_shape=jax.ShapeDtypeStruct((n, D_MODEL), F32),
        scratch_shapes=[pltpu.VMEM((MLP_TM, D_MODEL), BF16), pltpu.VMEM((MLP_TM, D_MODEL), F32)],
        compiler_params=pltpu.CompilerParams(
            dimension_semantics=("arbitrary", "arbitrary"), vmem_limit_bytes=VMEM_LIMIT),
        name="mlp",
    )(x1, gpre, wu, wd, gpost)


def _rope_tables(seq):
    inv_freq = ROPE_THETA ** (-jnp.arange(0, ROPE_DIM, 2, dtype=F32) / ROPE_DIM)
    ang = jnp.arange(seq, dtype=F32)[:, None] * inv_freq[None, :]
    cos, sin = jnp.cos(ang), jnp.sin(ang)
    gap = HEAD_DIM // 2 - ROPE_HALF
    ones = jnp.ones((seq, gap), F32)
    zeros = jnp.zeros((seq, gap), F32)
    cos_t = jnp.concatenate([cos, ones, cos, ones], axis=1)
    sin_t = jnp.concatenate([-sin, zeros, sin, zeros], axis=1)
    return cos_t, sin_t


def _layer(x2, w_in, b_forget, w_bm, w_bf, w_out, g_mix_pre, g_mix_post,
           w_up, w_down, g_mlp_pre, g_mlp_post, batch, seq):
    b_row = jnp.pad(b_forget, (0, LANES - N_HEADS)).reshape(1, LANES)
    cos_t, sin_t = _rope_tables(seq)

    proj, ff, kmean = _in_proj(x2, g_mix_pre.reshape(1, -1), w_in, cos_t, sin_t, seq)
    faug = _fox_bias(ff, b_row, batch, seq)
    kmean = kmean.reshape(batch * (seq // MOBA_BLOCK), WIDTH)
    o_moba, o_fox, (wm, wf, wo), (wu, wd) = _attention(
        proj, kmean, faug, (w_bm, w_bf, w_out), (w_up, w_down), batch, seq)
    x1 = _mix_out(o_moba, o_fox, proj, x2, wm, wf, wo, g_mix_post.reshape(1, -1))
    return _mlp(x1, g_mlp_pre.reshape(1, -1), wu, wd, g_mlp_post.reshape(1, -1))


def kernel(x, w_in, b_forget, w_branch_moba, w_branch_fox, w_out, g_mix_pre, g_mix_post,
           w_up, w_down, g_mlp_pre, g_mlp_post):
    batch, seq, d = x.shape
    assert d == D_MODEL and seq % IN_TM == 0 and seq % ATT_T == 0
    assert w_in.shape[-1] == PROJ_COLS + N_HEADS
    x2 = x.reshape(batch * seq, d)
    for l in range(w_in.shape[0]):
        x2 = _layer(x2, w_in[l], b_forget[l], w_branch_moba[l], w_branch_fox[l], w_out[l],
                    g_mix_pre[l], g_mix_post[l], w_up[l], w_down[l], g_mlp_pre[l],
                    g_mlp_post[l], batch, seq)
    return x2.reshape(batch, seq, d)
```

```python
import functools
import math

import jax
import jax.numpy as jnp
from jax import lax
from jax.experimental import pallas as pl
from jax.experimental.pallas import tpu as pltpu

D_MODEL = 2048
HEAD_DIM = 128
N_HEADS = 8
WIDTH = N_HEADS * HEAD_DIM
MOBA_BLOCK = 256
MOBA_TOP_K = 3
ROPE_THETA = 500000.0
ROPE_DIM = HEAD_DIM // 4
ROPE_HALF = ROPE_DIM // 2
D_FF = 4 * D_MODEL
RMS_EPS = 1e-6
NEG_BIG = -1e30
LANES = 128
SUBLANES = 8

QKV_COLS = 6 * WIDTH
GATE_COLS = 2 * D_MODEL
PROJ_COLS = QKV_COLS + GATE_COLS

VMEM_LIMIT = 56 * 1024 * 1024

BF16 = jnp.bfloat16
F32 = jnp.float32
LOG2E = math.log2(math.e)
Q_SCALE = HEAD_DIM ** -0.5 * LOG2E


def _rms_scale(x):
    return lax.rsqrt(jnp.mean(x * x, axis=-1, keepdims=True) + RMS_EPS)


def _split3(x):
    hi = x.astype(BF16)
    r1 = x - hi.astype(F32)
    mid = r1.astype(BF16)
    lo = (r1 - mid.astype(F32)).astype(BF16)
    return hi, mid, lo


def _dot(a, b):
    return jnp.dot(a, b, preferred_element_type=F32)


def _dot_nt(a, b):
    return lax.dot_general(a, b, (((1,), (1,)), ((), ())), preferred_element_type=F32)


IN_TM = 1024
IN_TN = 1024
IN_ROT_TILES = 2 * WIDTH // IN_TN


def _permute_rotary_rows(blk):
    mid = HEAD_DIM // 2 + ROPE_HALF
    parts = []
    for hd in range(blk.shape[0] // HEAD_DIM):
        b = hd * HEAD_DIM
        parts += [blk[b:b + ROPE_HALF], blk[b + ROPE_DIM:b + mid],
                  blk[b + ROPE_HALF:b + ROPE_DIM], blk[b + mid:b + HEAD_DIM]]
    return jnp.concatenate(parts, axis=0)


def _in_proj_kernel(x_ref, g_ref, w_ref, wff_ref, cos_ref, sin_ref,
                    proj_ref, ff_ref, kmean_ref, h_ref):
    j = pl.program_id(1)

    @pl.when(j == 0)
    def _():
        x = x_ref[...]
        h = (x * _rms_scale(x) * g_ref[...]).astype(BF16)
        h_ref[...] = h
        ff_ref[...] = _dot_nt(h, wff_ref[...].astype(BF16))

    def tile(permute):
        w = w_ref[...]
        w = (_permute_rotary_rows(w) if permute else w).astype(BF16)
        return _dot_nt(h_ref[...], w)

    def rotary(t):
        outs = []
        for hd in range(IN_TN // HEAD_DIM):
            th = t[:, hd * HEAD_DIM:(hd + 1) * HEAD_DIM]
            outs.append(th * cos_ref[...] + pltpu.roll(th, HEAD_DIM // 2, 1) * sin_ref[...])
        return jnp.concatenate(outs, axis=1)

    @pl.when(j == 0)
    def _():
        proj_ref[...] = (rotary(tile(True)) * Q_SCALE).astype(BF16)

    @pl.when(j == 1)
    def _():
        kr = rotary(tile(True))
        proj_ref[...] = kr.astype(BF16)
        nblk = IN_TM // MOBA_BLOCK
        means = [jnp.mean(kr[b * MOBA_BLOCK:(b + 1) * MOBA_BLOCK, :], axis=0, keepdims=True)
                 for b in range(nblk)]
        kmean_ref[0] = jnp.concatenate(means, axis=0)

    @pl.when(j == 3)
    def _():
        proj_ref[...] = (tile(False) * Q_SCALE).astype(BF16)

    @pl.when((j == 2) | (j >= 4))
    def _():
        proj_ref[...] = tile(False).astype(BF16)


def _in_proj(x2, g, w_in, cos_t, sin_t, seq):
    assert IN_ROT_TILES == 2
    n = x2.shape[0]
    seq_tiles = seq // IN_TM
    n_qkv = QKV_COLS // IN_TN
    gate0 = QKV_COLS + N_HEADS
    assert gate0 % SUBLANES == 0
    wt = w_in.T

    def src_row(j):
        return pl.multiple_of(jnp.where(j < n_qkv, j * IN_TN, gate0 + (j - n_qkv) * IN_TN), SUBLANES)

    return pl.pallas_call(
        _in_proj_kernel,
        grid=(n // IN_TM, PROJ_COLS // IN_TN),
        in_specs=[
            pl.BlockSpec((IN_TM, D_MODEL), lambda i, j: (i, 0)),
            pl.BlockSpec((1, D_MODEL), lambda i, j: (0, 0)),
            pl.BlockSpec((pl.Element(IN_TN), pl.Element(D_MODEL)), lambda i, j: (src_row(j), 0)),
            pl.BlockSpec((LANES, D_MODEL), lambda i, j: (QKV_COLS // LANES, 0),
                         pipeline_mode=pl.Buffered(1)),
            pl.BlockSpec((IN_TM, HEAD_DIM), lambda i, j: (i % seq_tiles, 0)),
            pl.BlockSpec((IN_TM, HEAD_DIM), lambda i, j: (i % seq_tiles, 0)),
        ],
        out_specs=[
            pl.BlockSpec((IN_TM, IN_TN), lambda i, j: (i, j)),
            pl.BlockSpec((IN_TM, LANES), lambda i, j: (i, 0)),
            pl.BlockSpec((1, IN_TM // MOBA_BLOCK, WIDTH), lambda i, j: (i, 0, 0)),
        ],
        out_shape=[
            jax.ShapeDtypeStruct((n, PROJ_COLS), BF16),
            jax.ShapeDtypeStruct((n, LANES), F32),
            jax.ShapeDtypeStruct((n // IN_TM, IN_TM // MOBA_BLOCK, WIDTH), F32),
        ],
        scratch_shapes=[pltpu.VMEM((IN_TM, D_MODEL), BF16)],
        compiler_params=pltpu.CompilerParams(
            dimension_semantics=("arbitrary", "arbitrary"), vmem_limit_bytes=VMEM_LIMIT),
        name="in_proj",
    )(x2, g, wt, wt, cos_t, sin_t)


FB_CHUNK = 256


def _fox_bias_kernel(ff_ref, b_ref, out_ref):
    seq = ff_ref.shape[0]
    r = lax.broadcasted_iota(jnp.int32, (FB_CHUNK, FB_CHUNK), 0)
    c = lax.broadcasted_iota(jnp.int32, (FB_CHUNK, FB_CHUNK), 1)
    ltri = (c <= r).astype(BF16)
    pr = lax.broadcasted_iota(jnp.int32, (LANES, LANES), 0)
    pc = lax.broadcasted_iota(jnp.int32, (LANES, LANES), 1)
    place = [((pc == 3 * pr + t) & (pr < N_HEADS)).astype(BF16) for t in range(3)]

    local = []
    for ci in range(seq // FB_CHUNK):
        z = ff_ref[ci * FB_CHUNK:(ci + 1) * FB_CHUNK, :] + b_ref[...]
        logf = -(jnp.maximum(-z, 0.0) + jnp.log1p(jnp.exp(-jnp.abs(z))))
        hi, mid, lo = _split3(logf)
        local.append(_dot(ltri, hi) + _dot(ltri, mid) + _dot(ltri, lo))
    carry = jnp.zeros((1, LANES), F32)
    for ci, loc in enumerate(local):
        f = loc + carry
        carry = carry + loc[FB_CHUNK - 1:FB_CHUNK, :]
        nh, nm, nl = _split3(f * (-LOG2E))
        aug = _dot(nh, place[0]) + _dot(nm, place[1]) + _dot(nl, place[2])
        out_ref[ci * FB_CHUNK:(ci + 1) * FB_CHUNK, :] = aug.astype(BF16)


def _fox_bias(ff, b_row, batch, seq):
    return pl.pallas_call(
        _fox_bias_kernel,
        grid=(batch,),
        in_specs=[
            pl.BlockSpec((seq, LANES), lambda b: (b, 0)),
            pl.BlockSpec((1, LANES), lambda b: (0, 0)),
        ],
        out_specs=pl.BlockSpec((seq, LANES), lambda b: (b, 0)),
        out_shape=jax.ShapeDtypeStruct((batch * seq, LANES), BF16),
        compiler_params=pltpu.CompilerParams(
            dimension_semantics=("arbitrary",), vmem_limit_bytes=VMEM_LIMIT),
        name="fox_bias",
    )(ff, b_row)


ATT_T = 512
ATT_HG = 4
ATT_SUB = ATT_T // MOBA_BLOCK


def _attn_kernel(is_fox, n_cast, *refs):
    q_ref, k_ref, v_ref, x_ref = refs[:4]
    cast_src = refs[4:4 + n_cast]
    o_ref = refs[4 + n_cast]
    cast_dst = refs[5 + n_cast:5 + 2 * n_cast]
    vt_ref, acc_ref, sa_ref, sb_ref, *maybe_sel = refs[5 + 2 * n_cast:]
    hg = pl.program_id(1)
    qt = pl.program_id(2)
    seq = k_ref.shape[0]
    nblk = seq // MOBA_BLOCK
    t = ATT_T

    for src, dst in zip(cast_src, cast_dst):
        dst[...] = src[...].astype(BF16)

    def head_cols(hh):
        return slice(hh * HEAD_DIM, (hh + 1) * HEAD_DIM)

    @pl.when(qt == 0)
    def _():
        for hh in range(ATT_HG):
            for ci in range(nblk):
                rows = slice(ci * MOBA_BLOCK, (ci + 1) * MOBA_BLOCK)
                blk = v_ref[rows, head_cols(hh)].astype(F32)
                vt_ref[head_cols(hh), rows] = blk.T.astype(BF16)

    qs = []
    for hh in range(ATT_HG):
        qh = q_ref[:, head_cols(hh)]
        if is_fox:
            head = hg * ATT_HG + hh
            lane = lax.broadcasted_iota(jnp.int32, (t, LANES), 1)
            pick = ((lane >= 3 * head) & (lane < 3 * head + 3)).astype(BF16)
            qh = jnp.concatenate([qh, pick], axis=1)
        qs.append(qh)

    if not is_fox:
        sel_ref, = maybe_sel
        blk_id = lax.broadcasted_iota(jnp.int32, (nblk, t), 0)
        q_blk = qt * ATT_SUB + lax.broadcasted_iota(jnp.int32, (nblk, t), 1) // MOBA_BLOCK
        past = blk_id < q_blk
        for hh in range(ATT_HG):
            kh, km, kl = _split3(x_ref[:, head_cols(hh)])
            gate = _dot_nt(kh, qs[hh]) + _dot_nt(km, qs[hh]) + _dot_nt(kl, qs[hh])
            gate = jnp.where(past, gate, -jnp.inf)
            rank = jnp.zeros((nblk, t), jnp.int32)
            for mth in range(nblk):
                gm = gate[mth:mth + 1, :]
                ahead = (gm > gate) | ((gm == gate) & (mth < blk_id))
                rank = rank + ahead.astype(jnp.int32)
            sel_ref[hh] = ((rank < MOBA_TOP_K) & past).astype(F32)

    def scores(hh, start):
        kb = k_ref[pl.ds(start, t), head_cols(hh)]
        if is_fox:
            kb = jnp.concatenate([kb, x_ref[pl.ds(start, t), :]], axis=1)
        return _dot_nt(kb, qs[hh])

    def mask_rows(hh, s, first_blk, diagonal):
        parts = []
        for r in range(ATT_SUB):
            keep = sel_ref[hh, pl.ds(first_blk + r, 1), :] > 0.5
            if diagonal:
                key = r * MOBA_BLOCK + lax.broadcasted_iota(jnp.int32, (MOBA_BLOCK, t), 0)
                qry = lax.broadcasted_iota(jnp.int32, (MOBA_BLOCK, t), 1)
                keep = keep | ((key <= qry) & (qry < (r + 1) * MOBA_BLOCK))
            parts.append(jnp.where(keep, s[r * MOBA_BLOCK:(r + 1) * MOBA_BLOCK, :], NEG_BIG))
        return jnp.concatenate(parts, axis=0)

    def vt_chunk(hh, start):
        return vt_ref[head_cols(hh), pl.ds(start, t)]

    def produce(dst_ref, start, first_blk, diagonal):
        cmax = []
        for hh in range(ATT_HG):
            s = scores(hh, start)
            if is_fox:
                if diagonal:
                    key = lax.broadcasted_iota(jnp.int32, (t, t), 0)
                    qry = lax.broadcasted_iota(jnp.int32, (t, t), 1)
                    s = jnp.where(key <= qry, s, NEG_BIG)
            else:
                s = mask_rows(hh, s, first_blk, diagonal)
            dst_ref[hh] = s
            cmax.append(jnp.max(s, axis=0, keepdims=True))
        return tuple(cmax)

    def consume(src_ref, start, cmax, stats):
        new = []
        for hh in range(ATT_HG):
            m, l = stats[hh]
            m_new = jnp.maximum(m, cmax[hh])
            alpha = jnp.exp2(m - m_new)
            p = jnp.exp2(src_ref[hh] - m_new)
            l = alpha * l + jnp.sum(p, axis=0, keepdims=True)
            acc_ref[head_cols(hh), :] = (alpha * acc_ref[head_cols(hh), :]
                                         + _dot(vt_chunk(hh, start), p.astype(BF16)))
            new.append((m_new, l))
        return tuple(new)

    own = pl.multiple_of(qt * t, t)
    acc_ref[...] = jnp.zeros_like(acc_ref)
    stats = tuple((jnp.full((1, t), NEG_BIG, F32), jnp.zeros((1, t), F32)) for _ in range(ATT_HG))
    cmax = produce(sa_ref, own, qt * ATT_SUB, True)

    def step(dst_ref, src_ref):
        def run(i, cmax, stats):
            nxt = pl.multiple_of(i * t, t)
            cur = pl.multiple_of(jnp.where(i == 0, qt, i - 1) * t, t)
            cmax_next = produce(dst_ref, nxt, i * ATT_SUB, False)
            return cmax_next, consume(src_ref, cur, cmax, stats)
        return run

    def body(i, carry):
        return lax.cond(i % 2 == 0, step(sb_ref, sa_ref), step(sa_ref, sb_ref), i, *carry)

    cmax, stats = lax.fori_loop(0, qt, body, (cmax, stats))
    last = pl.multiple_of(jnp.maximum(qt - 1, 0) * t, t)
    stats = lax.cond(qt % 2 == 0,
                     lambda c, s: consume(sa_ref, last, c, s),
                     lambda c, s: consume(sb_ref, last, c, s), cmax, stats)
    for hh in range(ATT_HG):
        _, l = stats[hh]
        o_ref[:, head_cols(hh)] = (acc_ref[head_cols(hh), :] * (1.0 / l)).T.astype(BF16)


def _attention(proj, kmean, faug, moba_riders, fox_riders, batch, seq):
    n = batch * seq
    nq = seq // ATT_T
    nblk = seq // MOBA_BLOCK
    hw = ATT_HG * HEAD_DIM
    n_hg = N_HEADS // ATT_HG
    grid = (batch, n_hg, nq)
    n_steps = batch * n_hg * nq
    params = pltpu.CompilerParams(
        dimension_semantics=("arbitrary", "arbitrary", "arbitrary"),
        vmem_limit_bytes=VMEM_LIMIT)

    def q_spec(col0):
        return pl.BlockSpec((ATT_T, hw), lambda b, h, i: (b * nq + i, col0 + h))

    def kv_spec(col0):
        return pl.BlockSpec((seq, hw), lambda b, h, i: (b, col0 + h))

    def rider_spec(w):
        rows = w.shape[0] // n_steps
        assert rows * n_steps == w.shape[0] and rows % 16 == 0
        return pl.BlockSpec((rows, w.shape[1]), lambda b, h, i: ((b * n_hg + h) * nq + i, 0))

    region = WIDTH // hw
    out_spec = pl.BlockSpec((ATT_T, hw), lambda b, h, i: (b * nq + i, h))
    out_shape = jax.ShapeDtypeStruct((n, WIDTH), BF16)
    common = [pltpu.VMEM((hw, seq), BF16), pltpu.VMEM((hw, ATT_T), F32),
              pltpu.VMEM((ATT_HG, ATT_T, ATT_T), F32), pltpu.VMEM((ATT_HG, ATT_T, ATT_T), F32)]

    def call(is_fox, name, specs, args, riders, extra_scratch):
        rider_specs = [rider_spec(w) for w in riders]
        return pl.pallas_call(
            functools.partial(_attn_kernel, is_fox, len(riders)),
            grid=grid,
            in_specs=specs + rider_specs,
            out_specs=[out_spec] + rider_specs,
            out_shape=[out_shape] + [jax.ShapeDtypeStruct(w.shape, BF16) for w in riders],
            scratch_shapes=common + extra_scratch,
            compiler_params=params,
            name=name,
        )(*args, *riders)

    o_moba, *moba_cast = call(
        False, "moba_attn",
        [q_spec(0), kv_spec(region), kv_spec(2 * region),
         pl.BlockSpec((nblk, hw), lambda b, h, i: (b, h))],
        (proj, proj, proj, kmean), moba_riders, [pltpu.VMEM((ATT_HG, nblk, ATT_T), F32)])
    o_fox, *fox_cast = call(
        True, "fox_attn",
        [q_spec(3 * region), kv_spec(4 * region), kv_spec(5 * region),
         pl.BlockSpec((seq, LANES), lambda b, h, i: (b, 0))],
        (proj, proj, proj, faug), fox_riders, [])
    return o_moba, o_fox, moba_cast, fox_cast


MIX_TM = 256


def _sigmoid(z):
    return 1.0 / (1.0 + jnp.exp(-z))


def _mix_kernel(om_ref, of_ref, ga_ref, gb_ref, x_ref, wm_ref, wf_ref, wo_ref, g_ref, out_ref):
    y_m = _dot(om_ref[...], wm_ref[...])
    y_f = _dot(of_ref[...], wf_ref[...])
    merged = _sigmoid(ga_ref[...].astype(F32)) * y_m + _sigmoid(gb_ref[...].astype(F32)) * y_f
    mixed = _dot(merged.astype(BF16), wo_ref[...])
    out_ref[...] = x_ref[...] + mixed * _rms_scale(mixed) * g_ref[...]


def _mix_out(o_moba, o_fox, proj, x2, wm, wf, wo, g):
    n = x2.shape[0]
    ga_blk = QKV_COLS // D_MODEL
    const = dict(pipeline_mode=pl.Buffered(1))
    return pl.pallas_call(
        _mix_kernel,
        grid=(n // MIX_TM,),
        in_specs=[
            pl.BlockSpec((MIX_TM, WIDTH), lambda i: (i, 0)),
            pl.BlockSpec((MIX_TM, WIDTH), lambda i: (i, 0)),
            pl.BlockSpec((MIX_TM, D_MODEL), lambda i: (i, ga_blk)),
            pl.BlockSpec((MIX_TM, D_MODEL), lambda i: (i, ga_blk + 1)),
            pl.BlockSpec((MIX_TM, D_MODEL), lambda i: (i, 0)),
            pl.BlockSpec((WIDTH, D_MODEL), lambda i: (0, 0), **const),
            pl.BlockSpec((WIDTH, D_MODEL), lambda i: (0, 0), **const),
            pl.BlockSpec((D_MODEL, D_MODEL), lambda i: (0, 0), **const),
            pl.BlockSpec((1, D_MODEL), lambda i: (0, 0)),
        ],
        out_specs=pl.BlockSpec((MIX_TM, D_MODEL), lambda i: (i, 0)),
        out_shape=jax.ShapeDtypeStruct((n, D_MODEL), F32),
        compiler_params=pltpu.CompilerParams(
            dimension_semantics=("arbitrary",), vmem_limit_bytes=VMEM_LIMIT),
        name="mix_out",
    )(o_moba, o_fox, proj, proj, x2, wm, wf, wo, g)


MLP_TM = 512
MLP_TF = 1024


def _mlp_kernel(x_ref, gpre_ref, wu_ref, wd_ref, gpost_ref, out_ref, h_ref, acc_ref):
    f = pl.program_id(1)

    @pl.when(f == 0)
    def _():
        x = x_ref[...]
        h_ref[...] = (x * _rms_scale(x) * gpre_ref[...]).astype(BF16)
        acc_ref[...] = jnp.zeros_like(acc_ref)

    u = _dot(h_ref[...], wu_ref[...])
    a = jnp.square(jnp.maximum(u, 0.0)).astype(BF16)
    acc_ref[...] += _dot(a, wd_ref[...])

    @pl.when(f == pl.num_programs(1) - 1)
    def _():
        mo = acc_ref[...]
        out_ref[...] = x_ref[...] + mo * _rms_scale(mo) * gpost_ref[...]


def _mlp(x1, gpre, wu, wd, gpost):
    n = x1.shape[0]
    return pl.pallas_call(
        _mlp_kernel,
        grid=(n // MLP_TM, D_FF // MLP_TF),
        in_specs=[
            pl.BlockSpec((MLP_TM, D_MODEL), lambda i, f: (i, 0)),
            pl.BlockSpec((1, D_MODEL), lambda i, f: (0, 0)),
            pl.BlockSpec((D_MODEL, MLP_TF), lambda i, f: (0, f)),
            pl.BlockSpec((MLP_TF, D_MODEL), lambda i, f: (f, 0)),
            pl.BlockSpec((1, D_MODEL), lambda i, f: (0, 0)),
        ],
        out_specs=pl.BlockSpec((MLP_TM, D_MODEL), lambda i, f: (i, 0)),
        out_shape=jax.ShapeDtypeStruct((n, D_MODEL), F32),
        scratch_shapes=[pltpu.VMEM((MLP_TM, D_MODEL), BF16), pltpu.VMEM((MLP_TM, D_MODEL), F32)],
        compiler_params=pltpu.CompilerParams(
            dimension_semantics=("arbitrary", "arbitrary"), vmem_limit_bytes=VMEM_LIMIT),
        name="mlp",
    )(x1, gpre, wu, wd, gpost)


def _rope_tables(seq):
    inv_freq = ROPE_THETA ** (-jnp.arange(0, ROPE_DIM, 2, dtype=F32) / ROPE_DIM)
    ang = jnp.arange(seq, dtype=F32)[:, None] * inv_freq[None, :]
    cos, sin = jnp.cos(ang), jnp.sin(ang)
    gap = HEAD_DIM // 2 - ROPE_HALF
    ones = jnp.ones((seq, gap), F32)
    zeros = jnp.zeros((seq, gap), F32)
    cos_t = jnp.concatenate([cos, ones, cos, ones], axis=1)
    sin_t = jnp.concatenate([-sin, zeros, sin, zeros], axis=1)
    return cos_t, sin_t


def _layer(x2, w_in, b_forget, w_bm, w_bf, w_out, g_mix_pre, g_mix_post,
           w_up, w_down, g_mlp_pre, g_mlp_post, batch, seq):
    b_row = jnp.pad(b_forget, (0, LANES - N_HEADS)).reshape(1, LANES)
    cos_t, sin_t = _rope_tables(seq)

    proj, ff, kmean = _in_proj(x2, g_mix_pre.reshape(1, -1), w_in, cos_t, sin_t, seq)
    faug = _fox_bias(ff, b_row, batch, seq)
    kmean = kmean.reshape(batch * (seq // MOBA_BLOCK), WIDTH)
    o_moba, o_fox, (wm, wf, wo), (wu, wd) = _attention(
        proj, kmean, faug, (w_bm, w_bf, w_out), (w_up, w_down), batch, seq)
    x1 = _mix_out(o_moba, o_fox, proj, x2, wm, wf, wo, g_mix_post.reshape(1, -1))
    return _mlp(x1, g_mlp_pre.reshape(1, -1), wu, wd, g_mlp_post.reshape(1, -1))


def kernel(x, w_in, b_forget, w_branch_moba, w_branch_fox, w_out, g_mix_pre, g_mix_post,
           w_up, w_down, g_mlp_pre, g_mlp_post):
    batch, seq, d = x.shape
    assert d == D_MODEL and seq % IN_TM == 0 and seq % ATT_T == 0
    assert w_in.shape[-1] == PROJ_COLS + N_HEADS
    x2 = x.reshape(batch * seq, d)
    for l in range(w_in.shape[0]):
        x2 = _layer(x2, w_in[l], b_forget[l], w_branch_moba[l], w_branch_fox[l], w_out[l],
                    g_mix_pre[l], g_mix_post[l], w_up[l], w_down[l], g_mlp_pre[l],
                    g_mlp_post[l], batch, seq)
    return x2.reshape(batch, seq, d)
```

```python
import functools
import math

import jax
import jax.numpy as jnp
from jax import lax
from jax.experimental import pallas as pl
from jax.experimental.pallas import tpu as pltpu

D_MODEL = 2048
HEAD_DIM = 128
N_HEADS = 8
WIDTH = N_HEADS * HEAD_DIM
MOBA_BLOCK = 256
MOBA_TOP_K = 3
ROPE_THETA = 500000.0
ROPE_DIM = HEAD_DIM // 4
ROPE_HALF = ROPE_DIM // 2
D_FF = 4 * D_MODEL
RMS_EPS = 1e-6
NEG_BIG = -1e30
LANES = 128
SUBLANES = 8

QKV_COLS = 6 * WIDTH
GATE_COLS = 2 * D_MODEL
PROJ_COLS = QKV_COLS + GATE_COLS

VMEM_LIMIT = 56 * 1024 * 1024

BF16 = jnp.bfloat16
F32 = jnp.float32
LOG2E = math.log2(math.e)
Q_SCALE = HEAD_DIM ** -0.5 * LOG2E


def _rms_scale(x):
    return lax.rsqrt(jnp.mean(x * x, axis=-1, keepdims=True) + RMS_EPS)


def _split3(x):
    hi = x.astype(BF16)
    r1 = x - hi.astype(F32)
    mid = r1.astype(BF16)
    lo = (r1 - mid.astype(F32)).astype(BF16)
    return hi, mid, lo


def _dot(a, b):
    return jnp.dot(a, b, preferred_element_type=F32)


def _dot_nt(a, b):
    return lax.dot_general(a, b, (((1,), (1,)), ((), ())), preferred_element_type=F32)


IN_TM = 1024
IN_TN = 1024
IN_ROT_TILES = 2 * WIDTH // IN_TN


def _permute_rotary_rows(blk):
    mid = HEAD_DIM // 2 + ROPE_HALF
    parts = []
    for hd in range(blk.shape[0] // HEAD_DIM):
        b = hd * HEAD_DIM
        parts += [blk[b:b + ROPE_HALF], blk[b + ROPE_DIM:b + mid],
                  blk[b + ROPE_HALF:b + ROPE_DIM], blk[b + mid:b + HEAD_DIM]]
    return jnp.concatenate(parts, axis=0)


def _in_proj_kernel(x_ref, g_ref, w_ref, wff_ref, cos_ref, sin_ref,
                    proj_ref, ff_ref, kmean_ref, h_ref):
    j = pl.program_id(1)

    @pl.when(j == 0)
    def _():
        x = x_ref[...]
        h = (x * _rms_scale(x) * g_ref[...]).astype(BF16)
        h_ref[...] = h
        ff_ref[...] = _dot_nt(h, wff_ref[...].astype(BF16))

    def tile(permute):
        w = w_ref[...]
        w = (_permute_rotary_rows(w) if permute else w).astype(BF16)
        return _dot_nt(h_ref[...], w)

    def rotary(t):
        outs = []
        for hd in range(IN_TN // HEAD_DIM):
            th = t[:, hd * HEAD_DIM:(hd + 1) * HEAD_DIM]
            outs.append(th * cos_ref[...] + pltpu.roll(th, HEAD_DIM // 2, 1) * sin_ref[...])
        return jnp.concatenate(outs, axis=1)

    @pl.when(j == 0)
    def _():
        proj_ref[...] = (rotary(tile(True)) * Q_SCALE).astype(BF16)

    @pl.when(j == 1)
    def _():
        kr = rotary(tile(True))
        proj_ref[...] = kr.astype(BF16)
        nblk = IN_TM // MOBA_BLOCK
        means = [jnp.mean(kr[b * MOBA_BLOCK:(b + 1) * MOBA_BLOCK, :], axis=0, keepdims=True)
                 for b in range(nblk)]
        kmean_ref[0] = jnp.concatenate(means, axis=0)

    @pl.when(j == 3)
    def _():
        proj_ref[...] = (tile(False) * Q_SCALE).astype(BF16)

    @pl.when((j == 2) | (j >= 4))
    def _():
        proj_ref[...] = tile(False).astype(BF16)


def _in_proj(x2, g, w_in, cos_t, sin_t, seq):
    assert IN_ROT_TILES == 2
    n = x2.shape[0]
    seq_tiles = seq // IN_TM
    n_qkv = QKV_COLS // IN_TN
    gate0 = QKV_COLS + N_HEADS
    assert gate0 % SUBLANES == 0
    wt = w_in.T

    def src_row(j):
        return pl.multiple_of(jnp.where(j < n_qkv, j * IN_TN, gate0 + (j - n_qkv) * IN_TN), SUBLANES)

    return pl.pallas_call(
        _in_proj_kernel,
        grid=(n // IN_TM, PROJ_COLS // IN_TN),
        in_specs=[
            pl.BlockSpec((IN_TM, D_MODEL), lambda i, j: (i, 0)),
            pl.BlockSpec((1, D_MODEL), lambda i, j: (0, 0)),
            pl.BlockSpec((pl.Element(IN_TN), pl.Element(D_MODEL)), lambda i, j: (src_row(j), 0)),
            pl.BlockSpec((LANES, D_MODEL), lambda i, j: (QKV_COLS // LANES, 0),
                         pipeline_mode=pl.Buffered(1)),
            pl.BlockSpec((IN_TM, HEAD_DIM), lambda i, j: (i % seq_tiles, 0)),
            pl.BlockSpec((IN_TM, HEAD_DIM), lambda i, j: (i % seq_tiles, 0)),
        ],
        out_specs=[
            pl.BlockSpec((IN_TM, IN_TN), lambda i, j: (i, j)),
            pl.BlockSpec((IN_TM, LANES), lambda i, j: (i, 0)),
            pl.BlockSpec((1, IN_TM // MOBA_BLOCK, WIDTH), lambda i, j: (i, 0, 0)),
        ],
        out_shape=[
            jax.ShapeDtypeStruct((n, PROJ_COLS), BF16),
            jax.ShapeDtypeStruct((n, LANES), F32),
            jax.ShapeDtypeStruct((n // IN_TM, IN_TM // MOBA_BLOCK, WIDTH), F32),
        ],
        scratch_shapes=[pltpu.VMEM((IN_TM, D_MODEL), BF16)],
        compiler_params=pltpu.CompilerParams(
            dimension_semantics=("arbitrary", "arbitrary"), vmem_limit_bytes=VMEM_LIMIT),
        name="in_proj",
    )(x2, g, wt, wt, cos_t, sin_t)


FB_CHUNK = 256


def _fox_bias_kernel(ff_ref, b_ref, out_ref):
    seq = ff_ref.shape[0]
    r = lax.broadcasted_iota(jnp.int32, (FB_CHUNK, FB_CHUNK), 0)
    c = lax.broadcasted_iota(jnp.int32, (FB_CHUNK, FB_CHUNK), 1)
    ltri = (c <= r).astype(BF16)
    pr = lax.broadcasted_iota(jnp.int32, (LANES, LANES), 0)
    pc = lax.broadcasted_iota(jnp.int32, (LANES, LANES), 1)
    place = [((pc == 3 * pr + t) & (pr < N_HEADS)).astype(BF16) for t in range(3)]

    local = []
    for ci in range(seq // FB_CHUNK):
        z = ff_ref[ci * FB_CHUNK:(ci + 1) * FB_CHUNK, :] + b_ref[...]
        logf = -(jnp.maximum(-z, 0.0) + jnp.log1p(jnp.exp(-jnp.abs(z))))
        hi, mid, lo = _split3(logf)
        local.append(_dot(ltri, hi) + _dot(ltri, mid) + _dot(ltri, lo))
    carry = jnp.zeros((1, LANES), F32)
    for ci, loc in enumerate(local):
        f = loc + carry
        carry = carry + loc[FB_CHUNK - 1:FB_CHUNK, :]
        nh, nm, nl = _split3(f * (-LOG2E))
        aug = _dot(nh, place[0]) + _dot(nm, place[1]) + _dot(nl, place[2])
        out_ref[ci * FB_CHUNK:(ci + 1) * FB_CHUNK, :] = aug.astype(BF16)


def _fox_bias(ff, b_row, batch, seq):
    return pl.pallas_call(
        _fox_bias_kernel,
        grid=(batch,),
        in_specs=[
            pl.BlockSpec((seq, LANES), lambda b: (b, 0)),
            pl.BlockSpec((1, LANES), lambda b: (0, 0)),
        ],
        out_specs=pl.BlockSpec((seq, LANES), lambda b: (b, 0)),
        out_shape=jax.ShapeDtypeStruct((batch * seq, LANES), BF16),
        compiler_params=pltpu.CompilerParams(
            dimension_semantics=("arbitrary",), vmem_limit_bytes=VMEM_LIMIT),
        name="fox_bias",
    )(ff, b_row)


ATT_T = 512
ATT_HG = 4
ATT_TILES = 2
ATT_SUB = ATT_T // MOBA_BLOCK


def _attn_kernel(is_fox, n_cast, *refs):
    q_ref, k_ref, v_ref, x_ref = refs[:4]
    cast_src = refs[4:4 + n_cast]
    o_ref = refs[4 + n_cast]
    cast_dst = refs[5 + n_cast:5 + 2 * n_cast]
    vt_ref, acc_ref, sa_ref, sb_ref, *maybe_sel = refs[5 + 2 * n_cast:]
    hg = pl.program_id(1)
    pair = pl.program_id(2)
    seq = k_ref.shape[0]
    nblk = seq // MOBA_BLOCK
    t = ATT_T

    for src, dst in zip(cast_src, cast_dst):
        dst[...] = src[...].astype(BF16)

    def head_cols(hh):
        return slice(hh * HEAD_DIM, (hh + 1) * HEAD_DIM)

    def tile_rows(tile):
        return slice(tile * t, (tile + 1) * t)

    @pl.when(pair == 0)
    def _():
        for hh in range(ATT_HG):
            for ci in range(nblk):
                rows = slice(ci * MOBA_BLOCK, (ci + 1) * MOBA_BLOCK)
                blk = v_ref[rows, head_cols(hh)].astype(F32)
                vt_ref[head_cols(hh), rows] = blk.T.astype(BF16)

    def prep(tile, qt):
        qs = []
        for hh in range(ATT_HG):
            qh = q_ref[tile_rows(tile), head_cols(hh)]
            if is_fox:
                head = hg * ATT_HG + hh
                lane = lax.broadcasted_iota(jnp.int32, (t, LANES), 1)
                pick = ((lane >= 3 * head) & (lane < 3 * head + 3)).astype(BF16)
                qh = jnp.concatenate([qh, pick], axis=1)
            qs.append(qh)
        if not is_fox:
            sel_ref, = maybe_sel
            blk_id = lax.broadcasted_iota(jnp.int32, (nblk, t), 0)
            q_blk = qt * ATT_SUB + lax.broadcasted_iota(jnp.int32, (nblk, t), 1) // MOBA_BLOCK
            past = blk_id < q_blk
            for hh in range(ATT_HG):
                kh, km, kl = _split3(x_ref[:, head_cols(hh)])
                gate = _dot_nt(kh, qs[hh]) + _dot_nt(km, qs[hh]) + _dot_nt(kl, qs[hh])
                gate = jnp.where(past, gate, -jnp.inf)
                rank = jnp.zeros((nblk, t), jnp.int32)
                for mth in range(nblk):
                    gm = gate[mth:mth + 1, :]
                    ahead = (gm > gate) | ((gm == gate) & (mth < blk_id))
                    rank = rank + ahead.astype(jnp.int32)
                sel_ref[tile, hh] = ((rank < MOBA_TOP_K) & past).astype(F32)
        return qs

    def scores(qs, hh, start):
        kb = k_ref[pl.ds(start, t), head_cols(hh)]
        if is_fox:
            kb = jnp.concatenate([kb, x_ref[pl.ds(start, t), :]], axis=1)
        return _dot_nt(kb, qs[hh])

    def mask_rows(tile, hh, s, first_blk, diagonal):
        sel_ref, = maybe_sel
        parts = []
        for r in range(ATT_SUB):
            keep = sel_ref[tile, hh, pl.ds(first_blk + r, 1), :] > 0.5
            if diagonal:
                key = r * MOBA_BLOCK + lax.broadcasted_iota(jnp.int32, (MOBA_BLOCK, t), 0)
                qry = lax.broadcasted_iota(jnp.int32, (MOBA_BLOCK, t), 1)
                keep = keep | ((key <= qry) & (qry < (r + 1) * MOBA_BLOCK))
            parts.append(jnp.where(keep, s[r * MOBA_BLOCK:(r + 1) * MOBA_BLOCK, :], NEG_BIG))
        return jnp.concatenate(parts, axis=0)

    def vt_chunk(hh, start):
        return vt_ref[head_cols(hh), pl.ds(start, t)]

    def produce(tile, qs, dst_ref, start, first_blk, diagonal):
        cmax = []
        for hh in range(ATT_HG):
            s = scores(qs, hh, start)
            if is_fox:
                if diagonal:
                    key = lax.broadcasted_iota(jnp.int32, (t, t), 0)
                    qry = lax.broadcasted_iota(jnp.int32, (t, t), 1)
                    s = jnp.where(key <= qry, s, NEG_BIG)
            else:
                s = mask_rows(tile, hh, s, first_blk, diagonal)
            dst_ref[hh] = s
            cmax.append(jnp.max(s, axis=0, keepdims=True))
        return tuple(cmax)

    def consume(tile, src_ref, start, cmax, stats):
        new = []
        for hh in range(ATT_HG):
            m, l = stats[hh]
            m_new = jnp.maximum(m, cmax[hh])
            alpha = jnp.exp2(m - m_new)
            p = jnp.exp2(src_ref[hh] - m_new)
            l = alpha * l + jnp.sum(p, axis=0, keepdims=True)
            acc_ref[tile, head_cols(hh), :] = (alpha * acc_ref[tile, head_cols(hh), :]
                                               + _dot(vt_chunk(hh, start), p.astype(BF16)))
            new.append((m_new, l))
        return tuple(new)

    def sweep(tile, qs, qt, even, odd, cmax, stats):
        def step(dst_ref, src_ref):
            def run(i, cmax, stats):
                nxt = pl.multiple_of(i * t, t)
                cur = pl.multiple_of(jnp.where(i == 0, qt, i - 1) * t, t)
                cmax_next = produce(tile, qs, dst_ref, nxt, i * ATT_SUB, False)
                return cmax_next, consume(tile, src_ref, cur, cmax, stats)
            return run

        def body(i, carry):
            return lax.cond(i % 2 == 0, step(*even), step(*odd), i, *carry)

        return lax.fori_loop(0, qt, body, (cmax, stats))

    def finish(tile, src_ref, qt, cmax, stats):
        last = pl.multiple_of(jnp.maximum(qt - 1, 0) * t, t)
        stats = consume(tile, src_ref, last, cmax, stats)
        for hh in range(ATT_HG):
            _, l = stats[hh]
            o_ref[tile_rows(tile), head_cols(hh)] = (
                acc_ref[tile, head_cols(hh), :] * (1.0 / l)).T.astype(BF16)

    def fresh_stats():
        return tuple((jnp.full((1, t), NEG_BIG, F32), jnp.zeros((1, t), F32)) for _ in range(ATT_HG))

    def diagonal(tile, qs, qt, dst_ref):
        acc_ref[tile] = jnp.zeros((ATT_HG * HEAD_DIM, t), F32)
        return produce(tile, qs, dst_ref, pl.multiple_of(qt * t, t), qt * ATT_SUB, True)

    qt0 = pair * ATT_TILES
    qs0 = prep(0, qt0)
    cmax0 = diagonal(0, qs0, qt0, sa_ref)
    cmax0, stats0 = sweep(0, qs0, qt0, (sb_ref, sa_ref), (sa_ref, sb_ref), cmax0, fresh_stats())

    qt1 = qt0 + 1
    qs1 = prep(1, qt1)
    cmax1 = diagonal(1, qs1, qt1, sb_ref)
    finish(0, sa_ref, qt0, cmax0, stats0)
    cmax1, stats1 = sweep(1, qs1, qt1, (sa_ref, sb_ref), (sb_ref, sa_ref), cmax1, fresh_stats())
    finish(1, sa_ref, qt1, cmax1, stats1)


def _rider_specs(riders, n_steps, step_of):
    specs = []
    for w in riders:
        rows = w.shape[0] // n_steps
        assert rows * n_steps == w.shape[0] and rows % 16 == 0
        specs.append(pl.BlockSpec((rows, w.shape[1]), lambda *idx: (step_of(*idx), 0)))
    return specs


def _attention(proj, kmean, faug, moba_riders, batch, seq):
    assert ATT_TILES == 2
    n = batch * seq
    npair = seq // (ATT_TILES * ATT_T)
    nblk = seq // MOBA_BLOCK
    hw = ATT_HG * HEAD_DIM
    n_hg = N_HEADS // ATT_HG
    grid = (batch, n_hg, npair)
    params = pltpu.CompilerParams(
        dimension_semantics=("arbitrary", "arbitrary", "arbitrary"),
        vmem_limit_bytes=VMEM_LIMIT)

    def q_spec(col0):
        return pl.BlockSpec((ATT_TILES * ATT_T, hw), lambda b, h, i: (b * npair + i, col0 + h))

    def kv_spec(col0):
        return pl.BlockSpec((seq, hw), lambda b, h, i: (b, col0 + h))

    region = WIDTH // hw
    out_spec = pl.BlockSpec((ATT_TILES * ATT_T, hw), lambda b, h, i: (b * npair + i, h))
    out_shape = jax.ShapeDtypeStruct((n, WIDTH), BF16)
    common = [pltpu.VMEM((hw, seq), BF16), pltpu.VMEM((ATT_TILES, hw, ATT_T), F32),
              pltpu.VMEM((ATT_HG, ATT_T, ATT_T), F32), pltpu.VMEM((ATT_HG, ATT_T, ATT_T), F32)]

    def call(is_fox, name, specs, args, riders, extra_scratch):
        rider_specs = _rider_specs(riders, batch * n_hg * npair,
                                   lambda b, h, i: (b * n_hg + h) * npair + i)
        return pl.pallas_call(
            functools.partial(_attn_kernel, is_fox, len(riders)),
            grid=grid,
            in_specs=specs + rider_specs,
            out_specs=[out_spec] + rider_specs,
            out_shape=[out_shape] + [jax.ShapeDtypeStruct(w.shape, BF16) for w in riders],
            scratch_shapes=common + extra_scratch,
            compiler_params=params,
            name=name,
        )(*args, *riders)

    o_moba, *moba_cast = call(
        False, "moba_attn",
        [q_spec(0), kv_spec(region), kv_spec(2 * region),
         pl.BlockSpec((nblk, hw), lambda b, h, i: (b, h))],
        (proj, proj, proj, kmean), moba_riders,
        [pltpu.VMEM((ATT_TILES, ATT_HG, nblk, ATT_T), F32)])
    o_fox, = call(
        True, "fox_attn",
        [q_spec(3 * region), kv_spec(4 * region), kv_spec(5 * region),
         pl.BlockSpec((seq, LANES), lambda b, h, i: (b, 0))],
        (proj, proj, proj, faug), (), [])
    return o_moba, o_fox, moba_cast


MIX_TM = 256


def _sigmoid(z):
    return 1.0 / (1.0 + jnp.exp(-z))


def _mix_kernel(n_cast, om_ref, of_ref, ga_ref, gb_ref, x_ref, wm_ref, wf_ref, wo_ref, g_ref, *rest):
    cast_src = rest[:n_cast]
    out_ref = rest[n_cast]
    cast_dst = rest[n_cast + 1:]
    for src, dst in zip(cast_src, cast_dst):
        dst[...] = src[...].astype(BF16)
    y_m = _dot(om_ref[...], wm_ref[...])
    y_f = _dot(of_ref[...], wf_ref[...])
    merged = _sigmoid(ga_ref[...].astype(F32)) * y_m + _sigmoid(gb_ref[...].astype(F32)) * y_f
    mixed = _dot(merged.astype(BF16), wo_ref[...])
    out_ref[...] = x_ref[...] + mixed * _rms_scale(mixed) * g_ref[...]


def _mix_out(o_moba, o_fox, proj, x2, wm, wf, wo, g, riders):
    n = x2.shape[0]
    ga_blk = QKV_COLS // D_MODEL
    const = dict(pipeline_mode=pl.Buffered(1))
    rider_specs = _rider_specs(riders, n // MIX_TM, lambda i: i)
    x1, *cast = pl.pallas_call(
        functools.partial(_mix_kernel, len(riders)),
        grid=(n // MIX_TM,),
        in_specs=[
            pl.BlockSpec((MIX_TM, WIDTH), lambda i: (i, 0)),
            pl.BlockSpec((MIX_TM, WIDTH), lambda i: (i, 0)),
            pl.BlockSpec((MIX_TM, D_MODEL), lambda i: (i, ga_blk)),
            pl.BlockSpec((MIX_TM, D_MODEL), lambda i: (i, ga_blk + 1)),
            pl.BlockSpec((MIX_TM, D_MODEL), lambda i: (i, 0)),
            pl.BlockSpec((WIDTH, D_MODEL), lambda i: (0, 0), **const),
            pl.BlockSpec((WIDTH, D_MODEL), lambda i: (0, 0), **const),
            pl.BlockSpec((D_MODEL, D_MODEL), lambda i: (0, 0), **const),
            pl.BlockSpec((1, D_MODEL), lambda i: (0, 0)),
        ] + rider_specs,
        out_specs=[pl.BlockSpec((MIX_TM, D_MODEL), lambda i: (i, 0))] + rider_specs,
        out_shape=[jax.ShapeDtypeStruct((n, D_MODEL), F32)]
        + [jax.ShapeDtypeStruct(w.shape, BF16) for w in riders],
        compiler_params=pltpu.CompilerParams(
            dimension_semantics=("arbitrary",), vmem_limit_bytes=VMEM_LIMIT),
        name="mix_out",
    )(o_moba, o_fox, proj, proj, x2, wm, wf, wo, g, *riders)
    return x1, cast


MLP_TM = 512
MLP_TF = 1024


def _mlp_kernel(x_ref, gpre_ref, wu_ref, wd_ref, gpost_ref, out_ref, h_ref, acc_ref):
    f = pl.program_id(1)

    @pl.when(f == 0)
    def _():
        x = x_ref[...]
        h_ref[...] = (x * _rms_scale(x) * gpre_ref[...]).astype(BF16)
        acc_ref[...] = jnp.zeros_like(acc_ref)

    u = _dot(h_ref[...], wu_ref[...])
    a = jnp.square(jnp.maximum(u, 0.0)).astype(BF16)
    acc_ref[...] += _dot(a, wd_ref[...])

    @pl.when(f == pl.num_programs(1) - 1)
    def _():
        mo = acc_ref[...]
        out_ref[...] = x_ref[...] + mo * _rms_scale(mo) * gpost_ref[...]


def _mlp(x1, gpre, wu, wd, gpost):
    n = x1.shape[0]
    return pl.pallas_call(
        _mlp_kernel,
        grid=(n // MLP_TM, D_FF // MLP_TF),
        in_specs=[
            pl.BlockSpec((MLP_TM, D_MODEL), lambda i, f: (i, 0)),
            pl.BlockSpec((1, D_MODEL), lambda i, f: (0, 0)),
            pl.BlockSpec((D_MODEL, MLP_TF), lambda i, f: (0, f)),
            pl.BlockSpec((MLP_TF, D_MODEL), lambda i, f: (f, 0)),
            pl.BlockSpec((1, D_MODEL), lambda i, f: (0, 0)),
        ],
        out_specs=pl.BlockSpec((MLP_TM, D_MODEL), lambda i, f: (i, 0)),
        out_shape=jax.ShapeDtypeStruct((n, D_MODEL), F32),
        scratch_shapes=[pltpu.VMEM((MLP_TM, D_MODEL), BF16), pltpu.VMEM((MLP_TM, D_MODEL), F32)],
        compiler_params=pltpu.CompilerParams(
            dimension_semantics=("arbitrary", "arbitrary"), vmem_limit_bytes=VMEM_LIMIT),
        name="mlp",
    )(x1, gpre, wu, wd, gpost)


def _rope_tables(seq):
    inv_freq = ROPE_THETA ** (-jnp.arange(0, ROPE_DIM, 2, dtype=F32) / ROPE_DIM)
    ang = jnp.arange(seq, dtype=F32)[:, None] * inv_freq[None, :]
    cos, sin = jnp.cos(ang), jnp.sin(ang)
    gap = HEAD_DIM // 2 - ROPE_HALF
    ones = jnp.ones((seq, gap), F32)
    zeros = jnp.zeros((seq, gap), F32)
    cos_t = jnp.concatenate([cos, ones, cos, ones], axis=1)
    sin_t = jnp.concatenate([-sin, zeros, sin, zeros], axis=1)
    return cos_t, sin_t


def _layer(x2, w_in, b_forget, w_bm, w_bf, w_out, g_mix_pre, g_mix_post,
           w_up, w_down, g_mlp_pre, g_mlp_post, batch, seq):
    b_row = jnp.pad(b_forget, (0, LANES - N_HEADS)).reshape(1, LANES)
    cos_t, sin_t = _rope_tables(seq)

    proj, ff, kmean = _in_proj(x2, g_mix_pre.reshape(1, -1), w_in, cos_t, sin_t, seq)
    faug = _fox_bias(ff, b_row, batch, seq)
    kmean = kmean.reshape(batch * (seq // MOBA_BLOCK), WIDTH)
    o_moba, o_fox, (wm, wf, wo) = _attention(proj, kmean, faug, (w_bm, w_bf, w_out), batch, seq)
    x1, (wu, wd) = _mix_out(o_moba, o_fox, proj, x2, wm, wf, wo, g_mix_post.reshape(1, -1),
                            (w_up, w_down))
    return _mlp(x1, g_mlp_pre.reshape(1, -1), wu, wd, g_mlp_post.reshape(1, -1))


def kernel(x, w_in, b_forget, w_branch_moba, w_branch_fox, w_out, g_mix_pre, g_mix_post,
           w_up, w_down, g_mlp_pre, g_mlp_post):
    batch, seq, d = x.shape
    assert d == D_MODEL and seq % IN_TM == 0 and seq % (ATT_TILES * ATT_T) == 0
    assert w_in.shape[-1] == PROJ_COLS + N_HEADS
    x2 = x.reshape(batch * seq, d)
    for l in range(w_in.shape[0]):
        x2 = _layer(x2, w_in[l], b_forget[l], w_branch_moba[l], w_branch_fox[l], w_out[l],
                    g_mix_pre[l], g_mix_post[l], w_up[l], w_down[l], g_mlp_pre[l],
                    g_mlp_post[l], batch, seq)
    return x2.reshape(batch, seq, d)
```

```python
import functools
import math

import jax
import jax.numpy as jnp
from jax import lax
from jax.experimental import pallas as pl
from jax.experimental.pallas import tpu as pltpu

D_MODEL = 2048
HEAD_DIM = 128
N_HEADS = 8
WIDTH = N_HEADS * HEAD_DIM
MOBA_BLOCK = 256
MOBA_TOP_K = 3
ROPE_THETA = 500000.0
ROPE_DIM = HEAD_DIM // 4
ROPE_HALF = ROPE_DIM // 2
D_FF = 4 * D_MODEL
RMS_EPS = 1e-6
NEG_BIG = -1e30
LANES = 128
SUBLANES = 8

QKV_COLS = 6 * WIDTH
GATE_COLS = 2 * D_MODEL
PROJ_COLS = QKV_COLS + GATE_COLS

VMEM_LIMIT = 56 * 1024 * 1024

BF16 = jnp.bfloat16
F32 = jnp.float32
LOG2E = math.log2(math.e)
Q_SCALE = HEAD_DIM ** -0.5 * LOG2E


def _rms_scale(x):
    return lax.rsqrt(jnp.mean(x * x, axis=-1, keepdims=True) + RMS_EPS)


def _split3(x):
    hi = x.astype(BF16)
    r1 = x - hi.astype(F32)
    mid = r1.astype(BF16)
    lo = (r1 - mid.astype(F32)).astype(BF16)
    return hi, mid, lo


def _dot(a, b):
    return jnp.dot(a, b, preferred_element_type=F32)


def _dot_nt(a, b):
    return lax.dot_general(a, b, (((1,), (1,)), ((), ())), preferred_element_type=F32)


IN_TM = 1024
IN_TN = 1024
IN_ROT_TILES = 2 * WIDTH // IN_TN


def _permute_rotary_rows(blk):
    mid = HEAD_DIM // 2 + ROPE_HALF
    parts = []
    for hd in range(blk.shape[0] // HEAD_DIM):
        b = hd * HEAD_DIM
        parts += [blk[b:b + ROPE_HALF], blk[b + ROPE_DIM:b + mid],
                  blk[b + ROPE_HALF:b + ROPE_DIM], blk[b + mid:b + HEAD_DIM]]
    return jnp.concatenate(parts, axis=0)


def _in_proj_kernel(x_ref, g_ref, w_ref, wff_ref, cos_ref, sin_ref,
                    proj_ref, ff_ref, kmean_ref, h_ref):
    j = pl.program_id(1)

    @pl.when(j == 0)
    def _():
        x = x_ref[...]
        h = (x * _rms_scale(x) * g_ref[...]).astype(BF16)
        h_ref[...] = h
        ff_ref[...] = _dot_nt(h, wff_ref[...].astype(BF16))

    def tile(permute):
        w = w_ref[...]
        w = (_permute_rotary_rows(w) if permute else w).astype(BF16)
        return _dot_nt(h_ref[...], w)

    def rotary(t):
        outs = []
        for hd in range(IN_TN // HEAD_DIM):
            th = t[:, hd * HEAD_DIM:(hd + 1) * HEAD_DIM]
            outs.append(th * cos_ref[...] + pltpu.roll(th, HEAD_DIM // 2, 1) * sin_ref[...])
        return jnp.concatenate(outs, axis=1)

    @pl.when(j == 0)
    def _():
        proj_ref[...] = (rotary(tile(True)) * Q_SCALE).astype(BF16)

    @pl.when(j == 1)
    def _():
        kr = rotary(tile(True))
        proj_ref[...] = kr.astype(BF16)
        nblk = IN_TM // MOBA_BLOCK
        means = [jnp.mean(kr[b * MOBA_BLOCK:(b + 1) * MOBA_BLOCK, :], axis=0, keepdims=True)
                 for b in range(nblk)]
        kmean_ref[0] = jnp.concatenate(means, axis=0)

    @pl.when(j == 3)
    def _():
        proj_ref[...] = (tile(False) * Q_SCALE).astype(BF16)

    @pl.when((j == 2) | (j >= 4))
    def _():
        proj_ref[...] = tile(False).astype(BF16)


def _in_proj(x2, g, w_in, cos_t, sin_t, seq):
    assert IN_ROT_TILES == 2
    n = x2.shape[0]
    seq_tiles = seq // IN_TM
    n_qkv = QKV_COLS // IN_TN
    gate0 = QKV_COLS + N_HEADS
    assert gate0 % SUBLANES == 0
    wt = w_in.T

    def src_row(j):
        return pl.multiple_of(jnp.where(j < n_qkv, j * IN_TN, gate0 + (j - n_qkv) * IN_TN), SUBLANES)

    return pl.pallas_call(
        _in_proj_kernel,
        grid=(n // IN_TM, PROJ_COLS // IN_TN),
        in_specs=[
            pl.BlockSpec((IN_TM, D_MODEL), lambda i, j: (i, 0)),
            pl.BlockSpec((1, D_MODEL), lambda i, j: (0, 0)),
            pl.BlockSpec((pl.Element(IN_TN), pl.Element(D_MODEL)), lambda i, j: (src_row(j), 0)),
            pl.BlockSpec((LANES, D_MODEL), lambda i, j: (QKV_COLS // LANES, 0),
                         pipeline_mode=pl.Buffered(1)),
            pl.BlockSpec((IN_TM, HEAD_DIM), lambda i, j: (i % seq_tiles, 0)),
            pl.BlockSpec((IN_TM, HEAD_DIM), lambda i, j: (i % seq_tiles, 0)),
        ],
        out_specs=[
            pl.BlockSpec((IN_TM, IN_TN), lambda i, j: (i, j)),
            pl.BlockSpec((IN_TM, LANES), lambda i, j: (i, 0)),
            pl.BlockSpec((1, IN_TM // MOBA_BLOCK, WIDTH), lambda i, j: (i, 0, 0)),
        ],
        out_shape=[
            jax.ShapeDtypeStruct((n, PROJ_COLS), BF16),
            jax.ShapeDtypeStruct((n, LANES), F32),
            jax.ShapeDtypeStruct((n // IN_TM, IN_TM // MOBA_BLOCK, WIDTH), F32),
        ],
        scratch_shapes=[pltpu.VMEM((IN_TM, D_MODEL), BF16)],
        compiler_params=pltpu.CompilerParams(
            dimension_semantics=("arbitrary", "arbitrary"), vmem_limit_bytes=VMEM_LIMIT),
        name="in_proj",
    )(x2, g, wt, wt, cos_t, sin_t)


FB_CHUNK = 256


def _fox_bias_kernel(ff_ref, b_ref, out_ref):
    seq = ff_ref.shape[0]
    r = lax.broadcasted_iota(jnp.int32, (FB_CHUNK, FB_CHUNK), 0)
    c = lax.broadcasted_iota(jnp.int32, (FB_CHUNK, FB_CHUNK), 1)
    ltri = (c <= r).astype(BF16)
    pr = lax.broadcasted_iota(jnp.int32, (LANES, LANES), 0)
    pc = lax.broadcasted_iota(jnp.int32, (LANES, LANES), 1)
    place = [((pc == 3 * pr + t) & (pr < N_HEADS)).astype(BF16) for t in range(3)]

    local = []
    for ci in range(seq // FB_CHUNK):
        z = ff_ref[ci * FB_CHUNK:(ci + 1) * FB_CHUNK, :] + b_ref[...]
        logf = -(jnp.maximum(-z, 0.0) + jnp.log1p(jnp.exp(-jnp.abs(z))))
        hi, mid, lo = _split3(logf)
        local.append(_dot(ltri, hi) + _dot(ltri, mid) + _dot(ltri, lo))
    carry = jnp.zeros((1, LANES), F32)
    for ci, loc in enumerate(local):
        f = loc + carry
        carry = carry + loc[FB_CHUNK - 1:FB_CHUNK, :]
        nh, nm, nl = _split3(f * (-LOG2E))
        aug = _dot(nh, place[0]) + _dot(nm, place[1]) + _dot(nl, place[2])
        out_ref[ci * FB_CHUNK:(ci + 1) * FB_CHUNK, :] = aug.astype(BF16)


def _fox_bias(ff, b_row, batch, seq):
    return pl.pallas_call(
        _fox_bias_kernel,
        grid=(batch,),
        in_specs=[
            pl.BlockSpec((seq, LANES), lambda b: (b, 0)),
            pl.BlockSpec((1, LANES), lambda b: (0, 0)),
        ],
        out_specs=pl.BlockSpec((seq, LANES), lambda b: (b, 0)),
        out_shape=jax.ShapeDtypeStruct((batch * seq, LANES), BF16),
        compiler_params=pltpu.CompilerParams(
            dimension_semantics=("arbitrary",), vmem_limit_bytes=VMEM_LIMIT),
        name="fox_bias",
    )(ff, b_row)


ATT_T = 512
ATT_HG = 4
ATT_TILES = 2
ATT_SUB = ATT_T // MOBA_BLOCK


def _attn_kernel(is_fox, n_cast, *refs):
    q_ref, k_ref, v_ref, x_ref = refs[:4]
    cast_src = refs[4:4 + n_cast]
    o_ref = refs[4 + n_cast]
    cast_dst = refs[5 + n_cast:5 + 2 * n_cast]
    vt_ref, acc_ref, sa_ref, sb_ref = refs[5 + 2 * n_cast:]
    hg = pl.program_id(1)
    pair = pl.program_id(2)
    seq = k_ref.shape[0]
    nblk = seq // MOBA_BLOCK
    t = ATT_T

    for src, dst in zip(cast_src, cast_dst):
        dst[...] = src[...].astype(BF16)

    def head_cols(hh):
        return slice(hh * HEAD_DIM, (hh + 1) * HEAD_DIM)

    def tile_rows(tile):
        return slice(tile * t, (tile + 1) * t)

    @pl.when(pair == 0)
    def _():
        for hh in range(ATT_HG):
            for ci in range(nblk):
                rows = slice(ci * MOBA_BLOCK, (ci + 1) * MOBA_BLOCK)
                blk = v_ref[rows, head_cols(hh)].astype(F32)
                vt_ref[head_cols(hh), rows] = blk.T.astype(BF16)

    def prep(tile, qt):
        qs = []
        for hh in range(ATT_HG):
            qh = q_ref[tile_rows(tile), head_cols(hh)]
            if is_fox:
                head = hg * ATT_HG + hh
                lane = lax.broadcasted_iota(jnp.int32, (t, LANES), 1)
                pick = ((lane >= 3 * head) & (lane < 3 * head + 3)).astype(BF16)
                qh = jnp.concatenate([qh, pick], axis=1)
            qs.append(qh)
        if not is_fox:
            blk_id = lax.broadcasted_iota(jnp.int32, (nblk, t), 0)
            q_blk = qt * ATT_SUB + lax.broadcasted_iota(jnp.int32, (nblk, t), 1) // MOBA_BLOCK
            past = blk_id < q_blk
            for hh in range(ATT_HG):
                kh, km, kl = _split3(x_ref[:, head_cols(hh)])
                gate = _dot_nt(kh, qs[hh]) + _dot_nt(km, qs[hh]) + _dot_nt(kl, qs[hh])
                gate = jnp.where(past, gate, -jnp.inf)
                rank = jnp.zeros((nblk, t), jnp.int32)
                for mth in range(nblk):
                    gm = gate[mth:mth + 1, :]
                    ahead = (gm > gate) | ((gm == gate) & (mth < blk_id))
                    rank = rank + ahead.astype(jnp.int32)
                visible = ((rank < MOBA_TOP_K) & past) | (blk_id == q_blk)
                bias = jnp.where(visible, 0.0, NEG_BIG)
                bias = jnp.concatenate([bias, jnp.zeros((LANES - nblk, t), F32)], axis=0)
                qs[hh] = jnp.concatenate([qs[hh], bias.T.astype(BF16)], axis=1)
        return qs

    def scores(qs, hh, start):
        kb = k_ref[pl.ds(start, t), head_cols(hh)]
        if is_fox:
            extra = x_ref[pl.ds(start, t), :]
        else:
            key_blk = start // MOBA_BLOCK + lax.broadcasted_iota(jnp.int32, (t, LANES), 0) // MOBA_BLOCK
            extra = (lax.broadcasted_iota(jnp.int32, (t, LANES), 1) == key_blk).astype(BF16)
        return _dot_nt(jnp.concatenate([kb, extra], axis=1), qs[hh])

    def vt_chunk(hh, start):
        return vt_ref[head_cols(hh), pl.ds(start, t)]

    def produce(qs, dst_ref, start, diagonal):
        cmax = []
        for hh in range(ATT_HG):
            s = scores(qs, hh, start)
            if diagonal:
                key = lax.broadcasted_iota(jnp.int32, (t, t), 0)
                qry = lax.broadcasted_iota(jnp.int32, (t, t), 1)
                s = jnp.where(key <= qry, s, NEG_BIG)
            dst_ref[hh] = s
            cmax.append(jnp.max(s, axis=0, keepdims=True))
        return tuple(cmax)

    def consume(tile, src_ref, start, cmax, stats):
        new = []
        for hh in range(ATT_HG):
            m, l = stats[hh]
            m_new = jnp.maximum(m, cmax[hh])
            alpha = jnp.exp2(m - m_new)
            p = jnp.exp2(src_ref[hh] - m_new)
            l = alpha * l + jnp.sum(p, axis=0, keepdims=True)
            acc_ref[tile, head_cols(hh), :] = (alpha * acc_ref[tile, head_cols(hh), :]
                                               + _dot(vt_chunk(hh, start), p.astype(BF16)))
            new.append((m_new, l))
        return tuple(new)

    def sweep(tile, qs, qt, even, odd, cmax, stats):
        def step(dst_ref, src_ref):
            def run(i, cmax, stats):
                nxt = pl.multiple_of(i * t, t)
                cur = pl.multiple_of(jnp.where(i == 0, qt, i - 1) * t, t)
                cmax_next = produce(qs, dst_ref, nxt, False)
                return cmax_next, consume(tile, src_ref, cur, cmax, stats)
            return run

        def body(i, carry):
            return lax.cond(i % 2 == 0, step(*even), step(*odd), i, *carry)

        return lax.fori_loop(0, qt, body, (cmax, stats))

    def finish(tile, src_ref, qt, cmax, stats):
        last = pl.multiple_of(jnp.maximum(qt - 1, 0) * t, t)
        stats = consume(tile, src_ref, last, cmax, stats)
        for hh in range(ATT_HG):
            _, l = stats[hh]
            o_ref[tile_rows(tile), head_cols(hh)] = (
                acc_ref[tile, head_cols(hh), :] * (1.0 / l)).T.astype(BF16)

    def fresh_stats():
        return tuple((jnp.full((1, t), NEG_BIG, F32), jnp.zeros((1, t), F32)) for _ in range(ATT_HG))

    def diagonal(tile, qs, qt, dst_ref):
        acc_ref[tile] = jnp.zeros((ATT_HG * HEAD_DIM, t), F32)
        return produce(qs, dst_ref, pl.multiple_of(qt * t, t), True)

    qt0 = pair * ATT_TILES
    qs0 = prep(0, qt0)
    cmax0 = diagonal(0, qs0, qt0, sa_ref)
    cmax0, stats0 = sweep(0, qs0, qt0, (sb_ref, sa_ref), (sa_ref, sb_ref), cmax0, fresh_stats())

    qt1 = qt0 + 1
    qs1 = prep(1, qt1)
    cmax1 = diagonal(1, qs1, qt1, sb_ref)
    finish(0, sa_ref, qt0, cmax0, stats0)
    cmax1, stats1 = sweep(1, qs1, qt1, (sa_ref, sb_ref), (sb_ref, sa_ref), cmax1, fresh_stats())
    finish(1, sa_ref, qt1, cmax1, stats1)


def _rider_specs(riders, n_steps, step_of):
    specs = []
    for w in riders:
        rows = w.shape[0] // n_steps
        assert rows * n_steps == w.shape[0] and rows % 16 == 0
        specs.append(pl.BlockSpec((rows, w.shape[1]), lambda *idx: (step_of(*idx), 0)))
    return specs


def _attention(proj, kmean, faug, moba_riders, batch, seq):
    assert ATT_TILES == 2
    n = batch * seq
    npair = seq // (ATT_TILES * ATT_T)
    nblk = seq // MOBA_BLOCK
    hw = ATT_HG * HEAD_DIM
    n_hg = N_HEADS // ATT_HG
    grid = (batch, n_hg, npair)
    params = pltpu.CompilerParams(
        dimension_semantics=("arbitrary", "arbitrary", "arbitrary"),
        vmem_limit_bytes=VMEM_LIMIT)

    def q_spec(col0):
        return pl.BlockSpec((ATT_TILES * ATT_T, hw), lambda b, h, i: (b * npair + i, col0 + h))

    def kv_spec(col0):
        return pl.BlockSpec((seq, hw), lambda b, h, i: (b, col0 + h))

    region = WIDTH // hw
    out_spec = pl.BlockSpec((ATT_TILES * ATT_T, hw), lambda b, h, i: (b * npair + i, h))
    out_shape = jax.ShapeDtypeStruct((n, WIDTH), BF16)
    common = [pltpu.VMEM((hw, seq), BF16), pltpu.VMEM((ATT_TILES, hw, ATT_T), F32),
              pltpu.VMEM((ATT_HG, ATT_T, ATT_T), F32), pltpu.VMEM((ATT_HG, ATT_T, ATT_T), F32)]

    def call(is_fox, name, specs, args, riders, extra_scratch):
        rider_specs = _rider_specs(riders, batch * n_hg * npair,
                                   lambda b, h, i: (b * n_hg + h) * npair + i)
        return pl.pallas_call(
            functools.partial(_attn_kernel, is_fox, len(riders)),
            grid=grid,
            in_specs=specs + rider_specs,
            out_specs=[out_spec] + rider_specs,
            out_shape=[out_shape] + [jax.ShapeDtypeStruct(w.shape, BF16) for w in riders],
            scratch_shapes=common + extra_scratch,
            compiler_params=params,
            name=name,
        )(*args, *riders)

    o_moba, *moba_cast = call(
        False, "moba_attn",
        [q_spec(0), kv_spec(region), kv_spec(2 * region),
         pl.BlockSpec((nblk, hw), lambda b, h, i: (b, h))],
        (proj, proj, proj, kmean), moba_riders, [])
    o_fox, = call(
        True, "fox_attn",
        [q_spec(3 * region), kv_spec(4 * region), kv_spec(5 * region),
         pl.BlockSpec((seq, LANES), lambda b, h, i: (b, 0))],
        (proj, proj, proj, faug), (), [])
    return o_moba, o_fox, moba_cast


MIX_TM = 256


def _sigmoid(z):
    return 1.0 / (1.0 + jnp.exp(-z))


def _mix_kernel(n_cast, om_ref, of_ref, ga_ref, gb_ref, x_ref, wm_ref, wf_ref, wo_ref, g_ref, *rest):
    cast_src = rest[:n_cast]
    out_ref = rest[n_cast]
    cast_dst = rest[n_cast + 1:]
    for src, dst in zip(cast_src, cast_dst):
        dst[...] = src[...].astype(BF16)
    y_m = _dot(om_ref[...], wm_ref[...])
    y_f = _dot(of_ref[...], wf_ref[...])
    merged = _sigmoid(ga_ref[...].astype(F32)) * y_m + _sigmoid(gb_ref[...].astype(F32)) * y_f
    mixed = _dot(merged.astype(BF16), wo_ref[...])
    out_ref[...] = x_ref[...] + mixed * _rms_scale(mixed) * g_ref[...]


def _mix_out(o_moba, o_fox, proj, x2, wm, wf, wo, g, riders):
    n = x2.shape[0]
    ga_blk = QKV_COLS // D_MODEL
    const = dict(pipeline_mode=pl.Buffered(1))
    rider_specs = _rider_specs(riders, n // MIX_TM, lambda i: i)
    x1, *cast = pl.pallas_call(
        functools.partial(_mix_kernel, len(riders)),
        grid=(n // MIX_TM,),
        in_specs=[
            pl.BlockSpec((MIX_TM, WIDTH), lambda i: (i, 0)),
            pl.BlockSpec((MIX_TM, WIDTH), lambda i: (i, 0)),
            pl.BlockSpec((MIX_TM, D_MODEL), lambda i: (i, ga_blk)),
            pl.BlockSpec((MIX_TM, D_MODEL), lambda i: (i, ga_blk + 1)),
            pl.BlockSpec((MIX_TM, D_MODEL), lambda i: (i, 0)),
            pl.BlockSpec((WIDTH, D_MODEL), lambda i: (0, 0), **const),
            pl.BlockSpec((WIDTH, D_MODEL), lambda i: (0, 0), **const),
            pl.BlockSpec((D_MODEL, D_MODEL), lambda i: (0, 0), **const),
            pl.BlockSpec((1, D_MODEL), lambda i: (0, 0)),
        ] + rider_specs,
        out_specs=[pl.BlockSpec((MIX_TM, D_MODEL), lambda i: (i, 0))] + rider_specs,
        out_shape=[jax.ShapeDtypeStruct((n, D_MODEL), F32)]
        + [jax.ShapeDtypeStruct(w.shape, BF16) for w in riders],
        compiler_params=pltpu.CompilerParams(
            dimension_semantics=("arbitrary",), vmem_limit_bytes=VMEM_LIMIT),
        name="mix_out",
    )(o_moba, o_fox, proj, proj, x2, wm, wf, wo, g, *riders)
    return x1, cast


MLP_TM = 512
MLP_TF = 1024


def _mlp_kernel(x_ref, gpre_ref, wu_ref, wd_ref, gpost_ref, out_ref, h_ref, acc_ref):
    f = pl.program_id(1)

    @pl.when(f == 0)
    def _():
        x = x_ref[...]
        h_ref[...] = (x * _rms_scale(x) * gpre_ref[...]).astype(BF16)
        acc_ref[...] = jnp.zeros_like(acc_ref)

    u = _dot(h_ref[...], wu_ref[...])
    a = jnp.square(jnp.maximum(u, 0.0)).astype(BF16)
    acc_ref[...] += _dot(a, wd_ref[...])

    @pl.when(f == pl.num_programs(1) - 1)
    def _():
        mo = acc_ref[...]
        out_ref[...] = x_ref[...] + mo * _rms_scale(mo) * gpost_ref[...]


def _mlp(x1, gpre, wu, wd, gpost):
    n = x1.shape[0]
    return pl.pallas_call(
        _mlp_kernel,
        grid=(n // MLP_TM, D_FF // MLP_TF),
        in_specs=[
            pl.BlockSpec((MLP_TM, D_MODEL), lambda i, f: (i, 0)),
            pl.BlockSpec((1, D_MODEL), lambda i, f: (0, 0)),
            pl.BlockSpec((D_MODEL, MLP_TF), lambda i, f: (0, f)),
            pl.BlockSpec((MLP_TF, D_MODEL), lambda i, f: (f, 0)),
            pl.BlockSpec((1, D_MODEL), lambda i, f: (0, 0)),
        ],
        out_specs=pl.BlockSpec((MLP_TM, D_MODEL), lambda i, f: (i, 0)),
        out_shape=jax.ShapeDtypeStruct((n, D_MODEL), F32),
        scratch_shapes=[pltpu.VMEM((MLP_TM, D_MODEL), BF16), pltpu.VMEM((MLP_TM, D_MODEL), F32)],
        compiler_params=pltpu.CompilerParams(
            dimension_semantics=("arbitrary", "arbitrary"), vmem_limit_bytes=VMEM_LIMIT),
        name="mlp",
    )(x1, gpre, wu, wd, gpost)


def _rope_tables(seq):
    inv_freq = ROPE_THETA ** (-jnp.arange(0, ROPE_DIM, 2, dtype=F32) / ROPE_DIM)
    ang = jnp.arange(seq, dtype=F32)[:, None] * inv_freq[None, :]
    cos, sin = jnp.cos(ang), jnp.sin(ang)
    gap = HEAD_DIM // 2 - ROPE_HALF
    ones = jnp.ones((seq, gap), F32)
    zeros = jnp.zeros((seq, gap), F32)
    cos_t = jnp.concatenate([cos, ones, cos, ones], axis=1)
    sin_t = jnp.concatenate([-sin, zeros, sin, zeros], axis=1)
    return cos_t, sin_t


def _layer(x2, w_in, b_forget, w_bm, w_bf, w_out, g_mix_pre, g_mix_post,
           w_up, w_down, g_mlp_pre, g_mlp_post, batch, seq):
    b_row = jnp.pad(b_forget, (0, LANES - N_HEADS)).reshape(1, LANES)
    cos_t, sin_t = _rope_tables(seq)

    proj, ff, kmean = _in_proj(x2, g_mix_pre.reshape(1, -1), w_in, cos_t, sin_t, seq)
    faug = _fox_bias(ff, b_row, batch, seq)
    kmean = kmean.reshape(batch * (seq // MOBA_BLOCK), WIDTH)
    o_moba, o_fox, (wm, wf, wo) = _attention(proj, kmean, faug, (w_bm, w_bf, w_out), batch, seq)
    x1, (wu, wd) = _mix_out(o_moba, o_fox, proj, x2, wm, wf, wo, g_mix_post.reshape(1, -1),
                            (w_up, w_down))
    return _mlp(x1, g_mlp_pre.reshape(1, -1), wu, wd, g_mlp_post.reshape(1, -1))


def kernel(x, w_in, b_forget, w_branch_moba, w_branch_fox, w_out, g_mix_pre, g_mix_post,
           w_up, w_down, g_mlp_pre, g_mlp_post):
    batch, seq, d = x.shape
    assert d == D_MODEL and seq % IN_TM == 0 and seq % (ATT_TILES * ATT_T) == 0
    assert w_in.shape[-1] == PROJ_COLS + N_HEADS
    x2 = x.reshape(batch * seq, d)
    for l in range(w_in.shape[0]):
        x2 = _layer(x2, w_in[l], b_forget[l], w_branch_moba[l], w_branch_fox[l], w_out[l],
                    g_mix_pre[l], g_mix_post[l], w_up[l], w_down[l], g_mlp_pre[l],
                    g_mlp_post[l], batch, seq)
    return x2.reshape(batch, seq, d)
```

```python
import functools
import math

import jax
import jax.numpy as jnp
from jax import lax
from jax.experimental import pallas as pl
from jax.experimental.pallas import tpu as pltpu

D_MODEL = 2048
HEAD_DIM = 128
N_HEADS = 8
WIDTH = N_HEADS * HEAD_DIM
MOBA_BLOCK = 256
MOBA_TOP_K = 3
ROPE_THETA = 500000.0
ROPE_DIM = HEAD_DIM // 4
ROPE_HALF = ROPE_DIM // 2
D_FF = 4 * D_MODEL
RMS_EPS = 1e-6
NEG_BIG = -1e30
LANES = 128
SUBLANES = 8

QKV_COLS = 6 * WIDTH
GATE_COLS = 2 * D_MODEL
PROJ_COLS = QKV_COLS + GATE_COLS

VMEM_LIMIT = 56 * 1024 * 1024

BF16 = jnp.bfloat16
F32 = jnp.float32
LOG2E = math.log2(math.e)
Q_SCALE = HEAD_DIM ** -0.5 * LOG2E


def _rms_scale(x):
    return lax.rsqrt(jnp.mean(x * x, axis=-1, keepdims=True) + RMS_EPS)


def _split3(x):
    hi = x.astype(BF16)
    r1 = x - hi.astype(F32)
    mid = r1.astype(BF16)
    lo = (r1 - mid.astype(F32)).astype(BF16)
    return hi, mid, lo


def _dot(a, b):
    return jnp.dot(a, b, preferred_element_type=F32)


def _dot_nt(a, b):
    return lax.dot_general(a, b, (((1,), (1,)), ((), ())), preferred_element_type=F32)


IN_TM = 1024
IN_TN = 1024
IN_ROT_TILES = 2 * WIDTH // IN_TN


def _permute_rotary_rows(blk):
    mid = HEAD_DIM // 2 + ROPE_HALF
    parts = []
    for hd in range(blk.shape[0] // HEAD_DIM):
        b = hd * HEAD_DIM
        parts += [blk[b:b + ROPE_HALF], blk[b + ROPE_DIM:b + mid],
                  blk[b + ROPE_HALF:b + ROPE_DIM], blk[b + mid:b + HEAD_DIM]]
    return jnp.concatenate(parts, axis=0)


IN_W_SLOTS = 3


def _in_proj_kernel(x_hbm, g_ref, wt_hbm, wff_ref, cos_ref, sin_ref,
                    proj_ref, ff_ref, kmean_ref, h_ref, xbuf_ref, wbuf_ref, xsem, wsem):
    i = pl.program_id(0)
    j = pl.program_id(1)
    n_j = pl.num_programs(1)
    step = i * n_j + j
    n_steps = pl.num_programs(0) * n_j

    def x_rows(r):
        return pltpu.make_async_copy(x_hbm.at[pl.ds(pl.multiple_of(r * IN_TM, IN_TM), IN_TM), :],
                                     xbuf_ref, xsem.at[0])

    def window(s):
        col = s % n_j
        n_qkv = QKV_COLS // IN_TN
        row = pl.multiple_of(
            jnp.where(col < n_qkv, col * IN_TN, QKV_COLS + N_HEADS + (col - n_qkv) * IN_TN), SUBLANES)
        slot = s % IN_W_SLOTS
        return pltpu.make_async_copy(wt_hbm.at[pl.ds(row, IN_TN), :], wbuf_ref.at[slot],
                                     wsem.at[slot])

    @pl.when(step == 0)
    def _():
        x_rows(0).start()
        for s in range(IN_W_SLOTS - 1):
            window(s).start()

    @pl.when(step + IN_W_SLOTS - 1 < n_steps)
    def _():
        window(step + IN_W_SLOTS - 1).start()

    @pl.when((j == 1) & (i + 1 < pl.num_programs(0)))
    def _():
        x_rows(i + 1).start()

    @pl.when(j == 0)
    def _():
        x_rows(i).wait()
        x = xbuf_ref[...]
        h = (x * _rms_scale(x) * g_ref[...]).astype(BF16)
        h_ref[...] = h
        ff_ref[...] = _dot_nt(h, wff_ref[...].astype(BF16))

    window(step).wait()
    w_ref = wbuf_ref.at[step % IN_W_SLOTS]

    def tile(permute):
        w = w_ref[...]
        w = (_permute_rotary_rows(w) if permute else w).astype(BF16)
        return _dot_nt(h_ref[...], w)

    def rotary(t):
        outs = []
        for hd in range(IN_TN // HEAD_DIM):
            th = t[:, hd * HEAD_DIM:(hd + 1) * HEAD_DIM]
            outs.append(th * cos_ref[...] + pltpu.roll(th, HEAD_DIM // 2, 1) * sin_ref[...])
        return jnp.concatenate(outs, axis=1)

    @pl.when(j == 0)
    def _():
        proj_ref[...] = (rotary(tile(True)) * Q_SCALE).astype(BF16)

    @pl.when(j == 1)
    def _():
        kr = rotary(tile(True))
        proj_ref[...] = kr.astype(BF16)
        nblk = IN_TM // MOBA_BLOCK
        means = [jnp.mean(kr[b * MOBA_BLOCK:(b + 1) * MOBA_BLOCK, :], axis=0, keepdims=True)
                 for b in range(nblk)]
        kmean_ref[0] = jnp.concatenate(means, axis=0)

    @pl.when(j == 3)
    def _():
        proj_ref[...] = (tile(False) * Q_SCALE).astype(BF16)

    @pl.when((j == 2) | (j >= 4))
    def _():
        proj_ref[...] = tile(False).astype(BF16)


def _in_proj(x2, g, w_in, cos_t, sin_t, seq):
    assert IN_ROT_TILES == 2
    n = x2.shape[0]
    seq_tiles = seq // IN_TM
    assert (QKV_COLS + N_HEADS) % SUBLANES == 0 and n // IN_TM * (PROJ_COLS // IN_TN) >= IN_W_SLOTS
    wt = w_in.T

    return pl.pallas_call(
        _in_proj_kernel,
        grid=(n // IN_TM, PROJ_COLS // IN_TN),
        in_specs=[
            pl.BlockSpec(memory_space=pl.ANY),
            pl.BlockSpec((1, D_MODEL), lambda i, j: (0, 0)),
            pl.BlockSpec(memory_space=pl.ANY),
            pl.BlockSpec((LANES, D_MODEL), lambda i, j: (QKV_COLS // LANES, 0),
                         pipeline_mode=pl.Buffered(1)),
            pl.BlockSpec((IN_TM, HEAD_DIM), lambda i, j: (i % seq_tiles, 0)),
            pl.BlockSpec((IN_TM, HEAD_DIM), lambda i, j: (i % seq_tiles, 0)),
        ],
        out_specs=[
            pl.BlockSpec((IN_TM, IN_TN), lambda i, j: (i, j)),
            pl.BlockSpec((IN_TM, LANES), lambda i, j: (i, 0)),
            pl.BlockSpec((1, IN_TM // MOBA_BLOCK, WIDTH), lambda i, j: (i, 0, 0)),
        ],
        out_shape=[
            jax.ShapeDtypeStruct((n, PROJ_COLS), BF16),
            jax.ShapeDtypeStruct((n, LANES), F32),
            jax.ShapeDtypeStruct((n // IN_TM, IN_TM // MOBA_BLOCK, WIDTH), F32),
        ],
        scratch_shapes=[pltpu.VMEM((IN_TM, D_MODEL), BF16),
                        pltpu.VMEM((IN_TM, D_MODEL), F32),
                        pltpu.VMEM((IN_W_SLOTS, IN_TN, D_MODEL), F32),
                        pltpu.SemaphoreType.DMA((1,)),
                        pltpu.SemaphoreType.DMA((IN_W_SLOTS,))],
        compiler_params=pltpu.CompilerParams(
            dimension_semantics=("arbitrary", "arbitrary"), vmem_limit_bytes=VMEM_LIMIT),
        name="in_proj",
    )(x2, g, wt, wt, cos_t, sin_t)


FB_CHUNK = 256


def _fox_bias_kernel(ff_ref, b_ref, out_ref):
    seq = ff_ref.shape[0]
    r = lax.broadcasted_iota(jnp.int32, (FB_CHUNK, FB_CHUNK), 0)
    c = lax.broadcasted_iota(jnp.int32, (FB_CHUNK, FB_CHUNK), 1)
    ltri = (c <= r).astype(BF16)
    pr = lax.broadcasted_iota(jnp.int32, (LANES, LANES), 0)
    pc = lax.broadcasted_iota(jnp.int32, (LANES, LANES), 1)
    place = [((pc == 3 * pr + t) & (pr < N_HEADS)).astype(BF16) for t in range(3)]

    local = []
    for ci in range(seq // FB_CHUNK):
        z = ff_ref[ci * FB_CHUNK:(ci + 1) * FB_CHUNK, :] + b_ref[...]
        logf = -(jnp.maximum(-z, 0.0) + jnp.log1p(jnp.exp(-jnp.abs(z))))
        hi, mid, lo = _split3(logf)
        local.append(_dot(ltri, hi) + _dot(ltri, mid) + _dot(ltri, lo))
    carry = jnp.zeros((1, LANES), F32)
    for ci, loc in enumerate(local):
        f = loc + carry
        carry = carry + loc[FB_CHUNK - 1:FB_CHUNK, :]
        nh, nm, nl = _split3(f * (-LOG2E))
        aug = _dot(nh, place[0]) + _dot(nm, place[1]) + _dot(nl, place[2])
        out_ref[ci * FB_CHUNK:(ci + 1) * FB_CHUNK, :] = aug.astype(BF16)


def _fox_bias(ff, b_row, batch, seq):
    return pl.pallas_call(
        _fox_bias_kernel,
        grid=(batch,),
        in_specs=[
            pl.BlockSpec((seq, LANES), lambda b: (b, 0)),
            pl.BlockSpec((1, LANES), lambda b: (0, 0)),
        ],
        out_specs=pl.BlockSpec((seq, LANES), lambda b: (b, 0)),
        out_shape=jax.ShapeDtypeStruct((batch * seq, LANES), BF16),
        compiler_params=pltpu.CompilerParams(
            dimension_semantics=("arbitrary",), vmem_limit_bytes=VMEM_LIMIT),
        name="fox_bias",
    )(ff, b_row)


ATT_T = 512
ATT_HG = 4
ATT_TILES = 2
ATT_SUB = ATT_T // MOBA_BLOCK


def _attn_kernel(is_fox, n_cast, *refs):
    q_ref, k_ref, v_ref, x_ref = refs[:4]
    cast_src = refs[4:4 + n_cast]
    o_ref = refs[4 + n_cast]
    cast_dst = refs[5 + n_cast:5 + 2 * n_cast]
    vt_ref, acc_ref, sa_ref, sb_ref = refs[5 + 2 * n_cast:]
    hg = pl.program_id(1)
    pair = pl.program_id(2)
    seq = k_ref.shape[0]
    nblk = seq // MOBA_BLOCK
    t = ATT_T

    for src, dst in zip(cast_src, cast_dst):
        dst[...] = src[...].astype(BF16)

    def head_cols(hh):
        return slice(hh * HEAD_DIM, (hh + 1) * HEAD_DIM)

    def tile_rows(tile):
        return slice(tile * t, (tile + 1) * t)

    @pl.when(pair == 0)
    def _():
        for hh in range(ATT_HG):
            for ci in range(nblk):
                rows = slice(ci * MOBA_BLOCK, (ci + 1) * MOBA_BLOCK)
                blk = v_ref[rows, head_cols(hh)].astype(F32)
                vt_ref[head_cols(hh), rows] = blk.T.astype(BF16)

    def prep(tile, qt):
        qs = []
        for hh in range(ATT_HG):
            qh = q_ref[tile_rows(tile), head_cols(hh)]
            if is_fox:
                head = hg * ATT_HG + hh
                lane = lax.broadcasted_iota(jnp.int32, (t, LANES), 1)
                pick = ((lane >= 3 * head) & (lane < 3 * head + 3)).astype(BF16)
                qh = jnp.concatenate([qh, pick], axis=1)
            qs.append(qh)
        if not is_fox:
            blk_id = lax.broadcasted_iota(jnp.int32, (nblk, t), 0)
            q_blk = qt * ATT_SUB + lax.broadcasted_iota(jnp.int32, (nblk, t), 1) // MOBA_BLOCK
            past = blk_id < q_blk
            for hh in range(ATT_HG):
                kh, km, kl = _split3(x_ref[:, head_cols(hh)])
                gate = _dot_nt(kh, qs[hh]) + _dot_nt(km, qs[hh]) + _dot_nt(kl, qs[hh])
                gate = jnp.where(past, gate, -jnp.inf)
                rank = jnp.zeros((nblk, t), jnp.int32)
                for mth in range(nblk):
                    gm = gate[mth:mth + 1, :]
                    ahead = (gm > gate) | ((gm == gate) & (mth < blk_id))
                    rank = rank + ahead.astype(jnp.int32)
                visible = ((rank < MOBA_TOP_K) & past) | (blk_id == q_blk)
                bias = jnp.where(visible, 0.0, NEG_BIG)
                bias = jnp.concatenate([bias, jnp.zeros((LANES - nblk, t), F32)], axis=0)
                qs[hh] = jnp.concatenate([qs[hh], bias.T.astype(BF16)], axis=1)
        return qs

    def scores(qs, hh, start):
        kb = k_ref[pl.ds(start, t), head_cols(hh)]
        if is_fox:
            extra = x_ref[pl.ds(start, t), :]
        else:
            key_blk = start // MOBA_BLOCK + lax.broadcasted_iota(jnp.int32, (t, LANES), 0) // MOBA_BLOCK
            extra = (lax.broadcasted_iota(jnp.int32, (t, LANES), 1) == key_blk).astype(BF16)
        return _dot_nt(jnp.concatenate([kb, extra], axis=1), qs[hh])

    def vt_chunk(hh, start):
        return vt_ref[head_cols(hh), pl.ds(start, t)]

    def produce(qs, dst_ref, start, diagonal):
        cmax = []
        for hh in range(ATT_HG):
            s = scores(qs, hh, start)
            if diagonal:
                key = lax.broadcasted_iota(jnp.int32, (t, t), 0)
                qry = lax.broadcasted_iota(jnp.int32, (t, t), 1)
                s = jnp.where(key <= qry, s, NEG_BIG)
            dst_ref[hh] = s
            cmax.append(jnp.max(s, axis=0, keepdims=True))
        return tuple(cmax)

    def consume(tile, src_ref, start, cmax, stats):
        new = []
        for hh in range(ATT_HG):
            m, l = stats[hh]
            m_new = jnp.maximum(m, cmax[hh])
            alpha = jnp.exp2(m - m_new)
            p = jnp.exp2(src_ref[hh] - m_new)
            l = alpha * l + jnp.sum(p, axis=0, keepdims=True)
            acc_ref[tile, head_cols(hh), :] = (alpha * acc_ref[tile, head_cols(hh), :]
                                               + _dot(vt_chunk(hh, start), p.astype(BF16)))
            new.append((m_new, l))
        return tuple(new)

    def sweep(tile, qs, qt, even, odd, cmax, stats):
        def step(dst_ref, src_ref):
            def run(i, cmax, stats):
                nxt = pl.multiple_of(i * t, t)
                cur = pl.multiple_of(jnp.where(i == 0, qt, i - 1) * t, t)
                cmax_next = produce(qs, dst_ref, nxt, False)
                return cmax_next, consume(tile, src_ref, cur, cmax, stats)
            return run

        def body(i, carry):
            return lax.cond(i % 2 == 0, step(*even), step(*odd), i, *carry)

        return lax.fori_loop(0, qt, body, (cmax, stats))

    def finish(tile, src_ref, qt, cmax, stats):
        last = pl.multiple_of(jnp.maximum(qt - 1, 0) * t, t)
        stats = consume(tile, src_ref, last, cmax, stats)
        for hh in range(ATT_HG):
            _, l = stats[hh]
            o_ref[tile_rows(tile), head_cols(hh)] = (
                acc_ref[tile, head_cols(hh), :] * (1.0 / l)).T.astype(BF16)

    def fresh_stats():
        return tuple((jnp.full((1, t), NEG_BIG, F32), jnp.zeros((1, t), F32)) for _ in range(ATT_HG))

    def diagonal(tile, qs, qt, dst_ref):
        acc_ref[tile] = jnp.zeros((ATT_HG * HEAD_DIM, t), F32)
        return produce(qs, dst_ref, pl.multiple_of(qt * t, t), True)

    qt0 = pair * ATT_TILES
    qs0 = prep(0, qt0)
    cmax0 = diagonal(0, qs0, qt0, sa_ref)
    cmax0, stats0 = sweep(0, qs0, qt0, (sb_ref, sa_ref), (sa_ref, sb_ref), cmax0, fresh_stats())

    qt1 = qt0 + 1
    qs1 = prep(1, qt1)
    cmax1 = diagonal(1, qs1, qt1, sb_ref)
    finish(0, sa_ref, qt0, cmax0, stats0)
    cmax1, stats1 = sweep(1, qs1, qt1, (sa_ref, sb_ref), (sb_ref, sa_ref), cmax1, fresh_stats())
    finish(1, sa_ref, qt1, cmax1, stats1)


def _rider_specs(riders, n_steps, step_of):
    specs = []
    for w in riders:
        rows = w.shape[0] // n_steps
        assert rows * n_steps == w.shape[0] and rows % 16 == 0
        specs.append(pl.BlockSpec((rows, w.shape[1]), lambda *idx: (step_of(*idx), 0)))
    return specs


def _attention(proj, kmean, faug, moba_riders, batch, seq):
    assert ATT_TILES == 2
    n = batch * seq
    npair = seq // (ATT_TILES * ATT_T)
    nblk = seq // MOBA_BLOCK
    hw = ATT_HG * HEAD_DIM
    n_hg = N_HEADS // ATT_HG
    grid = (batch, n_hg, npair)
    params = pltpu.CompilerParams(
        dimension_semantics=("arbitrary", "arbitrary", "arbitrary"),
        vmem_limit_bytes=VMEM_LIMIT)

    def q_spec(col0):
        return pl.BlockSpec((ATT_TILES * ATT_T, hw), lambda b, h, i: (b * npair + i, col0 + h))

    def kv_spec(col0):
        return pl.BlockSpec((seq, hw), lambda b, h, i: (b, col0 + h))

    region = WIDTH // hw
    out_spec = pl.BlockSpec((ATT_TILES * ATT_T, hw), lambda b, h, i: (b * npair + i, h))
    out_shape = jax.ShapeDtypeStruct((n, WIDTH), BF16)
    common = [pltpu.VMEM((hw, seq), BF16), pltpu.VMEM((ATT_TILES, hw, ATT_T), F32),
              pltpu.VMEM((ATT_HG, ATT_T, ATT_T), F32), pltpu.VMEM((ATT_HG, ATT_T, ATT_T), F32)]

    def call(is_fox, name, specs, args, riders, extra_scratch):
        rider_specs = _rider_specs(riders, batch * n_hg * npair,
                                   lambda b, h, i: (b * n_hg + h) * npair + i)
        return pl.pallas_call(
            functools.partial(_attn_kernel, is_fox, len(riders)),
            grid=grid,
            in_specs=specs + rider_specs,
            out_specs=[out_spec] + rider_specs,
            out_shape=[out_shape] + [jax.ShapeDtypeStruct(w.shape, BF16) for w in riders],
            scratch_shapes=common + extra_scratch,
            compiler_params=params,
            name=name,
        )(*args, *riders)

    o_moba, *moba_cast = call(
        False, "moba_attn",
        [q_spec(0), kv_spec(region), kv_spec(2 * region),
         pl.BlockSpec((nblk, hw), lambda b, h, i: (b, h))],
        (proj, proj, proj, kmean), moba_riders, [])
    o_fox, = call(
        True, "fox_attn",
        [q_spec(3 * region), kv_spec(4 * region), kv_spec(5 * region),
         pl.BlockSpec((seq, LANES), lambda b, h, i: (b, 0))],
        (proj, proj, proj, faug), (), [])
    return o_moba, o_fox, moba_cast


MIX_TM = 256


def _sigmoid(z):
    return 1.0 / (1.0 + jnp.exp(-z))


def _mix_kernel(n_cast, om_ref, of_ref, ga_ref, gb_ref, x_ref, wm_ref, wf_ref, wo_ref, g_ref, *rest):
    cast_src = rest[:n_cast]
    out_ref = rest[n_cast]
    cast_dst = rest[n_cast + 1:]
    for src, dst in zip(cast_src, cast_dst):
        dst[...] = src[...].astype(BF16)
    y_m = _dot(om_ref[...], wm_ref[...])
    y_f = _dot(of_ref[...], wf_ref[...])
    merged = _sigmoid(ga_ref[...].astype(F32)) * y_m + _sigmoid(gb_ref[...].astype(F32)) * y_f
    mixed = _dot(merged.astype(BF16), wo_ref[...])
    out_ref[...] = x_ref[...] + mixed * _rms_scale(mixed) * g_ref[...]


def _mix_out(o_moba, o_fox, proj, x2, wm, wf, wo, g, riders):
    n = x2.shape[0]
    ga_blk = QKV_COLS // D_MODEL
    const = dict(pipeline_mode=pl.Buffered(1))
    rider_specs = _rider_specs(riders, n // MIX_TM, lambda i: i)
    x1, *cast = pl.pallas_call(
        functools.partial(_mix_kernel, len(riders)),
        grid=(n // MIX_TM,),
        in_specs=[
            pl.BlockSpec((MIX_TM, WIDTH), lambda i: (i, 0)),
            pl.BlockSpec((MIX_TM, WIDTH), lambda i: (i, 0)),
            pl.BlockSpec((MIX_TM, D_MODEL), lambda i: (i, ga_blk)),
            pl.BlockSpec((MIX_TM, D_MODEL), lambda i: (i, ga_blk + 1)),
            pl.BlockSpec((MIX_TM, D_MODEL), lambda i: (i, 0)),
            pl.BlockSpec((WIDTH, D_MODEL), lambda i: (0, 0), **const),
            pl.BlockSpec((WIDTH, D_MODEL), lambda i: (0, 0), **const),
            pl.BlockSpec((D_MODEL, D_MODEL), lambda i: (0, 0), **const),
            pl.BlockSpec((1, D_MODEL), lambda i: (0, 0)),
        ] + rider_specs,
        out_specs=[pl.BlockSpec((MIX_TM, D_MODEL), lambda i: (i, 0))] + rider_specs,
        out_shape=[jax.ShapeDtypeStruct((n, D_MODEL), F32)]
        + [jax.ShapeDtypeStruct(w.shape, BF16) for w in riders],
        compiler_params=pltpu.CompilerParams(
            dimension_semantics=("arbitrary",), vmem_limit_bytes=VMEM_LIMIT),
        name="mix_out",
    )(o_moba, o_fox, proj, proj, x2, wm, wf, wo, g, *riders)
    return x1, cast


MLP_TM = 512
MLP_TF = 1024


def _mlp_kernel(x_ref, gpre_ref, wu_ref, wd_ref, gpost_ref, out_ref, h_ref, acc_ref):
    f = pl.program_id(1)

    @pl.when(f == 0)
    def _():
        x = x_ref[...]
        h_ref[...] = (x * _rms_scale(x) * gpre_ref[...]).astype(BF16)
        acc_ref[...] = jnp.zeros_like(acc_ref)

    u = _dot(h_ref[...], wu_ref[...])
    a = jnp.square(jnp.maximum(u, 0.0)).astype(BF16)
    acc_ref[...] += _dot(a, wd_ref[...])

    @pl.when(f == pl.num_programs(1) - 1)
    def _():
        mo = acc_ref[...]
        out_ref[...] = x_ref[...] + mo * _rms_scale(mo) * gpost_ref[...]


def _mlp(x1, gpre, wu, wd, gpost):
    n = x1.shape[0]
    return pl.pallas_call(
        _mlp_kernel,
        grid=(n // MLP_TM, D_FF // MLP_TF),
        in_specs=[
            pl.BlockSpec((MLP_TM, D_MODEL), lambda i, f: (i, 0)),
            pl.BlockSpec((1, D_MODEL), lambda i, f: (0, 0)),
            pl.BlockSpec((D_MODEL, MLP_TF), lambda i, f: (0, f)),
            pl.BlockSpec((MLP_TF, D_MODEL), lambda i, f: (f, 0)),
            pl.BlockSpec((1, D_MODEL), lambda i, f: (0, 0)),
        ],
        out_specs=pl.BlockSpec((MLP_TM, D_MODEL), lambda i, f: (i, 0)),
        out_shape=jax.ShapeDtypeStruct((n, D_MODEL), F32),
        scratch_shapes=[pltpu.VMEM((MLP_TM, D_MODEL), BF16), pltpu.VMEM((MLP_TM, D_MODEL), F32)],
        compiler_params=pltpu.CompilerParams(
            dimension_semantics=("arbitrary", "arbitrary"), vmem_limit_bytes=VMEM_LIMIT),
        name="mlp",
    )(x1, gpre, wu, wd, gpost)


def _rope_tables(seq):
    inv_freq = ROPE_THETA ** (-jnp.arange(0, ROPE_DIM, 2, dtype=F32) / ROPE_DIM)
    ang = jnp.arange(seq, dtype=F32)[:, None] * inv_freq[None, :]
    cos, sin = jnp.cos(ang), jnp.sin(ang)
    gap = HEAD_DIM // 2 - ROPE_HALF
    ones = jnp.ones((seq, gap), F32)
    zeros = jnp.zeros((seq, gap), F32)
    cos_t = jnp.concatenate([cos, ones, cos, ones], axis=1)
    sin_t = jnp.concatenate([-sin, zeros, sin, zeros], axis=1)
    return cos_t, sin_t


def _layer(x2, w_in, b_forget, w_bm, w_bf, w_out, g_mix_pre, g_mix_post,
           w_up, w_down, g_mlp_pre, g_mlp_post, batch, seq):
    b_row = jnp.pad(b_forget, (0, LANES - N_HEADS)).reshape(1, LANES)
    cos_t, sin_t = _rope_tables(seq)

    proj, ff, kmean = _in_proj(x2, g_mix_pre.reshape(1, -1), w_in, cos_t, sin_t, seq)
    faug = _fox_bias(ff, b_row, batch, seq)
    kmean = kmean.reshape(batch * (seq // MOBA_BLOCK), WIDTH)
    o_moba, o_fox, (wm, wf, wo) = _attention(proj, kmean, faug, (w_bm, w_bf, w_out), batch, seq)
    x1, (wu, wd) = _mix_out(o_moba, o_fox, proj, x2, wm, wf, wo, g_mix_post.reshape(1, -1),
                            (w_up, w_down))
    return _mlp(x1, g_mlp_pre.reshape(1, -1), wu, wd, g_mlp_post.reshape(1, -1))


def kernel(x, w_in, b_forget, w_branch_moba, w_branch_fox, w_out, g_mix_pre, g_mix_post,
           w_up, w_down, g_mlp_pre, g_mlp_post):
    batch, seq, d = x.shape
    assert d == D_MODEL and seq % IN_TM == 0 and seq % (ATT_TILES * ATT_T) == 0
    assert w_in.shape[-1] == PROJ_COLS + N_HEADS
    x2 = x.reshape(batch * seq, d)
    for l in range(w_in.shape[0]):
        x2 = _layer(x2, w_in[l], b_forget[l], w_branch_moba[l], w_branch_fox[l], w_out[l],
                    g_mix_pre[l], g_mix_post[l], w_up[l], w_down[l], g_mlp_pre[l],
                    g_mlp_post[l], batch, seq)
    return x2.reshape(batch, seq, d)
```

```python
import functools
import math

import jax
import jax.numpy as jnp
from jax import lax
from jax.experimental import pallas as pl
from jax.experimental.pallas import tpu as pltpu

D_MODEL = 2048
HEAD_DIM = 128
N_HEADS = 8
WIDTH = N_HEADS * HEAD_DIM
MOBA_BLOCK = 256
MOBA_TOP_K = 3
ROPE_THETA = 500000.0
ROPE_DIM = HEAD_DIM // 4
ROPE_HALF = ROPE_DIM // 2
D_FF = 4 * D_MODEL
RMS_EPS = 1e-6
NEG_BIG = -1e30
LANES = 128
SUBLANES = 8

QKV_COLS = 6 * WIDTH
GATE_COLS = 2 * D_MODEL
PROJ_COLS = QKV_COLS + GATE_COLS

VMEM_LIMIT = 56 * 1024 * 1024

BF16 = jnp.bfloat16
F32 = jnp.float32
LOG2E = math.log2(math.e)
Q_SCALE = HEAD_DIM ** -0.5 * LOG2E


def _rms_scale(x):
    return lax.rsqrt(jnp.mean(x * x, axis=-1, keepdims=True) + RMS_EPS)


def _split3(x):
    hi = x.astype(BF16)
    r1 = x - hi.astype(F32)
    mid = r1.astype(BF16)
    lo = (r1 - mid.astype(F32)).astype(BF16)
    return hi, mid, lo


def _dot(a, b):
    return jnp.dot(a, b, preferred_element_type=F32)


def _dot_nt(a, b):
    return lax.dot_general(a, b, (((1,), (1,)), ((), ())), preferred_element_type=F32)


IN_TM = 1024
IN_TN = 1024
IN_ROT_TILES = 2 * WIDTH // IN_TN


def _permute_rotary_rows(blk):
    mid = HEAD_DIM // 2 + ROPE_HALF
    parts = []
    for hd in range(blk.shape[0] // HEAD_DIM):
        b = hd * HEAD_DIM
        parts += [blk[b:b + ROPE_HALF], blk[b + ROPE_DIM:b + mid],
                  blk[b + ROPE_HALF:b + ROPE_DIM], blk[b + mid:b + HEAD_DIM]]
    return jnp.concatenate(parts, axis=0)


IN_W_SLOTS = 3
IN_VMEM_LIMIT = 60 * 1024 * 1024


def _in_proj_kernel(x_ref, g_ref, wt_hbm, wff_ref, cos_ref, sin_ref,
                    proj_ref, ff_ref, kmean_ref, h_ref, wbuf_ref, wsem):
    j = pl.program_id(1)
    n_j = pl.num_programs(1)
    step = pl.program_id(0) * n_j + j
    n_steps = pl.num_programs(0) * n_j

    def window(s):
        col = s % n_j
        n_qkv = QKV_COLS // IN_TN
        row = pl.multiple_of(
            jnp.where(col < n_qkv, col * IN_TN, QKV_COLS + N_HEADS + (col - n_qkv) * IN_TN), SUBLANES)
        slot = s % IN_W_SLOTS
        return pltpu.make_async_copy(wt_hbm.at[pl.ds(row, IN_TN), :], wbuf_ref.at[slot],
                                     wsem.at[slot])

    @pl.when(step == 0)
    def _():
        for s in range(IN_W_SLOTS - 1):
            window(s).start()

    @pl.when(step + IN_W_SLOTS - 1 < n_steps)
    def _():
        window(step + IN_W_SLOTS - 1).start()

    window(step).wait()
    w_ref = wbuf_ref.at[step % IN_W_SLOTS]

    @pl.when(j == 0)
    def _():
        x = x_ref[...]
        h = (x * _rms_scale(x) * g_ref[...]).astype(BF16)
        h_ref[...] = h
        ff_ref[...] = _dot_nt(h, wff_ref[...].astype(BF16))

    def tile(permute):
        w = w_ref[...]
        w = (_permute_rotary_rows(w) if permute else w).astype(BF16)
        return _dot_nt(h_ref[...], w)

    def rotary(t):
        outs = []
        for hd in range(IN_TN // HEAD_DIM):
            th = t[:, hd * HEAD_DIM:(hd + 1) * HEAD_DIM]
            outs.append(th * cos_ref[...] + pltpu.roll(th, HEAD_DIM // 2, 1) * sin_ref[...])
        return jnp.concatenate(outs, axis=1)

    @pl.when(j == 0)
    def _():
        proj_ref[...] = (rotary(tile(True)) * Q_SCALE).astype(BF16)

    @pl.when(j == 1)
    def _():
        kr = rotary(tile(True))
        proj_ref[...] = kr.astype(BF16)
        nblk = IN_TM // MOBA_BLOCK
        means = [jnp.mean(kr[b * MOBA_BLOCK:(b + 1) * MOBA_BLOCK, :], axis=0, keepdims=True)
                 for b in range(nblk)]
        kmean_ref[0] = jnp.concatenate(means, axis=0)

    @pl.when(j == 3)
    def _():
        proj_ref[...] = (tile(False) * Q_SCALE).astype(BF16)

    @pl.when((j == 2) | (j >= 4))
    def _():
        proj_ref[...] = tile(False).astype(BF16)


def _in_proj(x2, g, w_in, cos_t, sin_t, seq):
    assert IN_ROT_TILES == 2
    n = x2.shape[0]
    seq_tiles = seq // IN_TM
    assert (QKV_COLS + N_HEADS) % SUBLANES == 0 and n // IN_TM * (PROJ_COLS // IN_TN) >= IN_W_SLOTS
    wt = w_in.T

    return pl.pallas_call(
        _in_proj_kernel,
        grid=(n // IN_TM, PROJ_COLS // IN_TN),
        in_specs=[
            pl.BlockSpec((IN_TM, D_MODEL), lambda i, j: (i, 0)),
            pl.BlockSpec((1, D_MODEL), lambda i, j: (0, 0)),
            pl.BlockSpec(memory_space=pl.ANY),
            pl.BlockSpec((LANES, D_MODEL), lambda i, j: (QKV_COLS // LANES, 0),
                         pipeline_mode=pl.Buffered(1)),
            pl.BlockSpec((IN_TM, HEAD_DIM), lambda i, j: (i % seq_tiles, 0)),
            pl.BlockSpec((IN_TM, HEAD_DIM), lambda i, j: (i % seq_tiles, 0)),
        ],
        out_specs=[
            pl.BlockSpec((IN_TM, IN_TN), lambda i, j: (i, j)),
            pl.BlockSpec((IN_TM, LANES), lambda i, j: (i, 0)),
            pl.BlockSpec((1, IN_TM // MOBA_BLOCK, WIDTH), lambda i, j: (i, 0, 0)),
        ],
        out_shape=[
            jax.ShapeDtypeStruct((n, PROJ_COLS), BF16),
            jax.ShapeDtypeStruct((n, LANES), F32),
            jax.ShapeDtypeStruct((n // IN_TM, IN_TM // MOBA_BLOCK, WIDTH), F32),
        ],
        scratch_shapes=[pltpu.VMEM((IN_TM, D_MODEL), BF16),
                        pltpu.VMEM((IN_W_SLOTS, IN_TN, D_MODEL), F32),
                        pltpu.SemaphoreType.DMA((IN_W_SLOTS,))],
        compiler_params=pltpu.CompilerParams(
            dimension_semantics=("arbitrary", "arbitrary"), vmem_limit_bytes=IN_VMEM_LIMIT),
        name="in_proj",
    )(x2, g, wt, wt, cos_t, sin_t)


FB_CHUNK = 256


def _fox_bias_kernel(ff_ref, b_ref, out_ref):
    seq = ff_ref.shape[0]
    r = lax.broadcasted_iota(jnp.int32, (FB_CHUNK, FB_CHUNK), 0)
    c = lax.broadcasted_iota(jnp.int32, (FB_CHUNK, FB_CHUNK), 1)
    ltri = (c <= r).astype(BF16)
    pr = lax.broadcasted_iota(jnp.int32, (LANES, LANES), 0)
    pc = lax.broadcasted_iota(jnp.int32, (LANES, LANES), 1)
    place = [((pc == 3 * pr + t) & (pr < N_HEADS)).astype(BF16) for t in range(3)]

    local = []
    for ci in range(seq // FB_CHUNK):
        z = ff_ref[ci * FB_CHUNK:(ci + 1) * FB_CHUNK, :] + b_ref[...]
        logf = -(jnp.maximum(-z, 0.0) + jnp.log1p(jnp.exp(-jnp.abs(z))))
        hi, mid, lo = _split3(logf)
        local.append(_dot(ltri, hi) + _dot(ltri, mid) + _dot(ltri, lo))
    carry = jnp.zeros((1, LANES), F32)
    for ci, loc in enumerate(local):
        f = loc + carry
        carry = carry + loc[FB_CHUNK - 1:FB_CHUNK, :]
        nh, nm, nl = _split3(f * (-LOG2E))
        aug = _dot(nh, place[0]) + _dot(nm, place[1]) + _dot(nl, place[2])
        out_ref[ci * FB_CHUNK:(ci + 1) * FB_CHUNK, :] = aug.astype(BF16)


def _fox_bias(ff, b_row, batch, seq):
    return pl.pallas_call(
        _fox_bias_kernel,
        grid=(batch,),
        in_specs=[
            pl.BlockSpec((seq, LANES), lambda b: (b, 0)),
            pl.BlockSpec((1, LANES), lambda b: (0, 0)),
        ],
        out_specs=pl.BlockSpec((seq, LANES), lambda b: (b, 0)),
        out_shape=jax.ShapeDtypeStruct((batch * seq, LANES), BF16),
        compiler_params=pltpu.CompilerParams(
            dimension_semantics=("arbitrary",), vmem_limit_bytes=VMEM_LIMIT),
        name="fox_bias",
    )(ff, b_row)


ATT_T = 512
ATT_HG = 4
ATT_TILES = 2
ATT_SUB = ATT_T // MOBA_BLOCK


def _attn_kernel(is_fox, n_cast, *refs):
    q_ref, k_ref, v_ref, x_ref = refs[:4]
    cast_src = refs[4:4 + n_cast]
    o_ref = refs[4 + n_cast]
    cast_dst = refs[5 + n_cast:5 + 2 * n_cast]
    vt_ref, acc_ref, sa_ref, sb_ref = refs[5 + 2 * n_cast:]
    hg = pl.program_id(1)
    pair = pl.program_id(2)
    seq = k_ref.shape[0]
    nblk = seq // MOBA_BLOCK
    t = ATT_T

    for src, dst in zip(cast_src, cast_dst):
        dst[...] = src[...].astype(BF16)

    def head_cols(hh):
        return slice(hh * HEAD_DIM, (hh + 1) * HEAD_DIM)

    def tile_rows(tile):
        return slice(tile * t, (tile + 1) * t)

    @pl.when(pair == 0)
    def _():
        for hh in range(ATT_HG):
            for ci in range(nblk):
                rows = slice(ci * MOBA_BLOCK, (ci + 1) * MOBA_BLOCK)
                blk = v_ref[rows, head_cols(hh)].astype(F32)
                vt_ref[head_cols(hh), rows] = blk.T.astype(BF16)

    def prep(tile, qt):
        qs = []
        for hh in range(ATT_HG):
            qh = q_ref[tile_rows(tile), head_cols(hh)]
            if is_fox:
                head = hg * ATT_HG + hh
                lane = lax.broadcasted_iota(jnp.int32, (t, LANES), 1)
                pick = ((lane >= 3 * head) & (lane < 3 * head + 3)).astype(BF16)
                qh = jnp.concatenate([qh, pick], axis=1)
            qs.append(qh)
        if not is_fox:
            blk_id = lax.broadcasted_iota(jnp.int32, (nblk, t), 0)
            q_blk = qt * ATT_SUB + lax.broadcasted_iota(jnp.int32, (nblk, t), 1) // MOBA_BLOCK
            past = blk_id < q_blk
            for hh in range(ATT_HG):
                kh, km, kl = _split3(x_ref[:, head_cols(hh)])
                gate = _dot_nt(kh, qs[hh]) + _dot_nt(km, qs[hh]) + _dot_nt(kl, qs[hh])
                gate = jnp.where(past, gate, -jnp.inf)
                rank = jnp.zeros((nblk, t), jnp.int32)
                for mth in range(nblk):
                    gm = gate[mth:mth + 1, :]
                    ahead = (gm > gate) | ((gm == gate) & (mth < blk_id))
                    rank = rank + ahead.astype(jnp.int32)
                visible = ((rank < MOBA_TOP_K) & past) | (blk_id == q_blk)
                bias = jnp.where(visible, 0.0, NEG_BIG)
                bias = jnp.concatenate([bias, jnp.zeros((LANES - nblk, t), F32)], axis=0)
                qs[hh] = jnp.concatenate([qs[hh], bias.T.astype(BF16)], axis=1)
        return qs

    def scores(qs, hh, start):
        kb = k_ref[pl.ds(start, t), head_cols(hh)]
        if is_fox:
            extra = x_ref[pl.ds(start, t), :]
        else:
            key_blk = start // MOBA_BLOCK + lax.broadcasted_iota(jnp.int32, (t, LANES), 0) // MOBA_BLOCK
            extra = (lax.broadcasted_iota(jnp.int32, (t, LANES), 1) == key_blk).astype(BF16)
        return _dot_nt(jnp.concatenate([kb, extra], axis=1), qs[hh])

    def vt_chunk(hh, start):
        return vt_ref[head_cols(hh), pl.ds(start, t)]

    def produce(qs, dst_ref, start, diagonal):
        cmax = []
        for hh in range(ATT_HG):
            s = scores(qs, hh, start)
            if diagonal:
                key = lax.broadcasted_iota(jnp.int32, (t, t), 0)
                qry = lax.broadcasted_iota(jnp.int32, (t, t), 1)
                s = jnp.where(key <= qry, s, NEG_BIG)
            dst_ref[hh] = s
            cmax.append(jnp.max(s, axis=0, keepdims=True))
        return tuple(cmax)

    def consume(tile, src_ref, start, cmax, stats):
        new = []
        for hh in range(ATT_HG):
            m, l = stats[hh]
            m_new = jnp.maximum(m, cmax[hh])
            alpha = jnp.exp2(m - m_new)
            p = jnp.exp2(src_ref[hh] - m_new)
            l = alpha * l + jnp.sum(p, axis=0, keepdims=True)
            acc_ref[tile, head_cols(hh), :] = (alpha * acc_ref[tile, head_cols(hh), :]
                                               + _dot(vt_chunk(hh, start), p.astype(BF16)))
            new.append((m_new, l))
        return tuple(new)

    def sweep(tile, qs, qt, even, odd, cmax, stats):
        def step(dst_ref, src_ref):
            def run(i, cmax, stats):
                nxt = pl.multiple_of(i * t, t)
                cur = pl.multiple_of(jnp.where(i == 0, qt, i - 1) * t, t)
                cmax_next = produce(qs, dst_ref, nxt, False)
                return cmax_next, consume(tile, src_ref, cur, cmax, stats)
            return run

        def body(i, carry):
            return lax.cond(i % 2 == 0, step(*even), step(*odd), i, *carry)

        return lax.fori_loop(0, qt, body, (cmax, stats))

    def finish(tile, src_ref, qt, cmax, stats):
        last = pl.multiple_of(jnp.maximum(qt - 1, 0) * t, t)
        stats = consume(tile, src_ref, last, cmax, stats)
        for hh in range(ATT_HG):
            _, l = stats[hh]
            o_ref[tile_rows(tile), head_cols(hh)] = (
                acc_ref[tile, head_cols(hh), :] * (1.0 / l)).T.astype(BF16)

    def fresh_stats():
        return tuple((jnp.full((1, t), NEG_BIG, F32), jnp.zeros((1, t), F32)) for _ in range(ATT_HG))

    def diagonal(tile, qs, qt, dst_ref):
        acc_ref[tile] = jnp.zeros((ATT_HG * HEAD_DIM, t), F32)
        return produce(qs, dst_ref, pl.multiple_of(qt * t, t), True)

    qt0 = pair * ATT_TILES
    qs0 = prep(0, qt0)
    cmax0 = diagonal(0, qs0, qt0, sa_ref)
    cmax0, stats0 = sweep(0, qs0, qt0, (sb_ref, sa_ref), (sa_ref, sb_ref), cmax0, fresh_stats())

    qt1 = qt0 + 1
    qs1 = prep(1, qt1)
    cmax1 = diagonal(1, qs1, qt1, sb_ref)
    finish(0, sa_ref, qt0, cmax0, stats0)
    cmax1, stats1 = sweep(1, qs1, qt1, (sa_ref, sb_ref), (sb_ref, sa_ref), cmax1, fresh_stats())
    finish(1, sa_ref, qt1, cmax1, stats1)


def _rider_specs(riders, n_steps, step_of):
    specs = []
    for w in riders:
        rows = w.shape[0] // n_steps
        assert rows * n_steps == w.shape[0] and rows % 16 == 0
        specs.append(pl.BlockSpec((rows, w.shape[1]), lambda *idx: (step_of(*idx), 0)))
    return specs


def _attention(proj, kmean, faug, moba_riders, batch, seq):
    assert ATT_TILES == 2
    n = batch * seq
    npair = seq // (ATT_TILES * ATT_T)
    nblk = seq // MOBA_BLOCK
    hw = ATT_HG * HEAD_DIM
    n_hg = N_HEADS // ATT_HG
    grid = (batch, n_hg, npair)
    params = pltpu.CompilerParams(
        dimension_semantics=("arbitrary", "arbitrary", "arbitrary"),
        vmem_limit_bytes=VMEM_LIMIT)

    def q_spec(col0):
        return pl.BlockSpec((ATT_TILES * ATT_T, hw), lambda b, h, i: (b * npair + i, col0 + h))

    def kv_spec(col0):
        return pl.BlockSpec((seq, hw), lambda b, h, i: (b, col0 + h))

    region = WIDTH // hw
    out_spec = pl.BlockSpec((ATT_TILES * ATT_T, hw), lambda b, h, i: (b * npair + i, h))
    out_shape = jax.ShapeDtypeStruct((n, WIDTH), BF16)
    common = [pltpu.VMEM((hw, seq), BF16), pltpu.VMEM((ATT_TILES, hw, ATT_T), F32),
              pltpu.VMEM((ATT_HG, ATT_T, ATT_T), F32), pltpu.VMEM((ATT_HG, ATT_T, ATT_T), F32)]

    def call(is_fox, name, specs, args, riders, extra_scratch):
        rider_specs = _rider_specs(riders, batch * n_hg * npair,
                                   lambda b, h, i: (b * n_hg + h) * npair + i)
        return pl.pallas_call(
            functools.partial(_attn_kernel, is_fox, len(riders)),
            grid=grid,
            in_specs=specs + rider_specs,
            out_specs=[out_spec] + rider_specs,
            out_shape=[out_shape] + [jax.ShapeDtypeStruct(w.shape, BF16) for w in riders],
            scratch_shapes=common + extra_scratch,
            compiler_params=params,
            name=name,
        )(*args, *riders)

    o_moba, *moba_cast = call(
        False, "moba_attn",
        [q_spec(0), kv_spec(region), kv_spec(2 * region),
         pl.BlockSpec((nblk, hw), lambda b, h, i: (b, h))],
        (proj, proj, proj, kmean), moba_riders, [])
    o_fox, = call(
        True, "fox_attn",
        [q_spec(3 * region), kv_spec(4 * region), kv_spec(5 * region),
         pl.BlockSpec((seq, LANES), lambda b, h, i: (b, 0))],
        (proj, proj, proj, faug), (), [])
    return o_moba, o_fox, moba_cast


MIX_TM = 256


def _sigmoid(z):
    return 1.0 / (1.0 + jnp.exp(-z))


def _mix_kernel(n_cast, om_ref, of_ref, ga_ref, gb_ref, x_ref, wm_ref, wf_ref, wo_ref, g_ref, *rest):
    cast_src = rest[:n_cast]
    out_ref = rest[n_cast]
    cast_dst = rest[n_cast + 1:]
    for src, dst in zip(cast_src, cast_dst):
        dst[...] = src[...].astype(BF16)
    y_m = _dot(om_ref[...], wm_ref[...])
    y_f = _dot(of_ref[...], wf_ref[...])
    merged = _sigmoid(ga_ref[...].astype(F32)) * y_m + _sigmoid(gb_ref[...].astype(F32)) * y_f
    mixed = _dot(merged.astype(BF16), wo_ref[...])
    out_ref[...] = x_ref[...] + mixed * _rms_scale(mixed) * g_ref[...]


def _mix_out(o_moba, o_fox, proj, x2, wm, wf, wo, g, riders):
    n = x2.shape[0]
    ga_blk = QKV_COLS // D_MODEL
    const = dict(pipeline_mode=pl.Buffered(1))
    rider_specs = _rider_specs(riders, n // MIX_TM, lambda i: i)
    x1, *cast = pl.pallas_call(
        functools.partial(_mix_kernel, len(riders)),
        grid=(n // MIX_TM,),
        in_specs=[
            pl.BlockSpec((MIX_TM, WIDTH), lambda i: (i, 0)),
            pl.BlockSpec((MIX_TM, WIDTH), lambda i: (i, 0)),
            pl.BlockSpec((MIX_TM, D_MODEL), lambda i: (i, ga_blk)),
            pl.BlockSpec((MIX_TM, D_MODEL), lambda i: (i, ga_blk + 1)),
            pl.BlockSpec((MIX_TM, D_MODEL), lambda i: (i, 0)),
            pl.BlockSpec((WIDTH, D_MODEL), lambda i: (0, 0), **const),
            pl.BlockSpec((WIDTH, D_MODEL), lambda i: (0, 0), **const),
            pl.BlockSpec((D_MODEL, D_MODEL), lambda i: (0, 0), **const),
            pl.BlockSpec((1, D_MODEL), lambda i: (0, 0)),
        ] + rider_specs,
        out_specs=[pl.BlockSpec((MIX_TM, D_MODEL), lambda i: (i, 0))] + rider_specs,
        out_shape=[jax.ShapeDtypeStruct((n, D_MODEL), F32)]
        + [jax.ShapeDtypeStruct(w.shape, BF16) for w in riders],
        compiler_params=pltpu.CompilerParams(
            dimension_semantics=("arbitrary",), vmem_limit_bytes=VMEM_LIMIT),
        name="mix_out",
    )(o_moba, o_fox, proj, proj, x2, wm, wf, wo, g, *riders)
    return x1, cast


MLP_TM = 512
MLP_TF = 1024


def _mlp_kernel(x_ref, gpre_ref, wu_ref, wd_ref, gpost_ref, out_ref, h_ref, acc_ref):
    f = pl.program_id(1)

    @pl.when(f == 0)
    def _():
        x = x_ref[...]
        h_ref[...] = (x * _rms_scale(x) * gpre_ref[...]).astype(BF16)
        acc_ref[...] = jnp.zeros_like(acc_ref)

    u = _dot(h_ref[...], wu_ref[...])
    a = jnp.square(jnp.maximum(u, 0.0)).astype(BF16)
    acc_ref[...] += _dot(a, wd_ref[...])

    @pl.when(f == pl.num_programs(1) - 1)
    def _():
        mo = acc_ref[...]
        out_ref[...] = x_ref[...] + mo * _rms_scale(mo) * gpost_ref[...]


def _mlp(x1, gpre, wu, wd, gpost):
    n = x1.shape[0]
    return pl.pallas_call(
        _mlp_kernel,
        grid=(n // MLP_TM, D_FF // MLP_TF),
        in_specs=[
            pl.BlockSpec((MLP_TM, D_MODEL), lambda i, f: (i, 0)),
            pl.BlockSpec((1, D_MODEL), lambda i, f: (0, 0)),
            pl.BlockSpec((D_MODEL, MLP_TF), lambda i, f: (0, f)),
            pl.BlockSpec((MLP_TF, D_MODEL), lambda i, f: (f, 0)),
            pl.BlockSpec((1, D_MODEL), lambda i, f: (0, 0)),
        ],
        out_specs=pl.BlockSpec((MLP_TM, D_MODEL), lambda i, f: (i, 0)),
        out_shape=jax.ShapeDtypeStruct((n, D_MODEL), F32),
        scratch_shapes=[pltpu.VMEM((MLP_TM, D_MODEL), BF16), pltpu.VMEM((MLP_TM, D_MODEL), F32)],
        compiler_params=pltpu.CompilerParams(
            dimension_semantics=("arbitrary", "arbitrary"), vmem_limit_bytes=VMEM_LIMIT),
        name="mlp",
    )(x1, gpre, wu, wd, gpost)


def _rope_tables(seq):
    inv_freq = ROPE_THETA ** (-jnp.arange(0, ROPE_DIM, 2, dtype=F32) / ROPE_DIM)
    ang = jnp.arange(seq, dtype=F32)[:, None] * inv_freq[None, :]
    cos, sin = jnp.cos(ang), jnp.sin(ang)
    gap = HEAD_DIM // 2 - ROPE_HALF
    ones = jnp.ones((seq, gap), F32)
    zeros = jnp.zeros((seq, gap), F32)
    cos_t = jnp.concatenate([cos, ones, cos, ones], axis=1)
    sin_t = jnp.concatenate([-sin, zeros, sin, zeros], axis=1)
    return cos_t, sin_t


def _layer(x2, w_in, b_forget, w_bm, w_bf, w_out, g_mix_pre, g_mix_post,
           w_up, w_down, g_mlp_pre, g_mlp_post, batch, seq):
    b_row = jnp.pad(b_forget, (0, LANES - N_HEADS)).reshape(1, LANES)
    cos_t, sin_t = _rope_tables(seq)

    proj, ff, kmean = _in_proj(x2, g_mix_pre.reshape(1, -1), w_in, cos_t, sin_t, seq)
    faug = _fox_bias(ff, b_row, batch, seq)
    kmean = kmean.reshape(batch * (seq // MOBA_BLOCK), WIDTH)
    o_moba, o_fox, (wm, wf, wo) = _attention(proj, kmean, faug, (w_bm, w_bf, w_out), batch, seq)
    x1, (wu, wd) = _mix_out(o_moba, o_fox, proj, x2, wm, wf, wo, g_mix_post.reshape(1, -1),
                            (w_up, w_down))
    return _mlp(x1, g_mlp_pre.reshape(1, -1), wu, wd, g_mlp_post.reshape(1, -1))


def kernel(x, w_in, b_forget, w_branch_moba, w_branch_fox, w_out, g_mix_pre, g_mix_post,
           w_up, w_down, g_mlp_pre, g_mlp_post):
    batch, seq, d = x.shape
    assert d == D_MODEL and seq % IN_TM == 0 and seq % (ATT_TILES * ATT_T) == 0
    assert w_in.shape[-1] == PROJ_COLS + N_HEADS
    x2 = x.reshape(batch * seq, d)
    for l in range(w_in.shape[0]):
        x2 = _layer(x2, w_in[l], b_forget[l], w_branch_moba[l], w_branch_fox[l], w_out[l],
                    g_mix_pre[l], g_mix_post[l], w_up[l], w_down[l], g_mlp_pre[l],
                    g_mlp_post[l], batch, seq)
    return x2.reshape(batch, seq, d)
```

```python
import functools
import math

import numpy as np
import jax
import jax.numpy as jnp
from jax import lax
from jax.experimental import pallas as pl
from jax.experimental.pallas import tpu as pltpu

D_MODEL = 2048
HEAD_DIM = 128
N_HEADS = 8
WIDTH = N_HEADS * HEAD_DIM
MOBA_BLOCK = 256
MOBA_TOP_K = 3
ROPE_THETA = 500000.0
ROPE_DIM = HEAD_DIM // 4
ROPE_HALF = ROPE_DIM // 2
D_FF = 4 * D_MODEL
RMS_EPS = 1e-6
NEG_BIG = -1e30
LANES = 128
SUBLANES = 8

QKV_COLS = 6 * WIDTH
GATE_COLS = 2 * D_MODEL
PROJ_COLS = QKV_COLS + GATE_COLS

VMEM_LIMIT = 56 * 1024 * 1024

BF16 = jnp.bfloat16
F32 = jnp.float32
LOG2E = math.log2(math.e)
Q_SCALE = HEAD_DIM ** -0.5 * LOG2E


def _rms_scale(x):
    return lax.rsqrt(jnp.mean(x * x, axis=-1, keepdims=True) + RMS_EPS)


def _split3(x):
    hi = x.astype(BF16)
    r1 = x - hi.astype(F32)
    mid = r1.astype(BF16)
    lo = (r1 - mid.astype(F32)).astype(BF16)
    return hi, mid, lo


def _dot(a, b):
    return jnp.dot(a, b, preferred_element_type=F32)


def _dot_nt(a, b):
    return lax.dot_general(a, b, (((1,), (1,)), ((), ())), preferred_element_type=F32)


IN_TM = 1024
IN_TN = 1024
IN_ROT_TILES = 2 * WIDTH // IN_TN


def _permute_rotary_rows(blk):
    mid = HEAD_DIM // 2 + ROPE_HALF
    parts = []
    for hd in range(blk.shape[0] // HEAD_DIM):
        b = hd * HEAD_DIM
        parts += [blk[b:b + ROPE_HALF], blk[b + ROPE_DIM:b + mid],
                  blk[b + ROPE_HALF:b + ROPE_DIM], blk[b + mid:b + HEAD_DIM]]
    return jnp.concatenate(parts, axis=0)


IN_W_SLOTS = 3
IN_VMEM_LIMIT = 60 * 1024 * 1024


def _in_proj_kernel(x_ref, g_ref, wt_hbm, wff_ref, cos_ref, sin_ref,
                    proj_ref, ff_ref, kmean_ref, h_ref, wbuf_ref, wsem):
    j = pl.program_id(1)
    n_j = pl.num_programs(1)
    step = pl.program_id(0) * n_j + j
    n_steps = pl.num_programs(0) * n_j

    def window(s):
        col = s % n_j
        n_qkv = QKV_COLS // IN_TN
        row = pl.multiple_of(
            jnp.where(col < n_qkv, col * IN_TN, QKV_COLS + N_HEADS + (col - n_qkv) * IN_TN), SUBLANES)
        slot = s % IN_W_SLOTS
        return pltpu.make_async_copy(wt_hbm.at[pl.ds(row, IN_TN), :], wbuf_ref.at[slot],
                                     wsem.at[slot])

    @pl.when(step == 0)
    def _():
        for s in range(IN_W_SLOTS - 1):
            window(s).start()

    @pl.when(step + IN_W_SLOTS - 1 < n_steps)
    def _():
        window(step + IN_W_SLOTS - 1).start()

    window(step).wait()
    w_ref = wbuf_ref.at[step % IN_W_SLOTS]

    @pl.when(j == 0)
    def _():
        x = x_ref[...]
        h = (x * _rms_scale(x) * g_ref[...]).astype(BF16)
        h_ref[...] = h
        ff_ref[...] = _dot_nt(h, wff_ref[...].astype(BF16))

    def tile(permute):
        w = w_ref[...]
        w = (_permute_rotary_rows(w) if permute else w).astype(BF16)
        return _dot_nt(h_ref[...], w)

    def rotary(t):
        outs = []
        for hd in range(IN_TN // HEAD_DIM):
            th = t[:, hd * HEAD_DIM:(hd + 1) * HEAD_DIM]
            outs.append(th * cos_ref[...] + pltpu.roll(th, HEAD_DIM // 2, 1) * sin_ref[...])
        return jnp.concatenate(outs, axis=1)

    @pl.when(j == 0)
    def _():
        proj_ref[...] = (rotary(tile(True)) * Q_SCALE).astype(BF16)

    @pl.when(j == 1)
    def _():
        kr = rotary(tile(True))
        proj_ref[...] = kr.astype(BF16)
        nblk = IN_TM // MOBA_BLOCK
        means = [jnp.mean(kr[b * MOBA_BLOCK:(b + 1) * MOBA_BLOCK, :], axis=0, keepdims=True)
                 for b in range(nblk)]
        kmean_ref[0] = jnp.concatenate(means, axis=0)

    @pl.when(j == 3)
    def _():
        proj_ref[...] = (tile(False) * Q_SCALE).astype(BF16)

    @pl.when((j == 2) | (j >= 4))
    def _():
        proj_ref[...] = tile(False).astype(BF16)


def _in_proj(x2, g, w_in, cos_t, sin_t, seq):
    assert IN_ROT_TILES == 2
    n = x2.shape[0]
    seq_tiles = seq // IN_TM
    assert (QKV_COLS + N_HEADS) % SUBLANES == 0 and n // IN_TM * (PROJ_COLS // IN_TN) >= IN_W_SLOTS
    wt = w_in.T

    return pl.pallas_call(
        _in_proj_kernel,
        grid=(n // IN_TM, PROJ_COLS // IN_TN),
        in_specs=[
            pl.BlockSpec((IN_TM, D_MODEL), lambda i, j: (i, 0)),
            pl.BlockSpec((1, D_MODEL), lambda i, j: (0, 0)),
            pl.BlockSpec(memory_space=pl.ANY),
            pl.BlockSpec((LANES, D_MODEL), lambda i, j: (QKV_COLS // LANES, 0),
                         pipeline_mode=pl.Buffered(1)),
            pl.BlockSpec((IN_TM, HEAD_DIM), lambda i, j: (i % seq_tiles, 0)),
            pl.BlockSpec((IN_TM, HEAD_DIM), lambda i, j: (i % seq_tiles, 0)),
        ],
        out_specs=[
            pl.BlockSpec((IN_TM, IN_TN), lambda i, j: (i, j)),
            pl.BlockSpec((IN_TM, LANES), lambda i, j: (i, 0)),
            pl.BlockSpec((1, IN_TM // MOBA_BLOCK, WIDTH), lambda i, j: (i, 0, 0)),
        ],
        out_shape=[
            jax.ShapeDtypeStruct((n, PROJ_COLS), BF16),
            jax.ShapeDtypeStruct((n, LANES), F32),
            jax.ShapeDtypeStruct((n // IN_TM, IN_TM // MOBA_BLOCK, WIDTH), F32),
        ],
        scratch_shapes=[pltpu.VMEM((IN_TM, D_MODEL), BF16),
                        pltpu.VMEM((IN_W_SLOTS, IN_TN, D_MODEL), F32),
                        pltpu.SemaphoreType.DMA((IN_W_SLOTS,))],
        compiler_params=pltpu.CompilerParams(
            dimension_semantics=("arbitrary", "arbitrary"), vmem_limit_bytes=IN_VMEM_LIMIT),
        name="in_proj",
    )(x2, g, wt, wt, cos_t, sin_t)


FB_CHUNK = 256


def _fox_bias_kernel(ff_ref, b_ref, out_ref):
    seq = ff_ref.shape[0]
    r = lax.broadcasted_iota(jnp.int32, (FB_CHUNK, FB_CHUNK), 0)
    c = lax.broadcasted_iota(jnp.int32, (FB_CHUNK, FB_CHUNK), 1)
    ltri = (c <= r).astype(BF16)
    pr = lax.broadcasted_iota(jnp.int32, (LANES, LANES), 0)
    pc = lax.broadcasted_iota(jnp.int32, (LANES, LANES), 1)
    place = [((pc == 3 * pr + t) & (pr < N_HEADS)).astype(BF16) for t in range(3)]

    local = []
    for ci in range(seq // FB_CHUNK):
        z = ff_ref[ci * FB_CHUNK:(ci + 1) * FB_CHUNK, :] + b_ref[...]
        logf = -(jnp.maximum(-z, 0.0) + jnp.log1p(jnp.exp(-jnp.abs(z))))
        hi, mid, lo = _split3(logf)
        local.append(_dot(ltri, hi) + _dot(ltri, mid) + _dot(ltri, lo))
    carry = jnp.zeros((1, LANES), F32)
    for ci, loc in enumerate(local):
        f = loc + carry
        carry = carry + loc[FB_CHUNK - 1:FB_CHUNK, :]
        nh, nm, nl = _split3(f * (-LOG2E))
        aug = _dot(nh, place[0]) + _dot(nm, place[1]) + _dot(nl, place[2])
        out_ref[ci * FB_CHUNK:(ci + 1) * FB_CHUNK, :] = aug.astype(BF16)


def _fox_bias(ff, b_row, batch, seq):
    return pl.pallas_call(
        _fox_bias_kernel,
        grid=(batch,),
        in_specs=[
            pl.BlockSpec((seq, LANES), lambda b: (b, 0)),
            pl.BlockSpec((1, LANES), lambda b: (0, 0)),
        ],
        out_specs=pl.BlockSpec((seq, LANES), lambda b: (b, 0)),
        out_shape=jax.ShapeDtypeStruct((batch * seq, LANES), BF16),
        compiler_params=pltpu.CompilerParams(
            dimension_semantics=("arbitrary",), vmem_limit_bytes=VMEM_LIMIT),
        name="fox_bias",
    )(ff, b_row)


ATT_T = 512
ATT_HG = 4
ATT_TILES = 2
ATT_SUB = ATT_T // MOBA_BLOCK


def _attn_kernel(is_fox, n_cast, *refs):
    q_ref, k_ref, v_ref, x_ref = refs[:4]
    cast_src = refs[4:4 + n_cast]
    o_ref = refs[4 + n_cast]
    cast_dst = refs[5 + n_cast:5 + 2 * n_cast]
    vt_ref, acc_ref, sa_ref, sb_ref = refs[5 + 2 * n_cast:]
    hg = pl.program_id(1)
    pair = pl.program_id(2)
    seq = k_ref.shape[0]
    nblk = seq // MOBA_BLOCK
    t = ATT_T

    for src, dst in zip(cast_src, cast_dst):
        dst[...] = src[...].astype(BF16)

    def head_cols(hh):
        return slice(hh * HEAD_DIM, (hh + 1) * HEAD_DIM)

    def tile_rows(tile):
        return slice(tile * t, (tile + 1) * t)

    @pl.when(pair == 0)
    def _():
        for hh in range(ATT_HG):
            for ci in range(nblk):
                rows = slice(ci * MOBA_BLOCK, (ci + 1) * MOBA_BLOCK)
                blk = v_ref[rows, head_cols(hh)].astype(F32)
                vt_ref[head_cols(hh), rows] = blk.T.astype(BF16)

    def prep(tile, qt):
        qs = []
        for hh in range(ATT_HG):
            qh = q_ref[tile_rows(tile), head_cols(hh)]
            if is_fox:
                head = hg * ATT_HG + hh
                lane = lax.broadcasted_iota(jnp.int32, (t, LANES), 1)
                pick = ((lane >= 3 * head) & (lane < 3 * head + 3)).astype(BF16)
                qh = jnp.concatenate([qh, pick], axis=1)
            qs.append(qh)
        if not is_fox:
            blk_id = lax.broadcasted_iota(jnp.int32, (nblk, t), 0)
            q_blk = qt * ATT_SUB + lax.broadcasted_iota(jnp.int32, (nblk, t), 1) // MOBA_BLOCK
            past = blk_id < q_blk
            for hh in range(ATT_HG):
                kh, km, kl = _split3(x_ref[:, head_cols(hh)])
                gate = _dot_nt(kh, qs[hh]) + _dot_nt(km, qs[hh]) + _dot_nt(kl, qs[hh])
                gate = jnp.where(past, gate, -jnp.inf)
                rank = jnp.zeros((nblk, t), jnp.int32)
                for mth in range(nblk):
                    gm = gate[mth:mth + 1, :]
                    ahead = (gm > gate) | ((gm == gate) & (mth < blk_id))
                    rank = rank + ahead.astype(jnp.int32)
                visible = ((rank < MOBA_TOP_K) & past) | (blk_id == q_blk)
                bias = jnp.where(visible, 0.0, NEG_BIG)
                bias = jnp.concatenate([bias, jnp.zeros((LANES - nblk, t), F32)], axis=0)
                qs[hh] = jnp.concatenate([qs[hh], bias.T.astype(BF16)], axis=1)
        return qs

    def scores(qs, hh, start):
        kb = k_ref[pl.ds(start, t), head_cols(hh)]
        if is_fox:
            extra = x_ref[pl.ds(start, t), :]
        else:
            key_blk = start // MOBA_BLOCK + lax.broadcasted_iota(jnp.int32, (t, LANES), 0) // MOBA_BLOCK
            extra = (lax.broadcasted_iota(jnp.int32, (t, LANES), 1) == key_blk).astype(BF16)
        return _dot_nt(jnp.concatenate([kb, extra], axis=1), qs[hh])

    def vt_chunk(hh, start):
        return vt_ref[head_cols(hh), pl.ds(start, t)]

    def produce(qs, dst_ref, start, diagonal):
        cmax = []
        for hh in range(ATT_HG):
            s = scores(qs, hh, start)
            if diagonal:
                key = lax.broadcasted_iota(jnp.int32, (t, t), 0)
                qry = lax.broadcasted_iota(jnp.int32, (t, t), 1)
                s = jnp.where(key <= qry, s, NEG_BIG)
            dst_ref[hh] = s
            cmax.append(jnp.max(s, axis=0, keepdims=True))
        return tuple(cmax)

    def consume(tile, src_ref, start, cmax, stats):
        new = []
        for hh in range(ATT_HG):
            m, l = stats[hh]
            m_new = jnp.maximum(m, cmax[hh])
            alpha = jnp.exp2(m - m_new)
            p = jnp.exp2(src_ref[hh] - m_new)
            l = alpha * l + jnp.sum(p, axis=0, keepdims=True)
            acc_ref[tile, head_cols(hh), :] = (alpha * acc_ref[tile, head_cols(hh), :]
                                               + _dot(vt_chunk(hh, start), p.astype(BF16)))
            new.append((m_new, l))
        return tuple(new)

    def sweep(tile, qs, qt, even, odd, cmax, stats):
        def step(dst_ref, src_ref):
            def run(i, cmax, stats):
                nxt = pl.multiple_of(i * t, t)
                cur = pl.multiple_of(jnp.where(i == 0, qt, i - 1) * t, t)
                cmax_next = produce(qs, dst_ref, nxt, False)
                return cmax_next, consume(tile, src_ref, cur, cmax, stats)
            return run

        def body(i, carry):
            return lax.cond(i % 2 == 0, step(*even), step(*odd), i, *carry)

        return lax.fori_loop(0, qt, body, (cmax, stats))

    def finish(tile, src_ref, qt, cmax, stats):
        last = pl.multiple_of(jnp.maximum(qt - 1, 0) * t, t)
        stats = consume(tile, src_ref, last, cmax, stats)
        for hh in range(ATT_HG):
            _, l = stats[hh]
            o_ref[tile_rows(tile), head_cols(hh)] = (
                acc_ref[tile, head_cols(hh), :] * (1.0 / l)).T.astype(BF16)

    def fresh_stats():
        return tuple((jnp.full((1, t), NEG_BIG, F32), jnp.zeros((1, t), F32)) for _ in range(ATT_HG))

    def diagonal(tile, qs, qt, dst_ref):
        acc_ref[tile] = jnp.zeros((ATT_HG * HEAD_DIM, t), F32)
        return produce(qs, dst_ref, pl.multiple_of(qt * t, t), True)

    qt0 = pair * ATT_TILES
    qs0 = prep(0, qt0)
    cmax0 = diagonal(0, qs0, qt0, sa_ref)
    cmax0, stats0 = sweep(0, qs0, qt0, (sb_ref, sa_ref), (sa_ref, sb_ref), cmax0, fresh_stats())

    qt1 = qt0 + 1
    qs1 = prep(1, qt1)
    cmax1 = diagonal(1, qs1, qt1, sb_ref)
    finish(0, sa_ref, qt0, cmax0, stats0)
    cmax1, stats1 = sweep(1, qs1, qt1, (sa_ref, sb_ref), (sb_ref, sa_ref), cmax1, fresh_stats())
    finish(1, sa_ref, qt1, cmax1, stats1)


def _rider_specs(riders, n_steps, step_of):
    specs = []
    for w in riders:
        rows = w.shape[0] // n_steps
        assert rows * n_steps == w.shape[0] and rows % 16 == 0
        specs.append(pl.BlockSpec((rows, w.shape[1]), lambda *idx: (step_of(*idx), 0)))
    return specs


def _attention(proj, kmean, faug, moba_riders, batch, seq):
    assert ATT_TILES == 2
    n = batch * seq
    npair = seq // (ATT_TILES * ATT_T)
    nblk = seq // MOBA_BLOCK
    hw = ATT_HG * HEAD_DIM
    n_hg = N_HEADS // ATT_HG
    grid = (batch, n_hg, npair)
    params = pltpu.CompilerParams(
        dimension_semantics=("arbitrary", "arbitrary", "arbitrary"),
        vmem_limit_bytes=VMEM_LIMIT)

    def q_spec(col0):
        return pl.BlockSpec((ATT_TILES * ATT_T, hw), lambda b, h, i: (b * npair + i, col0 + h))

    def kv_spec(col0):
        return pl.BlockSpec((seq, hw), lambda b, h, i: (b, col0 + h))

    region = WIDTH // hw
    out_spec = pl.BlockSpec((ATT_TILES * ATT_T, hw), lambda b, h, i: (b * npair + i, h))
    out_shape = jax.ShapeDtypeStruct((n, WIDTH), BF16)
    common = [pltpu.VMEM((hw, seq), BF16), pltpu.VMEM((ATT_TILES, hw, ATT_T), F32),
              pltpu.VMEM((ATT_HG, ATT_T, ATT_T), F32), pltpu.VMEM((ATT_HG, ATT_T, ATT_T), F32)]

    def call(is_fox, name, specs, args, riders, extra_scratch):
        rider_specs = _rider_specs(riders, batch * n_hg * npair,
                                   lambda b, h, i: (b * n_hg + h) * npair + i)
        return pl.pallas_call(
            functools.partial(_attn_kernel, is_fox, len(riders)),
            grid=grid,
            in_specs=specs + rider_specs,
            out_specs=[out_spec] + rider_specs,
            out_shape=[out_shape] + [jax.ShapeDtypeStruct(w.shape, BF16) for w in riders],
            scratch_shapes=common + extra_scratch,
            compiler_params=params,
            name=name,
        )(*args, *riders)

    o_moba, *moba_cast = call(
        False, "moba_attn",
        [q_spec(0), kv_spec(region), kv_spec(2 * region),
         pl.BlockSpec((nblk, hw), lambda b, h, i: (b, h))],
        (proj, proj, proj, kmean), moba_riders, [])
    o_fox, = call(
        True, "fox_attn",
        [q_spec(3 * region), kv_spec(4 * region), kv_spec(5 * region),
         pl.BlockSpec((seq, LANES), lambda b, h, i: (b, 0))],
        (proj, proj, proj, faug), (), [])
    return o_moba, o_fox, moba_cast


MIX_TM = 256


def _sigmoid(z):
    return 1.0 / (1.0 + jnp.exp(-z))


def _mix_kernel(n_cast, om_ref, of_ref, ga_ref, gb_ref, x_ref, wm_ref, wf_ref, wo_ref, g_ref, *rest):
    cast_src = rest[:n_cast]
    out_ref = rest[n_cast]
    cast_dst = rest[n_cast + 1:]
    for src, dst in zip(cast_src, cast_dst):
        dst[...] = src[...].astype(BF16)
    y_m = _dot(om_ref[...], wm_ref[...])
    y_f = _dot(of_ref[...], wf_ref[...])
    merged = _sigmoid(ga_ref[...].astype(F32)) * y_m + _sigmoid(gb_ref[...].astype(F32)) * y_f
    mixed = _dot(merged.astype(BF16), wo_ref[...])
    out_ref[...] = x_ref[...] + mixed * _rms_scale(mixed) * g_ref[...]


def _mix_out(o_moba, o_fox, proj, x2, wm, wf, wo, g, riders):
    n = x2.shape[0]
    ga_blk = QKV_COLS // D_MODEL
    const = dict(pipeline_mode=pl.Buffered(1))
    rider_specs = _rider_specs(riders, n // MIX_TM, lambda i: i)
    x1, *cast = pl.pallas_call(
        functools.partial(_mix_kernel, len(riders)),
        grid=(n // MIX_TM,),
        in_specs=[
            pl.BlockSpec((MIX_TM, WIDTH), lambda i: (i, 0)),
            pl.BlockSpec((MIX_TM, WIDTH), lambda i: (i, 0)),
            pl.BlockSpec((MIX_TM, D_MODEL), lambda i: (i, ga_blk)),
            pl.BlockSpec((MIX_TM, D_MODEL), lambda i: (i, ga_blk + 1)),
            pl.BlockSpec((MIX_TM, D_MODEL), lambda i: (i, 0)),
            pl.BlockSpec((WIDTH, D_MODEL), lambda i: (0, 0), **const),
            pl.BlockSpec((WIDTH, D_MODEL), lambda i: (0, 0), **const),
            pl.BlockSpec((D_MODEL, D_MODEL), lambda i: (0, 0), **const),
            pl.BlockSpec((1, D_MODEL), lambda i: (0, 0)),
        ] + rider_specs,
        out_specs=[pl.BlockSpec((MIX_TM, D_MODEL), lambda i: (i, 0))] + rider_specs,
        out_shape=[jax.ShapeDtypeStruct((n, D_MODEL), F32)]
        + [jax.ShapeDtypeStruct(w.shape, BF16) for w in riders],
        compiler_params=pltpu.CompilerParams(
            dimension_semantics=("arbitrary",), vmem_limit_bytes=VMEM_LIMIT),
        name="mix_out",
    )(o_moba, o_fox, proj, proj, x2, wm, wf, wo, g, *riders)
    return x1, cast


MLP_TM = 512
MLP_TF = 1024


def _mlp_kernel(x_ref, gpre_ref, wu_ref, wd_ref, gpost_ref, out_ref, h_ref, acc_ref):
    f = pl.program_id(1)

    @pl.when(f == 0)
    def _():
        x = x_ref[...]
        h_ref[...] = (x * _rms_scale(x) * gpre_ref[...]).astype(BF16)
        acc_ref[...] = jnp.zeros_like(acc_ref)

    u = _dot(h_ref[...], wu_ref[...])
    a = jnp.square(jnp.maximum(u, 0.0)).astype(BF16)
    acc_ref[...] += _dot(a, wd_ref[...])

    @pl.when(f == pl.num_programs(1) - 1)
    def _():
        mo = acc_ref[...]
        out_ref[...] = x_ref[...] + mo * _rms_scale(mo) * gpost_ref[...]


def _mlp(x1, gpre, wu, wd, gpost):
    n = x1.shape[0]
    return pl.pallas_call(
        _mlp_kernel,
        grid=(n // MLP_TM, D_FF // MLP_TF),
        in_specs=[
            pl.BlockSpec((MLP_TM, D_MODEL), lambda i, f: (i, 0)),
            pl.BlockSpec((1, D_MODEL), lambda i, f: (0, 0)),
            pl.BlockSpec((D_MODEL, MLP_TF), lambda i, f: (0, f)),
            pl.BlockSpec((MLP_TF, D_MODEL), lambda i, f: (f, 0)),
            pl.BlockSpec((1, D_MODEL), lambda i, f: (0, 0)),
        ],
        out_specs=pl.BlockSpec((MLP_TM, D_MODEL), lambda i, f: (i, 0)),
        out_shape=jax.ShapeDtypeStruct((n, D_MODEL), F32),
        scratch_shapes=[pltpu.VMEM((MLP_TM, D_MODEL), BF16), pltpu.VMEM((MLP_TM, D_MODEL), F32)],
        compiler_params=pltpu.CompilerParams(
            dimension_semantics=("arbitrary", "arbitrary"), vmem_limit_bytes=VMEM_LIMIT),
        name="mlp",
    )(x1, gpre, wu, wd, gpost)


@functools.lru_cache(maxsize=None)
def _rope_tables(seq):
    f32 = np.float32
    inv_freq = np.power(f32(ROPE_THETA), -np.arange(0, ROPE_DIM, 2, dtype=f32) / f32(ROPE_DIM)).astype(f32)
    ang = np.arange(seq, dtype=f32)[:, None] * inv_freq[None, :]
    cos, sin = np.cos(ang).astype(f32), np.sin(ang).astype(f32)
    gap = HEAD_DIM // 2 - ROPE_HALF
    ones = np.ones((seq, gap), f32)
    zeros = np.zeros((seq, gap), f32)
    cos_t = np.concatenate([cos, ones, cos, ones], axis=1)
    sin_t = np.concatenate([-sin, zeros, sin, zeros], axis=1)
    return cos_t, sin_t


def _layer(x2, w_in, b_forget, w_bm, w_bf, w_out, g_mix_pre, g_mix_post,
           w_up, w_down, g_mlp_pre, g_mlp_post, batch, seq):
    b_row = jnp.pad(b_forget, (0, LANES - N_HEADS)).reshape(1, LANES)
    cos_t, sin_t = _rope_tables(seq)

    proj, ff, kmean = _in_proj(x2, g_mix_pre.reshape(1, -1), w_in, cos_t, sin_t, seq)
    faug = _fox_bias(ff, b_row, batch, seq)
    kmean = kmean.reshape(batch * (seq // MOBA_BLOCK), WIDTH)
    o_moba, o_fox, (wm, wf, wo) = _attention(proj, kmean, faug, (w_bm, w_bf, w_out), batch, seq)
    x1, (wu, wd) = _mix_out(o_moba, o_fox, proj, x2, wm, wf, wo, g_mix_post.reshape(1, -1),
                            (w_up, w_down))
    return _mlp(x1, g_mlp_pre.reshape(1, -1), wu, wd, g_mlp_post.reshape(1, -1))


def kernel(x, w_in, b_forget, w_branch_moba, w_branch_fox, w_out, g_mix_pre, g_mix_post,
           w_up, w_down, g_mlp_pre, g_mlp_post):
    batch, seq, d = x.shape
    assert d == D_MODEL and seq % IN_TM == 0 and seq % (ATT_TILES * ATT_T) == 0
    assert w_in.shape[-1] == PROJ_COLS + N_HEADS
    x2 = x.reshape(batch * seq, d)
    for l in range(w_in.shape[0]):
        x2 = _layer(x2, w_in[l], b_forget[l], w_branch_moba[l], w_branch_fox[l], w_out[l],
                    g_mix_pre[l], g_mix_post[l], w_up[l], w_down[l], g_mlp_pre[l],
                    g_mlp_post[l], batch, seq)
    return x2.reshape(batch, seq, d)
```

```python
import functools
import math

import numpy as np
import jax
import jax.numpy as jnp
from jax import lax
from jax.experimental import pallas as pl
from jax.experimental.pallas import tpu as pltpu

D_MODEL = 2048
HEAD_DIM = 128
N_HEADS = 8
WIDTH = N_HEADS * HEAD_DIM
MOBA_BLOCK = 256
MOBA_TOP_K = 3
ROPE_THETA = 500000.0
ROPE_DIM = HEAD_DIM // 4
ROPE_HALF = ROPE_DIM // 2
D_FF = 4 * D_MODEL
RMS_EPS = 1e-6
NEG_BIG = -1e30
LANES = 128
SUBLANES = 8

QKV_COLS = 6 * WIDTH
GATE_COLS = 2 * D_MODEL
PROJ_COLS = QKV_COLS + GATE_COLS

VMEM_LIMIT = 56 * 1024 * 1024

BF16 = jnp.bfloat16
F32 = jnp.float32
LOG2E = math.log2(math.e)
Q_SCALE = HEAD_DIM ** -0.5 * LOG2E


def _rms_scale(x):
    return lax.rsqrt(jnp.mean(x * x, axis=-1, keepdims=True) + RMS_EPS)


def _split3(x):
    hi = x.astype(BF16)
    r1 = x - hi.astype(F32)
    mid = r1.astype(BF16)
    lo = (r1 - mid.astype(F32)).astype(BF16)
    return hi, mid, lo


def _dot(a, b):
    return jnp.dot(a, b, preferred_element_type=F32)


def _dot_nt(a, b):
    return lax.dot_general(a, b, (((1,), (1,)), ((), ())), preferred_element_type=F32)


IN_TM = 1024
IN_TN = 1024
IN_ROT_TILES = 2 * WIDTH // IN_TN


def _permute_rotary_rows(blk):
    mid = HEAD_DIM // 2 + ROPE_HALF
    parts = []
    for hd in range(blk.shape[0] // HEAD_DIM):
        b = hd * HEAD_DIM
        parts += [blk[b:b + ROPE_HALF], blk[b + ROPE_DIM:b + mid],
                  blk[b + ROPE_HALF:b + ROPE_DIM], blk[b + mid:b + HEAD_DIM]]
    return jnp.concatenate(parts, axis=0)


IN_W_SLOTS = 3
IN_VMEM_LIMIT = 60 * 1024 * 1024


def _in_proj_kernel(x_ref, g_ref, wt_hbm, wff_ref, cos_ref, sin_ref,
                    proj_ref, ff_ref, kmean_ref, h_ref, wbuf_ref, wsem):
    j = pl.program_id(1)
    n_j = pl.num_programs(1)
    step = pl.program_id(0) * n_j + j
    n_steps = pl.num_programs(0) * n_j

    def window(s):
        col = s % n_j
        n_qkv = QKV_COLS // IN_TN
        row = pl.multiple_of(
            jnp.where(col < n_qkv, col * IN_TN, QKV_COLS + N_HEADS + (col - n_qkv) * IN_TN), SUBLANES)
        slot = s % IN_W_SLOTS
        return pltpu.make_async_copy(wt_hbm.at[pl.ds(row, IN_TN), :], wbuf_ref.at[slot],
                                     wsem.at[slot])

    @pl.when(step == 0)
    def _():
        for s in range(IN_W_SLOTS - 1):
            window(s).start()

    @pl.when(step + IN_W_SLOTS - 1 < n_steps)
    def _():
        window(step + IN_W_SLOTS - 1).start()

    window(step).wait()
    w_ref = wbuf_ref.at[step % IN_W_SLOTS]

    @pl.when(j == 0)
    def _():
        x = x_ref[...]
        h = (x * _rms_scale(x) * g_ref[...]).astype(BF16)
        h_ref[...] = h
        ff_ref[...] = _dot_nt(h, wff_ref[...].astype(BF16))

    def tile(permute):
        w = w_ref[...]
        w = (_permute_rotary_rows(w) if permute else w).astype(BF16)
        return _dot_nt(h_ref[...], w)

    def rotary(t):
        outs = []
        for hd in range(IN_TN // HEAD_DIM):
            th = t[:, hd * HEAD_DIM:(hd + 1) * HEAD_DIM]
            outs.append(th * cos_ref[...] + pltpu.roll(th, HEAD_DIM // 2, 1) * sin_ref[...])
        return jnp.concatenate(outs, axis=1)

    @pl.when(j == 0)
    def _():
        proj_ref[...] = (rotary(tile(True)) * Q_SCALE).astype(BF16)

    @pl.when(j == 1)
    def _():
        kr = rotary(tile(True))
        proj_ref[...] = kr.astype(BF16)
        nblk = IN_TM // MOBA_BLOCK
        means = [jnp.mean(kr[b * MOBA_BLOCK:(b + 1) * MOBA_BLOCK, :], axis=0, keepdims=True)
                 for b in range(nblk)]
        kmean_ref[0] = jnp.concatenate(means, axis=0)

    @pl.when(j == 3)
    def _():
        proj_ref[...] = (tile(False) * Q_SCALE).astype(BF16)

    @pl.when((j == 2) | (j >= 4))
    def _():
        proj_ref[...] = tile(False).astype(BF16)


def _in_proj(x2, g, w_in, cos_t, sin_t, seq):
    assert IN_ROT_TILES == 2
    n = x2.shape[0]
    seq_tiles = seq // IN_TM
    assert (QKV_COLS + N_HEADS) % SUBLANES == 0 and n // IN_TM * (PROJ_COLS // IN_TN) >= IN_W_SLOTS
    wt = w_in.T

    return pl.pallas_call(
        _in_proj_kernel,
        grid=(n // IN_TM, PROJ_COLS // IN_TN),
        in_specs=[
            pl.BlockSpec((IN_TM, D_MODEL), lambda i, j: (i, 0)),
            pl.BlockSpec((1, D_MODEL), lambda i, j: (0, 0)),
            pl.BlockSpec(memory_space=pl.ANY),
            pl.BlockSpec((LANES, D_MODEL), lambda i, j: (QKV_COLS // LANES, 0),
                         pipeline_mode=pl.Buffered(1)),
            pl.BlockSpec((IN_TM, HEAD_DIM), lambda i, j: (i % seq_tiles, 0)),
            pl.BlockSpec((IN_TM, HEAD_DIM), lambda i, j: (i % seq_tiles, 0)),
        ],
        out_specs=[
            pl.BlockSpec((IN_TM, IN_TN), lambda i, j: (i, j)),
            pl.BlockSpec((IN_TM, LANES), lambda i, j: (i, 0)),
            pl.BlockSpec((1, IN_TM // MOBA_BLOCK, WIDTH), lambda i, j: (i, 0, 0)),
        ],
        out_shape=[
            jax.ShapeDtypeStruct((n, PROJ_COLS), BF16),
            jax.ShapeDtypeStruct((n, LANES), F32),
            jax.ShapeDtypeStruct((n // IN_TM, IN_TM // MOBA_BLOCK, WIDTH), F32),
        ],
        scratch_shapes=[pltpu.VMEM((IN_TM, D_MODEL), BF16),
                        pltpu.VMEM((IN_W_SLOTS, IN_TN, D_MODEL), F32),
                        pltpu.SemaphoreType.DMA((IN_W_SLOTS,))],
        compiler_params=pltpu.CompilerParams(
            dimension_semantics=("arbitrary", "arbitrary"), vmem_limit_bytes=IN_VMEM_LIMIT),
        name="in_proj",
    )(x2, g, wt, wt, cos_t, sin_t)


FB_CHUNK = 256


def _fox_bias_kernel(ff_ref, b_ref, out_ref):
    seq = ff_ref.shape[0]
    r = lax.broadcasted_iota(jnp.int32, (FB_CHUNK, FB_CHUNK), 0)
    c = lax.broadcasted_iota(jnp.int32, (FB_CHUNK, FB_CHUNK), 1)
    ltri = (c <= r).astype(BF16)
    pr = lax.broadcasted_iota(jnp.int32, (LANES, LANES), 0)
    pc = lax.broadcasted_iota(jnp.int32, (LANES, LANES), 1)
    place = [((pc == 3 * pr + t) & (pr < N_HEADS)).astype(BF16) for t in range(3)]

    local = []
    for ci in range(seq // FB_CHUNK):
        z = ff_ref[ci * FB_CHUNK:(ci + 1) * FB_CHUNK, :] + b_ref[...]
        logf = -(jnp.maximum(-z, 0.0) + jnp.log1p(jnp.exp(-jnp.abs(z))))
        hi, mid, lo = _split3(logf)
        local.append(_dot(ltri, hi) + _dot(ltri, mid) + _dot(ltri, lo))
    carry = jnp.zeros((1, LANES), F32)
    for ci, loc in enumerate(local):
        f = loc + carry
        carry = carry + loc[FB_CHUNK - 1:FB_CHUNK, :]
        nh, nm, nl = _split3(f * (-LOG2E))
        aug = _dot(nh, place[0]) + _dot(nm, place[1]) + _dot(nl, place[2])
        out_ref[ci * FB_CHUNK:(ci + 1) * FB_CHUNK, :] = aug.astype(BF16)


def _fox_bias(ff, b_row, batch, seq):
    return pl.pallas_call(
        _fox_bias_kernel,
        grid=(batch,),
        in_specs=[
            pl.BlockSpec((seq, LANES), lambda b: (b, 0)),
            pl.BlockSpec((1, LANES), lambda b: (0, 0)),
        ],
        out_specs=pl.BlockSpec((seq, LANES), lambda b: (b, 0)),
        out_shape=jax.ShapeDtypeStruct((batch * seq, LANES), BF16),
        compiler_params=pltpu.CompilerParams(
            dimension_semantics=("arbitrary",), vmem_limit_bytes=VMEM_LIMIT),
        name="fox_bias",
    )(ff, b_row)


ATT_T = 512
ATT_HG = 4
ATT_TILES = 2
ATT_SUB = ATT_T // MOBA_BLOCK


def _attn_kernel(is_fox, n_cast, *refs):
    q_ref, k_ref, v_ref, x_ref = refs[:4]
    cast_src = refs[4:4 + n_cast]
    o_ref = refs[4 + n_cast]
    cast_dst = refs[5 + n_cast:5 + 2 * n_cast]
    vt_ref, acc_ref, sa_ref, sb_ref = refs[5 + 2 * n_cast:]
    hg = pl.program_id(1)
    pair = pl.program_id(2)
    seq = k_ref.shape[0]
    nblk = seq // MOBA_BLOCK
    t = ATT_T

    for src, dst in zip(cast_src, cast_dst):
        dst[...] = src[...].astype(BF16)

    def head_cols(hh):
        return slice(hh * HEAD_DIM, (hh + 1) * HEAD_DIM)

    def tile_rows(tile):
        return slice(tile * t, (tile + 1) * t)

    @pl.when(pair == 0)
    def _():
        for hh in range(ATT_HG):
            for ci in range(nblk):
                rows = slice(ci * MOBA_BLOCK, (ci + 1) * MOBA_BLOCK)
                blk = v_ref[rows, head_cols(hh)].astype(F32)
                vt_ref[head_cols(hh), rows] = blk.T.astype(BF16)

    def prep(tile, qt):
        qs = []
        for hh in range(ATT_HG):
            qh = q_ref[tile_rows(tile), head_cols(hh)]
            if is_fox:
                head = hg * ATT_HG + hh
                lane = lax.broadcasted_iota(jnp.int32, (t, LANES), 1)
                pick = ((lane >= 3 * head) & (lane < 3 * head + 3)).astype(BF16)
                qh = jnp.concatenate([qh, pick], axis=1)
            qs.append(qh)
        if not is_fox:
            blk_id = lax.broadcasted_iota(jnp.int32, (nblk, t), 0)
            q_blk = qt * ATT_SUB + lax.broadcasted_iota(jnp.int32, (nblk, t), 1) // MOBA_BLOCK
            past = blk_id < q_blk
            blk_f = blk_id.astype(F32)
            for hh in range(ATT_HG):
                kh, km, kl = _split3(x_ref[:, head_cols(hh)])
                gate = _dot_nt(kh, qs[hh]) + _dot_nt(km, qs[hh]) + _dot_nt(kl, qs[hh])
                gate = jnp.where(past, gate, -jnp.inf)
                picked = jnp.zeros((nblk, t), jnp.bool_)
                for _ in range(MOBA_TOP_K):
                    top = jnp.max(gate, axis=0, keepdims=True)
                    first = jnp.min(jnp.where(gate == top, blk_f, float(nblk)), axis=0, keepdims=True)
                    pick = blk_f == first
                    picked = picked | pick
                    gate = jnp.where(pick, -jnp.inf, gate)
                visible = (picked & past) | (blk_id == q_blk)
                bias = jnp.where(visible, 0.0, NEG_BIG)
                bias = jnp.concatenate([bias, jnp.zeros((LANES - nblk, t), F32)], axis=0)
                qs[hh] = jnp.concatenate([qs[hh], bias.T.astype(BF16)], axis=1)
        return qs

    def scores(qs, hh, start):
        kb = k_ref[pl.ds(start, t), head_cols(hh)]
        if is_fox:
            extra = x_ref[pl.ds(start, t), :]
        else:
            key_blk = start // MOBA_BLOCK + lax.broadcasted_iota(jnp.int32, (t, LANES), 0) // MOBA_BLOCK
            extra = (lax.broadcasted_iota(jnp.int32, (t, LANES), 1) == key_blk).astype(BF16)
        return _dot_nt(jnp.concatenate([kb, extra], axis=1), qs[hh])

    def vt_chunk(hh, start):
        return vt_ref[head_cols(hh), pl.ds(start, t)]

    def produce(qs, dst_ref, start, diagonal):
        cmax = []
        for hh in range(ATT_HG):
            s = scores(qs, hh, start)
            if diagonal:
                key = lax.broadcasted_iota(jnp.int32, (t, t), 0)
                qry = lax.broadcasted_iota(jnp.int32, (t, t), 1)
                s = jnp.where(key <= qry, s, NEG_BIG)
            dst_ref[hh] = s
            cmax.append(jnp.max(s, axis=0, keepdims=True))
        return tuple(cmax)

    def consume(tile, src_ref, start, cmax, stats):
        new = []
        for hh in range(ATT_HG):
            m, l = stats[hh]
            m_new = jnp.maximum(m, cmax[hh])
            alpha = jnp.exp2(m - m_new)
            p = jnp.exp2(src_ref[hh] - m_new)
            l = alpha * l + jnp.sum(p, axis=0, keepdims=True)
            acc_ref[tile, head_cols(hh), :] = (alpha * acc_ref[tile, head_cols(hh), :]
                                               + _dot(vt_chunk(hh, start), p.astype(BF16)))
            new.append((m_new, l))
        return tuple(new)

    def sweep(tile, qs, qt, even, odd, cmax, stats):
        def step(dst_ref, src_ref):
            def run(i, cmax, stats):
                nxt = pl.multiple_of(i * t, t)
                cur = pl.multiple_of(jnp.where(i == 0, qt, i - 1) * t, t)
                cmax_next = produce(qs, dst_ref, nxt, False)
                return cmax_next, consume(tile, src_ref, cur, cmax, stats)
            return run

        def body(i, carry):
            return lax.cond(i % 2 == 0, step(*even), step(*odd), i, *carry)

        return lax.fori_loop(0, qt, body, (cmax, stats))

    def finish(tile, src_ref, qt, cmax, stats):
        last = pl.multiple_of(jnp.maximum(qt - 1, 0) * t, t)
        stats = consume(tile, src_ref, last, cmax, stats)
        for hh in range(ATT_HG):
            _, l = stats[hh]
            o_ref[tile_rows(tile), head_cols(hh)] = (
                acc_ref[tile, head_cols(hh), :] * (1.0 / l)).T.astype(BF16)

    def fresh_stats():
        return tuple((jnp.full((1, t), NEG_BIG, F32), jnp.zeros((1, t), F32)) for _ in range(ATT_HG))

    def diagonal(tile, qs, qt, dst_ref):
        acc_ref[tile] = jnp.zeros((ATT_HG * HEAD_DIM, t), F32)
        return produce(qs, dst_ref, pl.multiple_of(qt * t, t), True)

    qt0 = pair * ATT_TILES
    qs0 = prep(0, qt0)
    cmax0 = diagonal(0, qs0, qt0, sa_ref)
    cmax0, stats0 = sweep(0, qs0, qt0, (sb_ref, sa_ref), (sa_ref, sb_ref), cmax0, fresh_stats())

    qt1 = qt0 + 1
    qs1 = prep(1, qt1)
    cmax1 = diagonal(1, qs1, qt1, sb_ref)
    finish(0, sa_ref, qt0, cmax0, stats0)
    cmax1, stats1 = sweep(1, qs1, qt1, (sa_ref, sb_ref), (sb_ref, sa_ref), cmax1, fresh_stats())
    finish(1, sa_ref, qt1, cmax1, stats1)


def _rider_specs(riders, n_steps, step_of):
    specs = []
    for w in riders:
        rows = w.shape[0] // n_steps
        assert rows * n_steps == w.shape[0] and rows % 16 == 0
        specs.append(pl.BlockSpec((rows, w.shape[1]), lambda *idx: (step_of(*idx), 0)))
    return specs


def _attention(proj, kmean, faug, moba_riders, batch, seq):
    assert ATT_TILES == 2
    n = batch * seq
    npair = seq // (ATT_TILES * ATT_T)
    nblk = seq // MOBA_BLOCK
    hw = ATT_HG * HEAD_DIM
    n_hg = N_HEADS // ATT_HG
    grid = (batch, n_hg, npair)
    params = pltpu.CompilerParams(
        dimension_semantics=("arbitrary", "arbitrary", "arbitrary"),
        vmem_limit_bytes=VMEM_LIMIT)

    def q_spec(col0):
        return pl.BlockSpec((ATT_TILES * ATT_T, hw), lambda b, h, i: (b * npair + i, col0 + h))

    def kv_spec(col0):
        return pl.BlockSpec((seq, hw), lambda b, h, i: (b, col0 + h))

    region = WIDTH // hw
    out_spec = pl.BlockSpec((ATT_TILES * ATT_T, hw), lambda b, h, i: (b * npair + i, h))
    out_shape = jax.ShapeDtypeStruct((n, WIDTH), BF16)
    common = [pltpu.VMEM((hw, seq), BF16), pltpu.VMEM((ATT_TILES, hw, ATT_T), F32),
              pltpu.VMEM((ATT_HG, ATT_T, ATT_T), F32), pltpu.VMEM((ATT_HG, ATT_T, ATT_T), F32)]

    def call(is_fox, name, specs, args, riders, extra_scratch):
        rider_specs = _rider_specs(riders, batch * n_hg * npair,
                                   lambda b, h, i: (b * n_hg + h) * npair + i)
        return pl.pallas_call(
            functools.partial(_attn_kernel, is_fox, len(riders)),
            grid=grid,
            in_specs=specs + rider_specs,
            out_specs=[out_spec] + rider_specs,
            out_shape=[out_shape] + [jax.ShapeDtypeStruct(w.shape, BF16) for w in riders],
            scratch_shapes=common + extra_scratch,
            compiler_params=params,
            name=name,
        )(*args, *riders)

    o_moba, *moba_cast = call(
        False, "moba_attn",
        [q_spec(0), kv_spec(region), kv_spec(2 * region),
         pl.BlockSpec((nblk, hw), lambda b, h, i: (b, h))],
        (proj, proj, proj, kmean), moba_riders, [])
    o_fox, = call(
        True, "fox_attn",
        [q_spec(3 * region), kv_spec(4 * region), kv_spec(5 * region),
         pl.BlockSpec((seq, LANES), lambda b, h, i: (b, 0))],
        (proj, proj, proj, faug), (), [])
    return o_moba, o_fox, moba_cast


MIX_TM = 256


def _sigmoid(z):
    return 1.0 / (1.0 + jnp.exp(-z))


def _mix_kernel(n_cast, om_ref, of_ref, ga_ref, gb_ref, x_ref, wm_ref, wf_ref, wo_ref, g_ref, *rest):
    cast_src = rest[:n_cast]
    out_ref = rest[n_cast]
    cast_dst = rest[n_cast + 1:]
    for src, dst in zip(cast_src, cast_dst):
        dst[...] = src[...].astype(BF16)
    y_m = _dot(om_ref[...], wm_ref[...])
    y_f = _dot(of_ref[...], wf_ref[...])
    merged = _sigmoid(ga_ref[...].astype(F32)) * y_m + _sigmoid(gb_ref[...].astype(F32)) * y_f
    mixed = _dot(merged.astype(BF16), wo_ref[...])
    out_ref[...] = x_ref[...] + mixed * _rms_scale(mixed) * g_ref[...]


def _mix_out(o_moba, o_fox, proj, x2, wm, wf, wo, g, riders):
    n = x2.shape[0]
    ga_blk = QKV_COLS // D_MODEL
    const = dict(pipeline_mode=pl.Buffered(1))
    rider_specs = _rider_specs(riders, n // MIX_TM, lambda i: i)
    x1, *cast = pl.pallas_call(
        functools.partial(_mix_kernel, len(riders)),
        grid=(n // MIX_TM,),
        in_specs=[
            pl.BlockSpec((MIX_TM, WIDTH), lambda i: (i, 0)),
            pl.BlockSpec((MIX_TM, WIDTH), lambda i: (i, 0)),
            pl.BlockSpec((MIX_TM, D_MODEL), lambda i: (i, ga_blk)),
            pl.BlockSpec((MIX_TM, D_MODEL), lambda i: (i, ga_blk + 1)),
            pl.BlockSpec((MIX_TM, D_MODEL), lambda i: (i, 0)),
            pl.BlockSpec((WIDTH, D_MODEL), lambda i: (0, 0), **const),
            pl.BlockSpec((WIDTH, D_MODEL), lambda i: (0, 0), **const),
            pl.BlockSpec((D_MODEL, D_MODEL), lambda i: (0, 0), **const),
            pl.BlockSpec((1, D_MODEL), lambda i: (0, 0)),
        ] + rider_specs,
        out_specs=[pl.BlockSpec((MIX_TM, D_MODEL), lambda i: (i, 0))] + rider_specs,
        out_shape=[jax.ShapeDtypeStruct((n, D_MODEL), F32)]
        + [jax.ShapeDtypeStruct(w.shape, BF16) for w in riders],
        compiler_params=pltpu.CompilerParams(
            dimension_semantics=("arbitrary",), vmem_limit_bytes=VMEM_LIMIT),
        name="mix_out",
    )(o_moba, o_fox, proj, proj, x2, wm, wf, wo, g, *riders)
    return x1, cast


MLP_TM = 512
MLP_TF = 1024


def _mlp_kernel(x_ref, gpre_ref, wu_ref, wd_ref, gpost_ref, out_ref, h_ref, acc_ref):
    f = pl.program_id(1)

    @pl.when(f == 0)
    def _():
        x = x_ref[...]
        h_ref[...] = (x * _rms_scale(x) * gpre_ref[...]).astype(BF16)
        acc_ref[...] = jnp.zeros_like(acc_ref)

    u = _dot(h_ref[...], wu_ref[...])
    a = jnp.square(jnp.maximum(u, 0.0)).astype(BF16)
    acc_ref[...] += _dot(a, wd_ref[...])

    @pl.when(f == pl.num_programs(1) - 1)
    def _():
        mo = acc_ref[...]
        out_ref[...] = x_ref[...] + mo * _rms_scale(mo) * gpost_ref[...]


def _mlp(x1, gpre, wu, wd, gpost):
    n = x1.shape[0]
    return pl.pallas_call(
        _mlp_kernel,
        grid=(n // MLP_TM, D_FF // MLP_TF),
        in_specs=[
            pl.BlockSpec((MLP_TM, D_MODEL), lambda i, f: (i, 0)),
            pl.BlockSpec((1, D_MODEL), lambda i, f: (0, 0)),
            pl.BlockSpec((D_MODEL, MLP_TF), lambda i, f: (0, f)),
            pl.BlockSpec((MLP_TF, D_MODEL), lambda i, f: (f, 0)),
            pl.BlockSpec((1, D_MODEL), lambda i, f: (0, 0)),
        ],
        out_specs=pl.BlockSpec((MLP_TM, D_MODEL), lambda i, f: (i, 0)),
        out_shape=jax.ShapeDtypeStruct((n, D_MODEL), F32),
        scratch_shapes=[pltpu.VMEM((MLP_TM, D_MODEL), BF16), pltpu.VMEM((MLP_TM, D_MODEL), F32)],
        compiler_params=pltpu.CompilerParams(
            dimension_semantics=("arbitrary", "arbitrary"), vmem_limit_bytes=VMEM_LIMIT),
        name="mlp",
    )(x1, gpre, wu, wd, gpost)


@functools.lru_cache(maxsize=None)
def _rope_tables(seq):
    f32 = np.float32
    inv_freq = np.power(f32(ROPE_THETA), -np.arange(0, ROPE_DIM, 2, dtype=f32) / f32(ROPE_DIM)).astype(f32)
    ang = np.arange(seq, dtype=f32)[:, None] * inv_freq[None, :]
    cos, sin = np.cos(ang).astype(f32), np.sin(ang).astype(f32)
    gap = HEAD_DIM // 2 - ROPE_HALF
    ones = np.ones((seq, gap), f32)
    zeros = np.zeros((seq, gap), f32)
    cos_t = np.concatenate([cos, ones, cos, ones], axis=1)
    sin_t = np.concatenate([-sin, zeros, sin, zeros], axis=1)
    return cos_t, sin_t


def _layer(x2, w_in, b_forget, w_bm, w_bf, w_out, g_mix_pre, g_mix_post,
           w_up, w_down, g_mlp_pre, g_mlp_post, batch, seq):
    b_row = jnp.pad(b_forget, (0, LANES - N_HEADS)).reshape(1, LANES)
    cos_t, sin_t = _rope_tables(seq)

    proj, ff, kmean = _in_proj(x2, g_mix_pre.reshape(1, -1), w_in, cos_t, sin_t, seq)
    faug = _fox_bias(ff, b_row, batch, seq)
    kmean = kmean.reshape(batch * (seq // MOBA_BLOCK), WIDTH)
    o_moba, o_fox, (wm, wf, wo) = _attention(proj, kmean, faug, (w_bm, w_bf, w_out), batch, seq)
    x1, (wu, wd) = _mix_out(o_moba, o_fox, proj, x2, wm, wf, wo, g_mix_post.reshape(1, -1),
                            (w_up, w_down))
    return _mlp(x1, g_mlp_pre.reshape(1, -1), wu, wd, g_mlp_post.reshape(1, -1))


def kernel(x, w_in, b_forget, w_branch_moba, w_branch_fox, w_out, g_mix_pre, g_mix_post,
           w_up, w_down, g_mlp_pre, g_mlp_post):
    batch, seq, d = x.shape
    assert d == D_MODEL and seq % IN_TM == 0 and seq % (ATT_TILES * ATT_T) == 0
    assert w_in.shape[-1] == PROJ_COLS + N_HEADS
    x2 = x.reshape(batch * seq, d)
    for l in range(w_in.shape[0]):
        x2 = _layer(x2, w_in[l], b_forget[l], w_branch_moba[l], w_branch_fox[l], w_out[l],
                    g_mix_pre[l], g_mix_post[l], w_up[l], w_down[l], g_mlp_pre[l],
                    g_mlp_post[l], batch, seq)
    return x2.reshape(batch, seq, d)
```

```python
import functools
import math

import numpy as np
import jax
import jax.numpy as jnp
from jax import lax
from jax.experimental import pallas as pl
from jax.experimental.pallas import tpu as pltpu

D_MODEL = 2048
HEAD_DIM = 128
N_HEADS = 8
WIDTH = N_HEADS * HEAD_DIM
MOBA_BLOCK = 256
MOBA_TOP_K = 3
ROPE_THETA = 500000.0
ROPE_DIM = HEAD_DIM // 4
ROPE_HALF = ROPE_DIM // 2
D_FF = 4 * D_MODEL
RMS_EPS = 1e-6
NEG_BIG = -1e30
LANES = 128
SUBLANES = 8

QKV_COLS = 6 * WIDTH
GATE_COLS = 2 * D_MODEL
PROJ_COLS = QKV_COLS + GATE_COLS

VMEM_LIMIT = 56 * 1024 * 1024

BF16 = jnp.bfloat16
F32 = jnp.float32
LOG2E = math.log2(math.e)
Q_SCALE = HEAD_DIM ** -0.5 * LOG2E


def _rms_scale(x):
    return lax.rsqrt(jnp.mean(x * x, axis=-1, keepdims=True) + RMS_EPS)


def _split3(x):
    hi = x.astype(BF16)
    r1 = x - hi.astype(F32)
    mid = r1.astype(BF16)
    lo = (r1 - mid.astype(F32)).astype(BF16)
    return hi, mid, lo


def _dot(a, b):
    return jnp.dot(a, b, preferred_element_type=F32)


def _dot_nt(a, b):
    return lax.dot_general(a, b, (((1,), (1,)), ((), ())), preferred_element_type=F32)


IN_TM = 1024
IN_TN = 1024
IN_ROT_TILES = 2 * WIDTH // IN_TN


def _permute_rotary_rows(blk):
    mid = HEAD_DIM // 2 + ROPE_HALF
    parts = []
    for hd in range(blk.shape[0] // HEAD_DIM):
        b = hd * HEAD_DIM
        parts += [blk[b:b + ROPE_HALF], blk[b + ROPE_DIM:b + mid],
                  blk[b + ROPE_HALF:b + ROPE_DIM], blk[b + mid:b + HEAD_DIM]]
    return jnp.concatenate(parts, axis=0)


IN_W_SLOTS = 3
IN_VMEM_LIMIT = 60 * 1024 * 1024


def _in_proj_kernel(x_ref, g_ref, wt_hbm, wff_ref, cos_ref, sin_ref,
                    proj_ref, ff_ref, kmean_ref, h_ref, wbuf_ref, wsem):
    j = pl.program_id(1)
    n_j = pl.num_programs(1)
    step = pl.program_id(0) * n_j + j
    n_steps = pl.num_programs(0) * n_j

    def window(s):
        col = s % n_j
        n_qkv = QKV_COLS // IN_TN
        row = pl.multiple_of(
            jnp.where(col < n_qkv, col * IN_TN, QKV_COLS + N_HEADS + (col - n_qkv) * IN_TN), SUBLANES)
        slot = s % IN_W_SLOTS
        return pltpu.make_async_copy(wt_hbm.at[pl.ds(row, IN_TN), :], wbuf_ref.at[slot],
                                     wsem.at[slot])

    @pl.when(step == 0)
    def _():
        for s in range(IN_W_SLOTS - 1):
            window(s).start()

    @pl.when(step + IN_W_SLOTS - 1 < n_steps)
    def _():
        window(step + IN_W_SLOTS - 1).start()

    window(step).wait()
    w_ref = wbuf_ref.at[step % IN_W_SLOTS]

    @pl.when(j == 0)
    def _():
        x = x_ref[...]
        h = (x * _rms_scale(x) * g_ref[...]).astype(BF16)
        h_ref[...] = h
        ff_ref[...] = _dot_nt(h, wff_ref[...].astype(BF16))

    def tile(permute):
        w = w_ref[...]
        w = (_permute_rotary_rows(w) if permute else w).astype(BF16)
        return _dot_nt(h_ref[...], w)

    def rotary(t):
        outs = []
        for hd in range(IN_TN // HEAD_DIM):
            th = t[:, hd * HEAD_DIM:(hd + 1) * HEAD_DIM]
            outs.append(th * cos_ref[...] + pltpu.roll(th, HEAD_DIM // 2, 1) * sin_ref[...])
        return jnp.concatenate(outs, axis=1)

    @pl.when(j == 0)
    def _():
        proj_ref[...] = (rotary(tile(True)) * Q_SCALE).astype(BF16)

    @pl.when(j == 1)
    def _():
        kr = rotary(tile(True))
        proj_ref[...] = kr.astype(BF16)
        nblk = IN_TM // MOBA_BLOCK
        means = [jnp.mean(kr[b * MOBA_BLOCK:(b + 1) * MOBA_BLOCK, :], axis=0, keepdims=True)
                 for b in range(nblk)]
        kmean_ref[0] = jnp.concatenate(means, axis=0)

    @pl.when(j == 3)
    def _():
        proj_ref[...] = (tile(False) * Q_SCALE).astype(BF16)

    @pl.when((j == 2) | (j >= 4))
    def _():
        proj_ref[...] = tile(False).astype(BF16)


def _in_proj(x2, g, w_in, cos_t, sin_t, seq):
    assert IN_ROT_TILES == 2
    n = x2.shape[0]
    seq_tiles = seq // IN_TM
    assert (QKV_COLS + N_HEADS) % SUBLANES == 0 and n // IN_TM * (PROJ_COLS // IN_TN) >= IN_W_SLOTS
    wt = w_in.T

    return pl.pallas_call(
        _in_proj_kernel,
        grid=(n // IN_TM, PROJ_COLS // IN_TN),
        in_specs=[
            pl.BlockSpec((IN_TM, D_MODEL), lambda i, j: (i, 0)),
            pl.BlockSpec((1, D_MODEL), lambda i, j: (0, 0)),
            pl.BlockSpec(memory_space=pl.ANY),
            pl.BlockSpec((LANES, D_MODEL), lambda i, j: (QKV_COLS // LANES, 0),
                         pipeline_mode=pl.Buffered(1)),
            pl.BlockSpec((IN_TM, HEAD_DIM), lambda i, j: (i % seq_tiles, 0)),
            pl.BlockSpec((IN_TM, HEAD_DIM), lambda i, j: (i % seq_tiles, 0)),
        ],
        out_specs=[
            pl.BlockSpec((IN_TM, IN_TN), lambda i, j: (i, j)),
            pl.BlockSpec((IN_TM, LANES), lambda i, j: (i, 0)),
            pl.BlockSpec((1, IN_TM // MOBA_BLOCK, WIDTH), lambda i, j: (i, 0, 0)),
        ],
        out_shape=[
            jax.ShapeDtypeStruct((n, PROJ_COLS), BF16),
            jax.ShapeDtypeStruct((n, LANES), F32),
            jax.ShapeDtypeStruct((n // IN_TM, IN_TM // MOBA_BLOCK, WIDTH), F32),
        ],
        scratch_shapes=[pltpu.VMEM((IN_TM, D_MODEL), BF16),
                        pltpu.VMEM((IN_W_SLOTS, IN_TN, D_MODEL), F32),
                        pltpu.SemaphoreType.DMA((IN_W_SLOTS,))],
        compiler_params=pltpu.CompilerParams(
            dimension_semantics=("arbitrary", "arbitrary"), vmem_limit_bytes=IN_VMEM_LIMIT),
        name="in_proj",
    )(x2, g, wt, wt, cos_t, sin_t)


FB_CHUNK = 256


def _fox_bias_kernel(ff_ref, b_ref, out_ref):
    seq = ff_ref.shape[0]
    r = lax.broadcasted_iota(jnp.int32, (FB_CHUNK, FB_CHUNK), 0)
    c = lax.broadcasted_iota(jnp.int32, (FB_CHUNK, FB_CHUNK), 1)
    ltri = (c <= r).astype(BF16)
    pr = lax.broadcasted_iota(jnp.int32, (LANES, LANES), 0)
    pc = lax.broadcasted_iota(jnp.int32, (LANES, LANES), 1)
    place = [((pc == 3 * pr + t) & (pr < N_HEADS)).astype(BF16) for t in range(3)]

    local = []
    for ci in range(seq // FB_CHUNK):
        z = ff_ref[ci * FB_CHUNK:(ci + 1) * FB_CHUNK, :] + b_ref[...]
        logf = -(jnp.maximum(-z, 0.0) + jnp.log1p(jnp.exp(-jnp.abs(z))))
        hi, mid, lo = _split3(logf)
        local.append(_dot(ltri, hi) + _dot(ltri, mid) + _dot(ltri, lo))
    carry = jnp.zeros((1, LANES), F32)
    for ci, loc in enumerate(local):
        f = loc + carry
        carry = carry + loc[FB_CHUNK - 1:FB_CHUNK, :]
        nh, nm, nl = _split3(f * (-LOG2E))
        aug = _dot(nh, place[0]) + _dot(nm, place[1]) + _dot(nl, place[2])
        out_ref[ci * FB_CHUNK:(ci + 1) * FB_CHUNK, :] = aug.astype(BF16)


def _fox_bias(ff, b_row, batch, seq):
    return pl.pallas_call(
        _fox_bias_kernel,
        grid=(batch,),
        in_specs=[
            pl.BlockSpec((seq, LANES), lambda b: (b, 0)),
            pl.BlockSpec((1, LANES), lambda b: (0, 0)),
        ],
        out_specs=pl.BlockSpec((seq, LANES), lambda b: (b, 0)),
        out_shape=jax.ShapeDtypeStruct((batch * seq, LANES), BF16),
        compiler_params=pltpu.CompilerParams(
            dimension_semantics=("arbitrary",), vmem_limit_bytes=VMEM_LIMIT),
        name="fox_bias",
    )(ff, b_row)


ATT_T = 512
ATT_HG = 4
ATT_TILES = 2
ATT_SUB = ATT_T // MOBA_BLOCK
ATT_VROWS = HEAD_DIM + 16


def _attn_kernel(is_fox, n_cast, *refs):
    q_ref, k_ref, v_ref, x_ref = refs[:4]
    cast_src = refs[4:4 + n_cast]
    o_ref = refs[4 + n_cast]
    cast_dst = refs[5 + n_cast:5 + 2 * n_cast]
    vt_ref, acc_ref, sa_ref, sb_ref = refs[5 + 2 * n_cast:]
    hg = pl.program_id(1)
    pair = pl.program_id(2)
    seq = k_ref.shape[0]
    nblk = seq // MOBA_BLOCK
    t = ATT_T

    for src, dst in zip(cast_src, cast_dst):
        dst[...] = src[...].astype(BF16)

    def head_cols(hh):
        return slice(hh * HEAD_DIM, (hh + 1) * HEAD_DIM)

    def tile_rows(tile):
        return slice(tile * t, (tile + 1) * t)

    def head_rows(hh, n=ATT_VROWS):
        return slice(hh * ATT_VROWS, hh * ATT_VROWS + n)

    @pl.when(pair == 0)
    def _():
        ones_row = (lax.broadcasted_iota(jnp.int32, (ATT_VROWS - HEAD_DIM, seq), 0) == 0).astype(BF16)
        for hh in range(ATT_HG):
            for ci in range(nblk):
                rows = slice(ci * MOBA_BLOCK, (ci + 1) * MOBA_BLOCK)
                blk = v_ref[rows, head_cols(hh)].astype(F32)
                vt_ref[head_rows(hh, HEAD_DIM), rows] = blk.T.astype(BF16)
            vt_ref[hh * ATT_VROWS + HEAD_DIM:(hh + 1) * ATT_VROWS, :] = ones_row

    def prep(tile, qt):
        qs = []
        for hh in range(ATT_HG):
            qh = q_ref[tile_rows(tile), head_cols(hh)]
            if is_fox:
                head = hg * ATT_HG + hh
                lane = lax.broadcasted_iota(jnp.int32, (t, LANES), 1)
                pick = ((lane >= 3 * head) & (lane < 3 * head + 3)).astype(BF16)
                qh = jnp.concatenate([qh, pick], axis=1)
            qs.append(qh)
        if not is_fox:
            blk_id = lax.broadcasted_iota(jnp.int32, (nblk, t), 0)
            q_blk = qt * ATT_SUB + lax.broadcasted_iota(jnp.int32, (nblk, t), 1) // MOBA_BLOCK
            past = blk_id < q_blk
            blk_f = blk_id.astype(F32)
            for hh in range(ATT_HG):
                kh, km, kl = _split3(x_ref[:, head_cols(hh)])
                gate = _dot_nt(kh, qs[hh]) + _dot_nt(km, qs[hh]) + _dot_nt(kl, qs[hh])
                gate = jnp.where(past, gate, -jnp.inf)
                picked = jnp.zeros((nblk, t), jnp.bool_)
                for _ in range(MOBA_TOP_K):
                    top = jnp.max(gate, axis=0, keepdims=True)
                    first = jnp.min(jnp.where(gate == top, blk_f, float(nblk)), axis=0, keepdims=True)
                    pick = blk_f == first
                    picked = picked | pick
                    gate = jnp.where(pick, -jnp.inf, gate)
                visible = (picked & past) | (blk_id == q_blk)
                bias = jnp.where(visible, 0.0, NEG_BIG)
                bias = jnp.concatenate([bias, jnp.zeros((LANES - nblk, t), F32)], axis=0)
                qs[hh] = jnp.concatenate([qs[hh], bias.T.astype(BF16)], axis=1)
        return qs

    def scores(qs, hh, start):
        kb = k_ref[pl.ds(start, t), head_cols(hh)]
        if is_fox:
            extra = x_ref[pl.ds(start, t), :]
        else:
            key_blk = start // MOBA_BLOCK + lax.broadcasted_iota(jnp.int32, (t, LANES), 0) // MOBA_BLOCK
            extra = (lax.broadcasted_iota(jnp.int32, (t, LANES), 1) == key_blk).astype(BF16)
        return _dot_nt(jnp.concatenate([kb, extra], axis=1), qs[hh])

    def vt_chunk(hh, start):
        return vt_ref[head_rows(hh), pl.ds(start, t)]

    def produce(qs, dst_ref, start, diagonal):
        cmax = []
        for hh in range(ATT_HG):
            s = scores(qs, hh, start)
            if diagonal:
                key = lax.broadcasted_iota(jnp.int32, (t, t), 0)
                qry = lax.broadcasted_iota(jnp.int32, (t, t), 1)
                s = jnp.where(key <= qry, s, NEG_BIG)
            dst_ref[hh] = s
            cmax.append(jnp.max(s, axis=0, keepdims=True))
        return tuple(cmax)

    def consume(tile, src_ref, start, cmax, stats):
        new = []
        for hh in range(ATT_HG):
            m = stats[hh]
            m_new = jnp.maximum(m, cmax[hh])
            alpha = jnp.exp2(m - m_new)
            p = jnp.exp2((src_ref[hh] - m_new).astype(BF16))
            acc_ref[tile, head_rows(hh), :] = (alpha * acc_ref[tile, head_rows(hh), :]
                                               + _dot(vt_chunk(hh, start), p))
            new.append(m_new)
        return tuple(new)

    def sweep(tile, qs, qt, even, odd, cmax, stats):
        def step(dst_ref, src_ref):
            def run(i, cmax, stats):
                nxt = pl.multiple_of(i * t, t)
                cur = pl.multiple_of(jnp.where(i == 0, qt, i - 1) * t, t)
                cmax_next = produce(qs, dst_ref, nxt, False)
                return cmax_next, consume(tile, src_ref, cur, cmax, stats)
            return run

        def body(i, carry):
            return lax.cond(i % 2 == 0, step(*even), step(*odd), i, *carry)

        return lax.fori_loop(0, qt, body, (cmax, stats))

    def finish(tile, src_ref, qt, cmax, stats):
        last = pl.multiple_of(jnp.maximum(qt - 1, 0) * t, t)
        consume(tile, src_ref, last, cmax, stats)
        for hh in range(ATT_HG):
            l = acc_ref[tile, hh * ATT_VROWS + HEAD_DIM:hh * ATT_VROWS + HEAD_DIM + 1, :]
            o_ref[tile_rows(tile), head_cols(hh)] = (
                acc_ref[tile, head_rows(hh, HEAD_DIM), :] * (1.0 / l)).T.astype(BF16)

    def fresh_stats():
        return tuple(jnp.full((1, t), NEG_BIG, F32) for _ in range(ATT_HG))

    def diagonal(tile, qs, qt, dst_ref):
        acc_ref[tile] = jnp.zeros((ATT_HG * ATT_VROWS, t), F32)
        return produce(qs, dst_ref, pl.multiple_of(qt * t, t), True)

    qt0 = pair * ATT_TILES
    qs0 = prep(0, qt0)
    cmax0 = diagonal(0, qs0, qt0, sa_ref)
    cmax0, stats0 = sweep(0, qs0, qt0, (sb_ref, sa_ref), (sa_ref, sb_ref), cmax0, fresh_stats())

    qt1 = qt0 + 1
    qs1 = prep(1, qt1)
    cmax1 = diagonal(1, qs1, qt1, sb_ref)
    finish(0, sa_ref, qt0, cmax0, stats0)
    cmax1, stats1 = sweep(1, qs1, qt1, (sa_ref, sb_ref), (sb_ref, sa_ref), cmax1, fresh_stats())
    finish(1, sa_ref, qt1, cmax1, stats1)


def _rider_specs(riders, n_steps, step_of):
    specs = []
    for w in riders:
        rows = w.shape[0] // n_steps
        assert rows * n_steps == w.shape[0] and rows % 16 == 0
        specs.append(pl.BlockSpec((rows, w.shape[1]), lambda *idx: (step_of(*idx), 0)))
    return specs


def _attention(proj, kmean, faug, moba_riders, batch, seq):
    assert ATT_TILES == 2
    n = batch * seq
    npair = seq // (ATT_TILES * ATT_T)
    nblk = seq // MOBA_BLOCK
    hw = ATT_HG * HEAD_DIM
    n_hg = N_HEADS // ATT_HG
    grid = (batch, n_hg, npair)
    params = pltpu.CompilerParams(
        dimension_semantics=("arbitrary", "arbitrary", "arbitrary"),
        vmem_limit_bytes=VMEM_LIMIT)

    def q_spec(col0):
        return pl.BlockSpec((ATT_TILES * ATT_T, hw), lambda b, h, i: (b * npair + i, col0 + h))

    def kv_spec(col0):
        return pl.BlockSpec((seq, hw), lambda b, h, i: (b, col0 + h))

    region = WIDTH // hw
    out_spec = pl.BlockSpec((ATT_TILES * ATT_T, hw), lambda b, h, i: (b * npair + i, h))
    out_shape = jax.ShapeDtypeStruct((n, WIDTH), BF16)
    vrows = ATT_HG * ATT_VROWS
    common = [pltpu.VMEM((vrows, seq), BF16), pltpu.VMEM((ATT_TILES, vrows, ATT_T), F32),
              pltpu.VMEM((ATT_HG, ATT_T, ATT_T), F32), pltpu.VMEM((ATT_HG, ATT_T, ATT_T), F32)]

    def call(is_fox, name, specs, args, riders, extra_scratch):
        rider_specs = _rider_specs(riders, batch * n_hg * npair,
                                   lambda b, h, i: (b * n_hg + h) * npair + i)
        return pl.pallas_call(
            functools.partial(_attn_kernel, is_fox, len(riders)),
            grid=grid,
            in_specs=specs + rider_specs,
            out_specs=[out_spec] + rider_specs,
            out_shape=[out_shape] + [jax.ShapeDtypeStruct(w.shape, BF16) for w in riders],
            scratch_shapes=common + extra_scratch,
            compiler_params=params,
            name=name,
        )(*args, *riders)

    o_moba, *moba_cast = call(
        False, "moba_attn",
        [q_spec(0), kv_spec(region), kv_spec(2 * region),
         pl.BlockSpec((nblk, hw), lambda b, h, i: (b, h))],
        (proj, proj, proj, kmean), moba_riders, [])
    o_fox, = call(
        True, "fox_attn",
        [q_spec(3 * region), kv_spec(4 * region), kv_spec(5 * region),
         pl.BlockSpec((seq, LANES), lambda b, h, i: (b, 0))],
        (proj, proj, proj, faug), (), [])
    return o_moba, o_fox, moba_cast


MIX_TM = 256


def _sigmoid(z):
    return 1.0 / (1.0 + jnp.exp(-z))


def _mix_kernel(n_cast, om_ref, of_ref, ga_ref, gb_ref, x_ref, wm_ref, wf_ref, wo_ref, g_ref, *rest):
    cast_src = rest[:n_cast]
    out_ref = rest[n_cast]
    cast_dst = rest[n_cast + 1:]
    for src, dst in zip(cast_src, cast_dst):
        dst[...] = src[...].astype(BF16)
    y_m = _dot(om_ref[...], wm_ref[...])
    y_f = _dot(of_ref[...], wf_ref[...])
    merged = _sigmoid(ga_ref[...].astype(F32)) * y_m + _sigmoid(gb_ref[...].astype(F32)) * y_f
    mixed = _dot(merged.astype(BF16), wo_ref[...])
    out_ref[...] = x_ref[...] + mixed * _rms_scale(mixed) * g_ref[...]


def _mix_out(o_moba, o_fox, proj, x2, wm, wf, wo, g, riders):
    n = x2.shape[0]
    ga_blk = QKV_COLS // D_MODEL
    const = dict(pipeline_mode=pl.Buffered(1))
    rider_specs = _rider_specs(riders, n // MIX_TM, lambda i: i)
    x1, *cast = pl.pallas_call(
        functools.partial(_mix_kernel, len(riders)),
        grid=(n // MIX_TM,),
        in_specs=[
            pl.BlockSpec((MIX_TM, WIDTH), lambda i: (i, 0)),
            pl.BlockSpec((MIX_TM, WIDTH), lambda i: (i, 0)),
            pl.BlockSpec((MIX_TM, D_MODEL), lambda i: (i, ga_blk)),
            pl.BlockSpec((MIX_TM, D_MODEL), lambda i: (i, ga_blk + 1)),
            pl.BlockSpec((MIX_TM, D_MODEL), lambda i: (i, 0)),
            pl.BlockSpec((WIDTH, D_MODEL), lambda i: (0, 0), **const),
            pl.BlockSpec((WIDTH, D_MODEL), lambda i: (0, 0), **const),
            pl.BlockSpec((D_MODEL, D_MODEL), lambda i: (0, 0), **const),
            pl.BlockSpec((1, D_MODEL), lambda i: (0, 0)),
        ] + rider_specs,
        out_specs=[pl.BlockSpec((MIX_TM, D_MODEL), lambda i: (i, 0))] + rider_specs,
        out_shape=[jax.ShapeDtypeStruct((n, D_MODEL), F32)]
        + [jax.ShapeDtypeStruct(w.shape, BF16) for w in riders],
        compiler_params=pltpu.CompilerParams(
            dimension_semantics=("arbitrary",), vmem_limit_bytes=VMEM_LIMIT),
        name="mix_out",
    )(o_moba, o_fox, proj, proj, x2, wm, wf, wo, g, *riders)
    return x1, cast


MLP_TM = 512
MLP_TF = 1024


def _mlp_kernel(x_ref, gpre_ref, wu_ref, wd_ref, gpost_ref, out_ref, h_ref, acc_ref):
    f = pl.program_id(1)

    @pl.when(f == 0)
    def _():
        x = x_ref[...]
        h_ref[...] = (x * _rms_scale(x) * gpre_ref[...]).astype(BF16)
        acc_ref[...] = jnp.zeros_like(acc_ref)

    u = _dot(h_ref[...], wu_ref[...])
    a = jnp.square(jnp.maximum(u, 0.0)).astype(BF16)
    acc_ref[...] += _dot(a, wd_ref[...])

    @pl.when(f == pl.num_programs(1) - 1)
    def _():
        mo = acc_ref[...]
        out_ref[...] = x_ref[...] + mo * _rms_scale(mo) * gpost_ref[...]


def _mlp(x1, gpre, wu, wd, gpost):
    n = x1.shape[0]
    return pl.pallas_call(
        _mlp_kernel,
        grid=(n // MLP_TM, D_FF // MLP_TF),
        in_specs=[
            pl.BlockSpec((MLP_TM, D_MODEL), lambda i, f: (i, 0)),
            pl.BlockSpec((1, D_MODEL), lambda i, f: (0, 0)),
            pl.BlockSpec((D_MODEL, MLP_TF), lambda i, f: (0, f)),
            pl.BlockSpec((MLP_TF, D_MODEL), lambda i, f: (f, 0)),
            pl.BlockSpec((1, D_MODEL), lambda i, f: (0, 0)),
        ],
        out_specs=pl.BlockSpec((MLP_TM, D_MODEL), lambda i, f: (i, 0)),
        out_shape=jax.ShapeDtypeStruct((n, D_MODEL), F32),
        scratch_shapes=[pltpu.VMEM((MLP_TM, D_MODEL), BF16), pltpu.VMEM((MLP_TM, D_MODEL), F32)],
        compiler_params=pltpu.CompilerParams(
            dimension_semantics=("arbitrary", "arbitrary"), vmem_limit_bytes=VMEM_LIMIT),
        name="mlp",
    )(x1, gpre, wu, wd, gpost)


@functools.lru_cache(maxsize=None)
def _rope_tables(seq):
    f32 = np.float32
    inv_freq = np.power(f32(ROPE_THETA), -np.arange(0, ROPE_DIM, 2, dtype=f32) / f32(ROPE_DIM)).astype(f32)
    ang = np.arange(seq, dtype=f32)[:, None] * inv_freq[None, :]
    cos, sin = np.cos(ang).astype(f32), np.sin(ang).astype(f32)
    gap = HEAD_DIM // 2 - ROPE_HALF
    ones = np.ones((seq, gap), f32)
    zeros = np.zeros((seq, gap), f32)
    cos_t = np.concatenate([cos, ones, cos, ones], axis=1)
    sin_t = np.concatenate([-sin, zeros, sin, zeros], axis=1)
    return cos_t, sin_t


def _layer(x2, w_in, b_forget, w_bm, w_bf, w_out, g_mix_pre, g_mix_post,
           w_up, w_down, g_mlp_pre, g_mlp_post, batch, seq):
    b_row = jnp.pad(b_forget, (0, LANES - N_HEADS)).reshape(1, LANES)
    cos_t, sin_t = _rope_tables(seq)

    proj, ff, kmean = _in_proj(x2, g_mix_pre.reshape(1, -1), w_in, cos_t, sin_t, seq)
    faug = _fox_bias(ff, b_row, batch, seq)
    kmean = kmean.reshape(batch * (seq // MOBA_BLOCK), WIDTH)
    o_moba, o_fox, (wm, wf, wo) = _attention(proj, kmean, faug, (w_bm, w_bf, w_out), batch, seq)
    x1, (wu, wd) = _mix_out(o_moba, o_fox, proj, x2, wm, wf, wo, g_mix_post.reshape(1, -1),
                            (w_up, w_down))
    return _mlp(x1, g_mlp_pre.reshape(1, -1), wu, wd, g_mlp_post.reshape(1, -1))


def kernel(x, w_in, b_forget, w_branch_moba, w_branch_fox, w_out, g_mix_pre, g_mix_post,
           w_up, w_down, g_mlp_pre, g_mlp_post):
    batch, seq, d = x.shape
    assert d == D_MODEL and seq % IN_TM == 0 and seq % (ATT_TILES * ATT_T) == 0
    assert w_in.shape[-1] == PROJ_COLS + N_HEADS
    x2 = x.reshape(batch * seq, d)
    for l in range(w_in.shape[0]):
        x2 = _layer(x2, w_in[l], b_forget[l], w_branch_moba[l], w_branch_fox[l], w_out[l],
                    g_mix_pre[l], g_mix_post[l], w_up[l], w_down[l], g_mlp_pre[l],
                    g_mlp_post[l], batch, seq)
    return x2.reshape(batch, seq, d)
```

```python
import functools
import math

import numpy as np
import jax
import jax.numpy as jnp
from jax import lax
from jax.experimental import pallas as pl
from jax.experimental.pallas import tpu as pltpu

D_MODEL = 2048
HEAD_DIM = 128
N_HEADS = 8
WIDTH = N_HEADS * HEAD_DIM
MOBA_BLOCK = 256
MOBA_TOP_K = 3
ROPE_THETA = 500000.0
ROPE_DIM = HEAD_DIM // 4
ROPE_HALF = ROPE_DIM // 2
D_FF = 4 * D_MODEL
RMS_EPS = 1e-6
NEG_BIG = -1e30
LANES = 128
SUBLANES = 8

QKV_COLS = 6 * WIDTH
GATE_COLS = 2 * D_MODEL
PROJ_COLS = QKV_COLS + GATE_COLS

VMEM_LIMIT = 56 * 1024 * 1024

BF16 = jnp.bfloat16
F32 = jnp.float32
LOG2E = math.log2(math.e)
Q_SCALE = HEAD_DIM ** -0.5 * LOG2E


def _rms_scale(x):
    return lax.rsqrt(jnp.mean(x * x, axis=-1, keepdims=True) + RMS_EPS)


def _split3(x):
    hi = x.astype(BF16)
    r1 = x - hi.astype(F32)
    mid = r1.astype(BF16)
    lo = (r1 - mid.astype(F32)).astype(BF16)
    return hi, mid, lo


def _dot(a, b):
    return jnp.dot(a, b, preferred_element_type=F32)


def _dot_nt(a, b):
    return lax.dot_general(a, b, (((1,), (1,)), ((), ())), preferred_element_type=F32)


IN_TM = 1024
IN_TN = 1024
IN_ROT_TILES = 2 * WIDTH // IN_TN


def _permute_rotary_rows(blk):
    mid = HEAD_DIM // 2 + ROPE_HALF
    parts = []
    for hd in range(blk.shape[0] // HEAD_DIM):
        b = hd * HEAD_DIM
        parts += [blk[b:b + ROPE_HALF], blk[b + ROPE_DIM:b + mid],
                  blk[b + ROPE_HALF:b + ROPE_DIM], blk[b + mid:b + HEAD_DIM]]
    return jnp.concatenate(parts, axis=0)


IN_W_SLOTS = 3
IN_VMEM_LIMIT = 60 * 1024 * 1024


def _in_proj_kernel(x_ref, g_ref, wt_hbm, wff_ref, cos_ref, sin_ref,
                    proj_ref, ff_ref, kmean_ref, h_ref, wbuf_ref, wsem):
    j = pl.program_id(1)
    n_j = pl.num_programs(1)
    step = pl.program_id(0) * n_j + j
    n_steps = pl.num_programs(0) * n_j

    def window(s):
        col = s % n_j
        n_qkv = QKV_COLS // IN_TN
        row = pl.multiple_of(
            jnp.where(col < n_qkv, col * IN_TN, QKV_COLS + N_HEADS + (col - n_qkv) * IN_TN), SUBLANES)
        slot = s % IN_W_SLOTS
        return pltpu.make_async_copy(wt_hbm.at[pl.ds(row, IN_TN), :], wbuf_ref.at[slot],
                                     wsem.at[slot])

    @pl.when(step == 0)
    def _():
        for s in range(IN_W_SLOTS - 1):
            window(s).start()

    @pl.when(step + IN_W_SLOTS - 1 < n_steps)
    def _():
        window(step + IN_W_SLOTS - 1).start()

    window(step).wait()
    w_ref = wbuf_ref.at[step % IN_W_SLOTS]

    @pl.when(j == 0)
    def _():
        x = x_ref[...]
        h = (x * _rms_scale(x) * g_ref[...]).astype(BF16)
        h_ref[...] = h
        ff_ref[...] = _dot_nt(h, wff_ref[...].astype(BF16))

    def tile(permute):
        w = w_ref[...]
        w = (_permute_rotary_rows(w) if permute else w).astype(BF16)
        return _dot_nt(h_ref[...], w)

    def rotary(t):
        outs = []
        for hd in range(IN_TN // HEAD_DIM):
            th = t[:, hd * HEAD_DIM:(hd + 1) * HEAD_DIM]
            outs.append(th * cos_ref[...] + pltpu.roll(th, HEAD_DIM // 2, 1) * sin_ref[...])
        return jnp.concatenate(outs, axis=1)

    @pl.when(j == 0)
    def _():
        proj_ref[...] = (rotary(tile(True)) * Q_SCALE).astype(BF16)

    @pl.when(j == 1)
    def _():
        kr = rotary(tile(True))
        proj_ref[...] = kr.astype(BF16)
        nblk = IN_TM // MOBA_BLOCK
        means = [jnp.mean(kr[b * MOBA_BLOCK:(b + 1) * MOBA_BLOCK, :], axis=0, keepdims=True)
                 for b in range(nblk)]
        kmean_ref[0] = jnp.concatenate(means, axis=0)

    @pl.when(j == 3)
    def _():
        proj_ref[...] = (tile(False) * Q_SCALE).astype(BF16)

    @pl.when((j == 2) | (j >= 4))
    def _():
        proj_ref[...] = tile(False).astype(BF16)


def _in_proj(x2, g, w_in, cos_t, sin_t, seq):
    assert IN_ROT_TILES == 2
    n = x2.shape[0]
    seq_tiles = seq // IN_TM
    assert (QKV_COLS + N_HEADS) % SUBLANES == 0 and n // IN_TM * (PROJ_COLS // IN_TN) >= IN_W_SLOTS
    wt = w_in.T

    return pl.pallas_call(
        _in_proj_kernel,
        grid=(n // IN_TM, PROJ_COLS // IN_TN),
        in_specs=[
            pl.BlockSpec((IN_TM, D_MODEL), lambda i, j: (i, 0)),
            pl.BlockSpec((1, D_MODEL), lambda i, j: (0, 0)),
            pl.BlockSpec(memory_space=pl.ANY),
            pl.BlockSpec((LANES, D_MODEL), lambda i, j: (QKV_COLS // LANES, 0),
                         pipeline_mode=pl.Buffered(1)),
            pl.BlockSpec((IN_TM, HEAD_DIM), lambda i, j: (i % seq_tiles, 0)),
            pl.BlockSpec((IN_TM, HEAD_DIM), lambda i, j: (i % seq_tiles, 0)),
        ],
        out_specs=[
            pl.BlockSpec((IN_TM, IN_TN), lambda i, j: (i, j)),
            pl.BlockSpec((IN_TM, LANES), lambda i, j: (i, 0)),
            pl.BlockSpec((1, IN_TM // MOBA_BLOCK, WIDTH), lambda i, j: (i, 0, 0)),
        ],
        out_shape=[
            jax.ShapeDtypeStruct((n, PROJ_COLS), BF16),
            jax.ShapeDtypeStruct((n, LANES), F32),
            jax.ShapeDtypeStruct((n // IN_TM, IN_TM // MOBA_BLOCK, WIDTH), F32),
        ],
        scratch_shapes=[pltpu.VMEM((IN_TM, D_MODEL), BF16),
                        pltpu.VMEM((IN_W_SLOTS, IN_TN, D_MODEL), F32),
                        pltpu.SemaphoreType.DMA((IN_W_SLOTS,))],
        compiler_params=pltpu.CompilerParams(
            dimension_semantics=("arbitrary", "arbitrary"), vmem_limit_bytes=IN_VMEM_LIMIT),
        name="in_proj",
    )(x2, g, wt, wt, cos_t, sin_t)


FB_CHUNK = 256


def _fox_bias_kernel(ff_ref, b_ref, out_ref):
    seq = ff_ref.shape[0]
    r = lax.broadcasted_iota(jnp.int32, (FB_CHUNK, FB_CHUNK), 0)
    c = lax.broadcasted_iota(jnp.int32, (FB_CHUNK, FB_CHUNK), 1)
    ltri = (c <= r).astype(BF16)
    pr = lax.broadcasted_iota(jnp.int32, (LANES, LANES), 0)
    pc = lax.broadcasted_iota(jnp.int32, (LANES, LANES), 1)
    place = [((pc == 3 * pr + t) & (pr < N_HEADS)).astype(BF16) for t in range(3)]

    lane = lax.broadcasted_iota(jnp.int32, (1, LANES), 1)
    b_row = jnp.zeros((1, LANES), F32)
    for h in range(N_HEADS):
        b_row = jnp.where(lane == h, b_ref[h], b_row)

    local = []
    for ci in range(seq // FB_CHUNK):
        z = ff_ref[ci * FB_CHUNK:(ci + 1) * FB_CHUNK, :] + b_row
        logf = -(jnp.maximum(-z, 0.0) + jnp.log1p(jnp.exp(-jnp.abs(z))))
        hi, mid, lo = _split3(logf)
        local.append(_dot(ltri, hi) + _dot(ltri, mid) + _dot(ltri, lo))
    carry = jnp.zeros((1, LANES), F32)
    for ci, loc in enumerate(local):
        f = loc + carry
        carry = carry + loc[FB_CHUNK - 1:FB_CHUNK, :]
        nh, nm, nl = _split3(f * (-LOG2E))
        aug = _dot(nh, place[0]) + _dot(nm, place[1]) + _dot(nl, place[2])
        out_ref[ci * FB_CHUNK:(ci + 1) * FB_CHUNK, :] = aug.astype(BF16)


def _fox_bias(ff, b_forget, batch, seq):
    return pl.pallas_call(
        _fox_bias_kernel,
        grid=(batch,),
        in_specs=[
            pl.BlockSpec((seq, LANES), lambda b: (b, 0)),
            pl.BlockSpec(memory_space=pltpu.SMEM),
        ],
        out_specs=pl.BlockSpec((seq, LANES), lambda b: (b, 0)),
        out_shape=jax.ShapeDtypeStruct((batch * seq, LANES), BF16),
        compiler_params=pltpu.CompilerParams(
            dimension_semantics=("arbitrary",), vmem_limit_bytes=VMEM_LIMIT),
        name="fox_bias",
    )(ff, b_forget)


ATT_T = 512
ATT_HG = 4
ATT_TILES = 2
ATT_SUB = ATT_T // MOBA_BLOCK
ATT_VROWS = HEAD_DIM + 16


def _attn_kernel(is_fox, n_cast, *refs):
    q_ref, k_ref, v_ref, x_ref = refs[:4]
    cast_src = refs[4:4 + n_cast]
    o_ref = refs[4 + n_cast]
    cast_dst = refs[5 + n_cast:5 + 2 * n_cast]
    vt_ref, acc_ref, sa_ref, sb_ref = refs[5 + 2 * n_cast:]
    hg = pl.program_id(1)
    pair = pl.program_id(2)
    seq = k_ref.shape[0]
    nblk = seq // MOBA_BLOCK
    t = ATT_T

    for src, dst in zip(cast_src, cast_dst):
        dst[...] = src[...].astype(BF16)

    def head_cols(hh):
        return slice(hh * HEAD_DIM, (hh + 1) * HEAD_DIM)

    def tile_rows(tile):
        return slice(tile * t, (tile + 1) * t)

    def head_rows(hh, n=ATT_VROWS):
        return slice(hh * ATT_VROWS, hh * ATT_VROWS + n)

    @pl.when(pair == 0)
    def _():
        ones_row = (lax.broadcasted_iota(jnp.int32, (ATT_VROWS - HEAD_DIM, seq), 0) == 0).astype(BF16)
        for hh in range(ATT_HG):
            for ci in range(nblk):
                rows = slice(ci * MOBA_BLOCK, (ci + 1) * MOBA_BLOCK)
                blk = v_ref[rows, head_cols(hh)].astype(F32)
                vt_ref[head_rows(hh, HEAD_DIM), rows] = blk.T.astype(BF16)
            vt_ref[hh * ATT_VROWS + HEAD_DIM:(hh + 1) * ATT_VROWS, :] = ones_row

    def prep(tile, qt):
        qs = []
        for hh in range(ATT_HG):
            qh = q_ref[tile_rows(tile), head_cols(hh)]
            if is_fox:
                head = hg * ATT_HG + hh
                lane = lax.broadcasted_iota(jnp.int32, (t, LANES), 1)
                pick = ((lane >= 3 * head) & (lane < 3 * head + 3)).astype(BF16)
                qh = jnp.concatenate([qh, pick], axis=1)
            qs.append(qh)
        if not is_fox:
            blk_id = lax.broadcasted_iota(jnp.int32, (nblk, t), 0)
            q_blk = qt * ATT_SUB + lax.broadcasted_iota(jnp.int32, (nblk, t), 1) // MOBA_BLOCK
            past = blk_id < q_blk
            blk_f = blk_id.astype(F32)
            for hh in range(ATT_HG):
                g3 = _dot_nt(jnp.concatenate(_split3(x_ref[:, head_cols(hh)]), axis=0), qs[hh])
                gate = g3[:nblk] + g3[nblk:2 * nblk] + g3[2 * nblk:]
                gate = jnp.where(past, gate, -jnp.inf)
                picked = jnp.zeros((nblk, t), jnp.bool_)
                for _ in range(MOBA_TOP_K):
                    top = jnp.max(gate, axis=0, keepdims=True)
                    first = jnp.min(jnp.where(gate == top, blk_f, float(nblk)), axis=0, keepdims=True)
                    pick = blk_f == first
                    picked = picked | pick
                    gate = jnp.where(pick, -jnp.inf, gate)
                visible = (picked & past) | (blk_id == q_blk)
                bias = jnp.where(visible, 0.0, NEG_BIG)
                bias = jnp.concatenate([bias, jnp.zeros((LANES - nblk, t), F32)], axis=0)
                qs[hh] = jnp.concatenate([qs[hh], bias.T.astype(BF16)], axis=1)
        return qs

    def scores(qs, hh, start):
        kb = k_ref[pl.ds(start, t), head_cols(hh)]
        if is_fox:
            extra = x_ref[pl.ds(start, t), :]
        else:
            key_blk = start // MOBA_BLOCK + lax.broadcasted_iota(jnp.int32, (t, LANES), 0) // MOBA_BLOCK
            extra = (lax.broadcasted_iota(jnp.int32, (t, LANES), 1) == key_blk).astype(BF16)
        return _dot_nt(jnp.concatenate([kb, extra], axis=1), qs[hh])

    def vt_chunk(hh, start):
        return vt_ref[head_rows(hh), pl.ds(start, t)]

    def produce(qs, dst_ref, start, diagonal):
        cmax = []
        for hh in range(ATT_HG):
            s = scores(qs, hh, start)
            if diagonal:
                key = lax.broadcasted_iota(jnp.int32, (t, t), 0)
                qry = lax.broadcasted_iota(jnp.int32, (t, t), 1)
                s = jnp.where(key <= qry, s, NEG_BIG)
            dst_ref[hh] = s
            cmax.append(jnp.max(s, axis=0, keepdims=True))
        return tuple(cmax)

    def consume(tile, src_ref, start, cmax, stats):
        new = []
        for hh in range(ATT_HG):
            m = stats[hh]
            m_new = jnp.maximum(m, cmax[hh])
            alpha = jnp.exp2(m - m_new)
            p = jnp.exp2((src_ref[hh] - m_new).astype(BF16))
            acc_ref[tile, head_rows(hh), :] = (alpha * acc_ref[tile, head_rows(hh), :]
                                               + _dot(vt_chunk(hh, start), p))
            new.append(m_new)
        return tuple(new)

    def sweep(tile, qs, qt, even, odd, cmax, stats):
        def step(dst_ref, src_ref):
            def run(i, cmax, stats):
                nxt = pl.multiple_of(i * t, t)
                cur = pl.multiple_of(jnp.where(i == 0, qt, i - 1) * t, t)
                cmax_next = produce(qs, dst_ref, nxt, False)
                return cmax_next, consume(tile, src_ref, cur, cmax, stats)
            return run

        def body(i, carry):
            return lax.cond(i % 2 == 0, step(*even), step(*odd), i, *carry)

        return lax.fori_loop(0, qt, body, (cmax, stats))

    def finish(tile, src_ref, qt, cmax, stats):
        last = pl.multiple_of(jnp.maximum(qt - 1, 0) * t, t)
        consume(tile, src_ref, last, cmax, stats)
        for hh in range(ATT_HG):
            l = acc_ref[tile, hh * ATT_VROWS + HEAD_DIM:hh * ATT_VROWS + HEAD_DIM + 1, :]
            o_ref[tile_rows(tile), head_cols(hh)] = (
                acc_ref[tile, head_rows(hh, HEAD_DIM), :] * (1.0 / l)).T.astype(BF16)

    def fresh_stats():
        return tuple(jnp.full((1, t), NEG_BIG, F32) for _ in range(ATT_HG))

    def diagonal(tile, qs, qt, dst_ref):
        acc_ref[tile] = jnp.zeros((ATT_HG * ATT_VROWS, t), F32)
        return produce(qs, dst_ref, pl.multiple_of(qt * t, t), True)

    qt0 = pair * ATT_TILES
    qs0 = prep(0, qt0)
    cmax0 = diagonal(0, qs0, qt0, sa_ref)
    cmax0, stats0 = sweep(0, qs0, qt0, (sb_ref, sa_ref), (sa_ref, sb_ref), cmax0, fresh_stats())

    qt1 = qt0 + 1
    qs1 = prep(1, qt1)
    cmax1 = diagonal(1, qs1, qt1, sb_ref)
    finish(0, sa_ref, qt0, cmax0, stats0)
    cmax1, stats1 = sweep(1, qs1, qt1, (sa_ref, sb_ref), (sb_ref, sa_ref), cmax1, fresh_stats())
    finish(1, sa_ref, qt1, cmax1, stats1)


def _rider_specs(riders, n_steps, step_of):
    specs = []
    for w in riders:
        rows = w.shape[0] // n_steps
        assert rows * n_steps == w.shape[0] and rows % 16 == 0
        specs.append(pl.BlockSpec((rows, w.shape[1]), lambda *idx: (step_of(*idx), 0)))
    return specs


def _attention(proj, kmean, faug, moba_riders, batch, seq):
    assert ATT_TILES == 2
    n = batch * seq
    npair = seq // (ATT_TILES * ATT_T)
    nblk = seq // MOBA_BLOCK
    hw = ATT_HG * HEAD_DIM
    n_hg = N_HEADS // ATT_HG
    grid = (batch, n_hg, npair)
    params = pltpu.CompilerParams(
        dimension_semantics=("arbitrary", "arbitrary", "arbitrary"),
        vmem_limit_bytes=VMEM_LIMIT)

    def q_spec(col0):
        return pl.BlockSpec((ATT_TILES * ATT_T, hw), lambda b, h, i: (b * npair + i, col0 + h))

    def kv_spec(col0):
        return pl.BlockSpec((seq, hw), lambda b, h, i: (b, col0 + h))

    region = WIDTH // hw
    out_spec = pl.BlockSpec((ATT_TILES * ATT_T, hw), lambda b, h, i: (b * npair + i, h))
    out_shape = jax.ShapeDtypeStruct((n, WIDTH), BF16)
    vrows = ATT_HG * ATT_VROWS
    common = [pltpu.VMEM((vrows, seq), BF16), pltpu.VMEM((ATT_TILES, vrows, ATT_T), F32),
              pltpu.VMEM((ATT_HG, ATT_T, ATT_T), F32), pltpu.VMEM((ATT_HG, ATT_T, ATT_T), F32)]

    def call(is_fox, name, specs, args, riders, extra_scratch):
        rider_specs = _rider_specs(riders, batch * n_hg * npair,
                                   lambda b, h, i: (b * n_hg + h) * npair + i)
        return pl.pallas_call(
            functools.partial(_attn_kernel, is_fox, len(riders)),
            grid=grid,
            in_specs=specs + rider_specs,
            out_specs=[out_spec] + rider_specs,
            out_shape=[out_shape] + [jax.ShapeDtypeStruct(w.shape, BF16) for w in riders],
            scratch_shapes=common + extra_scratch,
            compiler_params=params,
            name=name,
        )(*args, *riders)

    o_moba, *moba_cast = call(
        False, "moba_attn",
        [q_spec(0), kv_spec(region), kv_spec(2 * region),
         pl.BlockSpec((nblk, hw), lambda b, h, i: (b, h))],
        (proj, proj, proj, kmean), moba_riders, [])
    o_fox, = call(
        True, "fox_attn",
        [q_spec(3 * region), kv_spec(4 * region), kv_spec(5 * region),
         pl.BlockSpec((seq, LANES), lambda b, h, i: (b, 0))],
        (proj, proj, proj, faug), (), [])
    return o_moba, o_fox, moba_cast


MIX_TM = 256


def _sigmoid(z):
    return 1.0 / (1.0 + jnp.exp(-z))


def _mix_kernel(n_cast, om_ref, of_ref, ga_ref, gb_ref, x_ref, wm_ref, wf_ref, wo_ref, g_ref, *rest):
    cast_src = rest[:n_cast]
    out_ref = rest[n_cast]
    cast_dst = rest[n_cast + 1:]
    for src, dst in zip(cast_src, cast_dst):
        dst[...] = src[...].astype(BF16)
    y_m = _dot(om_ref[...], wm_ref[...])
    y_f = _dot(of_ref[...], wf_ref[...])
    merged = _sigmoid(ga_ref[...].astype(F32)) * y_m + _sigmoid(gb_ref[...].astype(F32)) * y_f
    mixed = _dot(merged.astype(BF16), wo_ref[...])
    out_ref[...] = x_ref[...] + mixed * _rms_scale(mixed) * g_ref[...]


def _mix_out(o_moba, o_fox, proj, x2, wm, wf, wo, g, riders):
    n = x2.shape[0]
    ga_blk = QKV_COLS // D_MODEL
    const = dict(pipeline_mode=pl.Buffered(1))
    rider_specs = _rider_specs(riders, n // MIX_TM, lambda i: i)
    x1, *cast = pl.pallas_call(
        functools.partial(_mix_kernel, len(riders)),
        grid=(n // MIX_TM,),
        in_specs=[
            pl.BlockSpec((MIX_TM, WIDTH), lambda i: (i, 0)),
            pl.BlockSpec((MIX_TM, WIDTH), lambda i: (i, 0)),
            pl.BlockSpec((MIX_TM, D_MODEL), lambda i: (i, ga_blk)),
            pl.BlockSpec((MIX_TM, D_MODEL), lambda i: (i, ga_blk + 1)),
            pl.BlockSpec((MIX_TM, D_MODEL), lambda i: (i, 0)),
            pl.BlockSpec((WIDTH, D_MODEL), lambda i: (0, 0), **const),
            pl.BlockSpec((WIDTH, D_MODEL), lambda i: (0, 0), **const),
            pl.BlockSpec((D_MODEL, D_MODEL), lambda i: (0, 0), **const),
            pl.BlockSpec((1, D_MODEL), lambda i: (0, 0)),
        ] + rider_specs,
        out_specs=[pl.BlockSpec((MIX_TM, D_MODEL), lambda i: (i, 0))] + rider_specs,
        out_shape=[jax.ShapeDtypeStruct((n, D_MODEL), F32)]
        + [jax.ShapeDtypeStruct(w.shape, BF16) for w in riders],
        compiler_params=pltpu.CompilerParams(
            dimension_semantics=("arbitrary",), vmem_limit_bytes=VMEM_LIMIT),
        name="mix_out",
    )(o_moba, o_fox, proj, proj, x2, wm, wf, wo, g, *riders)
    return x1, cast


MLP_TM = 512
MLP_TF = 1024


def _mlp_kernel(x_ref, gpre_ref, wu_ref, wd_ref, gpost_ref, out_ref, h_ref, acc_ref):
    f = pl.program_id(1)
    last = pl.num_programs(1) - 1

    def down(h):
        u = _dot(h, wu_ref[...])
        return _dot(jnp.square(jnp.maximum(u, 0.0)).astype(BF16), wd_ref[...])

    @pl.when(f == 0)
    def _():
        x = x_ref[...]
        h = (x * _rms_scale(x) * gpre_ref[...]).astype(BF16)
        h_ref[...] = h
        acc_ref[...] = down(h)

    @pl.when((f > 0) & (f < last))
    def _():
        acc_ref[...] += down(h_ref[...])

    @pl.when(f == last)
    def _():
        mo = acc_ref[...] + down(h_ref[...])
        out_ref[...] = x_ref[...] + mo * _rms_scale(mo) * gpost_ref[...]


def _mlp(x1, gpre, wu, wd, gpost):
    n = x1.shape[0]
    assert D_FF // MLP_TF >= 2
    return pl.pallas_call(
        _mlp_kernel,
        grid=(n // MLP_TM, D_FF // MLP_TF),
        in_specs=[
            pl.BlockSpec((MLP_TM, D_MODEL), lambda i, f: (i, 0)),
            pl.BlockSpec((1, D_MODEL), lambda i, f: (0, 0)),
            pl.BlockSpec((D_MODEL, MLP_TF), lambda i, f: (0, f)),
            pl.BlockSpec((MLP_TF, D_MODEL), lambda i, f: (f, 0)),
            pl.BlockSpec((1, D_MODEL), lambda i, f: (0, 0)),
        ],
        out_specs=pl.BlockSpec((MLP_TM, D_MODEL), lambda i, f: (i, 0)),
        out_shape=jax.ShapeDtypeStruct((n, D_MODEL), F32),
        scratch_shapes=[pltpu.VMEM((MLP_TM, D_MODEL), BF16), pltpu.VMEM((MLP_TM, D_MODEL), F32)],
        compiler_params=pltpu.CompilerParams(
            dimension_semantics=("arbitrary", "arbitrary"), vmem_limit_bytes=VMEM_LIMIT),
        name="mlp",
    )(x1, gpre, wu, wd, gpost)


@functools.lru_cache(maxsize=None)
def _rope_tables(seq):
    f32 = np.float32
    inv_freq = np.power(f32(ROPE_THETA), -np.arange(0, ROPE_DIM, 2, dtype=f32) / f32(ROPE_DIM)).astype(f32)
    ang = np.arange(seq, dtype=f32)[:, None] * inv_freq[None, :]
    cos, sin = np.cos(ang).astype(f32), np.sin(ang).astype(f32)
    gap = HEAD_DIM // 2 - ROPE_HALF
    ones = np.ones((seq, gap), f32)
    zeros = np.zeros((seq, gap), f32)
    cos_t = np.concatenate([cos, ones, cos, ones], axis=1)
    sin_t = np.concatenate([-sin, zeros, sin, zeros], axis=1)
    return cos_t, sin_t


def _layer(x2, w_in, b_forget, w_bm, w_bf, w_out, g_mix_pre, g_mix_post,
           w_up, w_down, g_mlp_pre, g_mlp_post, batch, seq):
    cos_t, sin_t = _rope_tables(seq)

    proj, ff, kmean = _in_proj(x2, g_mix_pre.reshape(1, -1), w_in, cos_t, sin_t, seq)
    faug = _fox_bias(ff, b_forget, batch, seq)
    kmean = kmean.reshape(batch * (seq // MOBA_BLOCK), WIDTH)
    o_moba, o_fox, (wm, wf, wo) = _attention(proj, kmean, faug, (w_bm, w_bf, w_out), batch, seq)
    x1, (wu, wd) = _mix_out(o_moba, o_fox, proj, x2, wm, wf, wo, g_mix_post.reshape(1, -1),
                            (w_up, w_down))
    return _mlp(x1, g_mlp_pre.reshape(1, -1), wu, wd, g_mlp_post.reshape(1, -1))


def kernel(x, w_in, b_forget, w_branch_moba, w_branch_fox, w_out, g_mix_pre, g_mix_post,
           w_up, w_down, g_mlp_pre, g_mlp_post):
    batch, seq, d = x.shape
    assert d == D_MODEL and seq % IN_TM == 0 and seq % (ATT_TILES * ATT_T) == 0
    assert w_in.shape[-1] == PROJ_COLS + N_HEADS
    x2 = x.reshape(batch * seq, d)
    for l in range(w_in.shape[0]):
        x2 = _layer(x2, w_in[l], b_forget[l], w_branch_moba[l], w_branch_fox[l], w_out[l],
                    g_mix_pre[l], g_mix_post[l], w_up[l], w_down[l], g_mlp_pre[l],
                    g_mlp_post[l], batch, seq)
    return x2.reshape(batch, seq, d)
```
